```python
import math
import jax, jax.numpy as jnp
from jax import lax
import numpy as np


D_MODEL = 2048
BATCH = 4
SEQ = 2048
DEPTH = 1
DEC_BATCH = 8
DEC_SEQ = 8
PAST_LEN = 16384
PAGE_SIZE = 128

MIX_WIDTH = D_MODEL
A_WIDTH = MIX_WIDTH // 2
B_WIDTH = MIX_WIDTH - A_WIDTH
A_QK_DIM = 64
A_V_DIM = 2 * A_QK_DIM
A_HEADS = A_WIDTH // A_V_DIM
B_K_DIM = 128
B_V_DIM = 128
B_HEADS = B_WIDTH // B_V_DIM
B_CONV_DIM = B_HEADS * (2 * B_K_DIM + B_V_DIM)
CONV_W = 4
CHUNK = 64
Q_BLOCK = 128
IN_WIDTH = 2 * A_HEADS * 2 * A_QK_DIM + A_WIDTH + B_CONV_DIM + B_WIDTH + 2 * B_HEADS
D_FF = ((8 * D_MODEL // 3 + 255) // 256) * 256
EPS = 1e-6

kernel_name = 'hymba_diffattn_gdn_macaron_step'


def rmsnorm(x, w):
    xf = x.astype(jnp.float32)
    y = xf * lax.rsqrt(jnp.mean(xf * xf, axis=-1, keepdims=True) + EPS)
    return (y * w.astype(jnp.float32)).astype(x.dtype)


def l2norm(x):
    return x * lax.rsqrt(jnp.sum(x * x, axis=-1, keepdims=True) + 1e-6)


def half_ffn(x, pre_w, post_w, w_gate, w_up, w_down):
    h = rmsnorm(x, pre_w)
    f = (jax.nn.silu(h @ w_gate) * (h @ w_up)) @ w_down
    return x + 0.5 * rmsnorm(f, post_w)


def alibi_slopes():
    return jnp.exp2(-8.0 * jnp.arange(1, A_HEADS + 1, dtype=jnp.float32) / A_HEADS)


def split_in(h):
    sizes = (A_HEADS * 2 * A_QK_DIM, A_HEADS * 2 * A_QK_DIM, A_WIDTH, B_CONV_DIM, B_WIDTH, B_HEADS, B_HEADS)
    offs = [int(o) for o in np.cumsum(sizes)[:-1]]
    return jnp.split(h, offs, axis=-1)


def diff_attn_block(q, k, v, q_pos, k_pos, lam, slopes):
    bsz, tq = q.shape[:2]
    tk = k.shape[1]
    qf = q.astype(jnp.float32).reshape(bsz, tq, A_HEADS, 2, A_QK_DIM)
    kf = k.astype(jnp.float32).reshape(bsz, tk, A_HEADS, 2, A_QK_DIM)
    s = jnp.einsum('bqhmd,bkhmd->bhmqk', qf, kf) * (A_QK_DIM ** -0.5)
    dist = (q_pos[:, None] - k_pos[None, :]).astype(jnp.float32)
    s = s - (slopes[:, None, None] * dist)[None, :, None]
    s = jnp.where(dist >= 0, s, -jnp.inf)
    p = jax.nn.softmax(s, axis=-1)
    pd = p[:, :, 0] - lam * p[:, :, 1]
    return jnp.einsum('bhqk,bkhd->bqhd', pd, v.astype(jnp.float32))


def diff_attn_prompt(q, k, v, lam, slopes):
    bsz, t = q.shape[:2]
    nb = t // Q_BLOCK
    qb = jnp.swapaxes(q.reshape(bsz, nb, Q_BLOCK, A_HEADS, 2 * A_QK_DIM), 0, 1)
    pos = jnp.arange(t, dtype=jnp.int32)
    pb = pos.reshape(nb, Q_BLOCK)
    ob = lax.map(lambda a: diff_attn_block(a[0], k, v, a[1], pos, lam, slopes), (qb, pb))
    return jnp.swapaxes(ob, 0, 1).reshape(bsz, t, A_HEADS, A_V_DIM)


def gated_delta_chunked(q, k, v, g, beta, s0):
    bsz, t, nh, _ = q.shape
    n = -(-t // CHUNK)
    pad = n * CHUNK - t

    def blocks(x):
        x = jnp.pad(x, [(0, 0), (0, pad)] + [(0, 0)] * (x.ndim - 2))
        x = x.reshape((bsz, n, CHUNK) + x.shape[2:])
        x = jnp.moveaxis(x, 3, 2)
        return jnp.moveaxis(x, 1, 0)

    qc, kc, vc, gc, bc = blocks(q), blocks(k), blocks(v), blocks(g), blocks(beta)
    G = jnp.cumsum(gc, axis=-1)
    idx = jnp.arange(CHUNK)
    causal = idx[:, None] >= idx[None, :]
    strict = idx[:, None] > idx[None, :]
    decay = jnp.exp(jnp.where(causal, G[..., :, None] - G[..., None, :], -jnp.inf))
    kbeta = kc * bc[..., None]
    L = jnp.where(strict, jnp.einsum('nbhid,nbhjd->nbhij', kbeta, kc) * decay, 0.0)
    eye = jnp.eye(CHUNK, dtype=jnp.float32)
    tinv = lax.linalg.triangular_solve(L + eye, jnp.broadcast_to(eye, L.shape), left_side=True,
                                       lower=True, unit_diagonal=True)
    u = tinv @ (vc * bc[..., None])
    w = tinv @ (kbeta * jnp.exp(G)[..., None])
    qk = jnp.einsum('nbhid,nbhjd->nbhij', qc, kc) * decay

    def step(S, xs):
        q_i, k_i, u_i, w_i, qk_i, G_i = xs
        v_new = u_i - w_i @ S
        o_i = (q_i * jnp.exp(G_i)[..., None]) @ S + qk_i @ v_new
        g_last = G_i[..., -1:]
        S = S * jnp.exp(g_last)[..., None] + jnp.einsum('bhck,bhcv->bhkv', k_i * jnp.exp(g_last - G_i)[..., None], v_new)
        return S, o_i

    S, o = lax.scan(step, s0, (qc, kc, u, w, qk, G))
    o = jnp.moveaxis(jnp.moveaxis(o, 0, 1), 2, 3).reshape(bsz, n * CHUNK, nh, -1)[:, :t]
    return o, S


def token_mix(x, attend, conv_buf, s0, pre_w, post_w, w_in, conv_w, a_log, dt_bias, delta_norm_w,
              subln_w, w_out, lam_init):
    bsz, t, _ = x.shape
    h = rmsnorm(x, pre_w) @ w_in
    qa, ka, va, qkv_b, z, b_raw, a_raw = split_in(h)
    ka = ka.reshape(bsz, t, A_HEADS, 2 * A_QK_DIM)
    va = va.reshape(bsz, t, A_HEADS, A_V_DIM)
    oa = attend(qa.reshape(bsz, t, A_HEADS, 2 * A_QK_DIM), ka, va)
    oa = (rmsnorm(oa, subln_w) * (1.0 - lam_init)).astype(x.dtype).reshape(bsz, t, A_WIDTH)
    xp = jnp.concatenate([conv_buf.astype(x.dtype), qkv_b], axis=1)
    conv = xp[:, 0:t] * conv_w[0]
    for j in range(1, CONV_W):
        conv = conv + xp[:, j:j + t] * conv_w[j]
    new_buf = xp[:, t:]
    act = jax.nn.silu(conv).astype(jnp.float32)
    qb, kb, vb = jnp.split(act, [B_HEADS * B_K_DIM, 2 * B_HEADS * B_K_DIM], axis=-1)
    qb = l2norm(qb.reshape(bsz, t, B_HEADS, B_K_DIM)) * (B_K_DIM ** -0.5)
    kb = l2norm(kb.reshape(bsz, t, B_HEADS, B_K_DIM))
    vb = vb.reshape(bsz, t, B_HEADS, B_V_DIM)
    beta = jax.nn.sigmoid(b_raw.astype(jnp.float32))
    g = -jnp.exp(a_log.astype(jnp.float32)) * jax.nn.softplus(a_raw.astype(jnp.float32) + dt_bias.astype(jnp.float32))
    ob, s_new = gated_delta_chunked(qb, kb, vb, g, beta, s0.astype(jnp.float32))
    ob = rmsnorm(ob, delta_norm_w) * jax.nn.silu(z.astype(jnp.float32).reshape(bsz, t, B_HEADS, B_V_DIM))
    ob = ob.astype(x.dtype).reshape(bsz, t, B_WIDTH)
    y = jnp.concatenate([oa, ob], axis=-1) @ w_out
    return x + rmsnorm(y, post_w), ka, va, s_new, new_buf


def setup_inputs(seed: int = 0) -> dict:
    key = jax.random.key(seed)
    ks = jax.random.split(key, 32)
    f32 = jnp.float32
    n_pages = PAST_LEN // PAGE_SIZE
    n_used = DEC_BATCH * n_pages
    n_pool = n_used + max(1, n_used // 4)
    perm = jax.random.permutation(ks[0], n_pool)
    page_table = perm[:n_used].reshape(DEC_BATCH, n_pages).astype(jnp.int32)

    def nrm(k, shape, scale):
        return jax.random.normal(k, shape, f32) * scale

    def gain(k, n):
        return 1.0 + 0.02 * jax.random.normal(k, (DEPTH, n), f32)

    dt = jnp.exp(jax.random.uniform(ks[1], (DEPTH, B_HEADS), f32, math.log(1e-3), math.log(1e-1)))
    return {
        'x_prompt': nrm(ks[2], (BATCH, SEQ, D_MODEL), 1.0),
        'x_sample': nrm(ks[3], (DEC_BATCH, DEC_SEQ, D_MODEL), 1.0),
        'cache_k': nrm(ks[4], (DEPTH, n_pool, PAGE_SIZE, A_HEADS, 2 * A_QK_DIM), 1.0),
        'cache_v': nrm(ks[5], (DEPTH, n_pool, PAGE_SIZE, A_HEADS, A_V_DIM), 1.0),
        'state_ssm': nrm(ks[6], (DEPTH, DEC_BATCH, B_HEADS, B_K_DIM, B_V_DIM), 0.1),
        'state_conv': nrm(ks[7], (DEPTH, DEC_BATCH, CONV_W - 1, B_CONV_DIM), 1.0),
        'page_table': page_table,
        'ffn1_pre_w': gain(ks[8], D_MODEL),
        'ffn1_post_w': gain(ks[9], D_MODEL),
        'ffn1_gate': nrm(ks[10], (DEPTH, D_MODEL, D_FF), D_MODEL ** -0.5),
        'ffn1_up': nrm(ks[11], (DEPTH, D_MODEL, D_FF), D_MODEL ** -0.5),
        'ffn1_down': nrm(ks[12], (DEPTH, D_FF, D_MODEL), D_FF ** -0.5),
        'mix_pre_w': gain(ks[13], D_MODEL),
        'mix_post_w': gain(ks[14], D_MODEL),
        'w_in': nrm(ks[15], (DEPTH, D_MODEL, IN_WIDTH), D_MODEL ** -0.5),
        'conv_w': nrm(ks[16], (DEPTH, CONV_W, B_CONV_DIM), CONV_W ** -0.5),
        'a_log': jnp.log(jax.random.uniform(ks[17], (DEPTH, B_HEADS), f32, 1.0, 16.0)),
        'dt_bias': dt + jnp.log(-jnp.expm1(-dt)),
        'delta_norm_w': gain(ks[18], B_V_DIM),
        'lambda_q1': nrm(ks[19], (DEPTH, A_QK_DIM), 0.1),
        'lambda_k1': nrm(ks[20], (DEPTH, A_QK_DIM), 0.1),
        'lambda_q2': nrm(ks[21], (DEPTH, A_QK_DIM), 0.1),
        'lambda_k2': nrm(ks[22], (DEPTH, A_QK_DIM), 0.1),
        'subln_w': gain(ks[23], A_V_DIM),
        'w_out': nrm(ks[24], (DEPTH, MIX_WIDTH, D_MODEL), MIX_WIDTH ** -0.5),
        'ffn2_pre_w': gain(ks[25], D_MODEL),
        'ffn2_post_w': gain(ks[26], D_MODEL),
        'ffn2_gate': nrm(ks[27], (DEPTH, D_MODEL, D_FF), D_MODEL ** -0.5),
        'ffn2_up': nrm(ks[28], (DEPTH, D_MODEL, D_FF), D_MODEL ** -0.5),
        'ffn2_down': nrm(ks[29], (DEPTH, D_FF, D_MODEL), D_FF ** -0.5),
    }


def reference(x_prompt, x_sample, cache_k, cache_v, state_ssm, state_conv, page_table,
              ffn1_pre_w, ffn1_post_w, ffn1_gate, ffn1_up, ffn1_down,
              mix_pre_w, mix_post_w, w_in, conv_w, a_log, dt_bias, delta_norm_w,
              lambda_q1, lambda_k1, lambda_q2, lambda_k2, subln_w, w_out,
              ffn2_pre_w, ffn2_post_w, ffn2_gate, ffn2_up, ffn2_down):
    f32 = jnp.float32
    slopes = alibi_slopes()
    dbsz, tn = x_sample.shape[:2]
    past = page_table.shape[1] * cache_k.shape[2]
    yp, ys = x_prompt, x_sample
    kp_l, vp_l, sp_l, cp_l, ks_l, vs_l, ss_l, cs_l = [], [], [], [], [], [], [], []
    for l in range(DEPTH):
        lam_init = 0.8 - 0.6 * math.exp(-0.3 * l)
        lam = (jnp.exp(jnp.sum(lambda_q1[l].astype(f32) * lambda_k1[l].astype(f32)))
               - jnp.exp(jnp.sum(lambda_q2[l].astype(f32) * lambda_k2[l].astype(f32))) + lam_init)
        ck, cv = cache_k[l], cache_v[l]

        yp = half_ffn(yp, ffn1_pre_w[l], ffn1_post_w[l], ffn1_gate[l], ffn1_up[l], ffn1_down[l])
        ys = half_ffn(ys, ffn1_pre_w[l], ffn1_post_w[l], ffn1_gate[l], ffn1_up[l], ffn1_down[l])

        def attend_prompt(q, k, v):
            return diff_attn_prompt(q, k, v, lam, slopes)

        zero_buf = jnp.zeros((yp.shape[0], CONV_W - 1, B_CONV_DIM), yp.dtype)
        zero_s = jnp.zeros((yp.shape[0], B_HEADS, B_K_DIM, B_V_DIM), f32)
        yp, kp, vp, sp, cp = token_mix(yp, attend_prompt, zero_buf, zero_s, mix_pre_w[l], mix_post_w[l],
                                       w_in[l], conv_w[l], a_log[l], dt_bias[l], delta_norm_w[l],
                                       subln_w[l], w_out[l], lam_init)

        def attend_sample(q, k, v):
            k_past = ck[page_table].reshape(dbsz, past, A_HEADS, 2 * A_QK_DIM)
            v_past = cv[page_table].reshape(dbsz, past, A_HEADS, A_V_DIM)
            k_all = jnp.concatenate([k_past.astype(k.dtype), k], axis=1)
            v_all = jnp.concatenate([v_past.astype(v.dtype), v], axis=1)
            q_pos = past + jnp.arange(tn, dtype=jnp.int32)
            k_pos = jnp.arange(past + tn, dtype=jnp.int32)
            return diff_attn_block(q, k_all, v_all, q_pos, k_pos, lam, slopes)

        ys, ksm, vsm, ssm, csm = token_mix(ys, attend_sample, state_conv[l], state_ssm[l], mix_pre_w[l],
                                           mix_post_w[l], w_in[l], conv_w[l], a_log[l], dt_bias[l],
                                           delta_norm_w[l], subln_w[l], w_out[l], lam_init)

        yp = half_ffn(yp, ffn2_pre_w[l], ffn2_post_w[l], ffn2_gate[l], ffn2_up[l], ffn2_down[l])
        ys = half_ffn(ys, ffn2_pre_w[l], ffn2_post_w[l], ffn2_gate[l], ffn2_up[l], ffn2_down[l])

        kp_l.append(kp)
        vp_l.append(vp)
        sp_l.append(sp.astype(state_ssm.dtype))
        cp_l.append(cp)
        ks_l.append(ksm)
        vs_l.append(vsm)
        ss_l.append(ssm.astype(state_ssm.dtype))
        cs_l.append(csm)

    return (yp, ys, jnp.stack(kp_l), jnp.stack(vp_l), jnp.stack(sp_l), jnp.stack(cp_l),
            jnp.stack(ks_l), jnp.stack(vs_l), jnp.stack(ss_l), jnp.stack(cs_l))
```

```python
import functools
import math

import jax
import jax.numpy as jnp
from jax import lax
from jax.experimental import pallas as pl
from jax.experimental.pallas import tpu as pltpu

F32 = jnp.float32
BF16 = jnp.bfloat16
EPS = 1e-6
NEG_INF = float("-inf")

V7X_VMEM_LIMIT_BYTES = 56 * 1024 * 1024

QK_DIM = 64
HEAD_DIM = 128
N_HEADS = 8
CONV_W = 4
CHUNK = 64
GDN_ROWS = 256


def _dot(a, b):
    return jnp.dot(a, b, preferred_element_type=F32)


def _dot_nt(a, b):
    return lax.dot_general(a, b, (((1,), (1,)), ((), ())), preferred_element_type=F32)


def _split3(x):
    hi = x.astype(BF16)
    r = x - hi.astype(F32)
    mid = r.astype(BF16)
    lo = (r - mid.astype(F32)).astype(BF16)
    return hi, mid, lo


def _dot_sel_l(sel, x):
    hi, mid, lo = _split3(x)
    return _dot(sel, hi) + _dot(sel, mid) + _dot(sel, lo)


def _dot_sel_r(x, sel):
    hi, mid, lo = _split3(x)
    return _dot(hi, sel) + _dot(mid, sel) + _dot(lo, sel)


def _dot_hp(a, b):
    ah = a.astype(BF16)
    al = (a - ah.astype(F32)).astype(BF16)
    bh = b.astype(BF16)
    bl = (b - bh.astype(F32)).astype(BF16)
    return _dot(ah, bh) + _dot(ah, bl) + _dot(al, bh)


def _rms(x, w):
    return x * lax.rsqrt(jnp.mean(x * x, axis=-1, keepdims=True) + EPS) * w


def _silu(x):
    return x * jax.nn.sigmoid(x)


def _softplus(x):
    return jnp.maximum(x, 0.0) + jnp.log1p(jnp.exp(-jnp.abs(x)))


def _lambda_full(lam_ref, lam_init):
    l = lam_ref[...]
    a = jnp.sum(l[0:1] * l[1:2], axis=-1, keepdims=True)
    b = jnp.sum(l[2:3] * l[3:4], axis=-1, keepdims=True)
    return jnp.exp(a) - jnp.exp(b) + lam_init


def _idiv(x, n):
    assert n & (n - 1) == 0
    return x >> (n.bit_length() - 1)


def _imod(x, n):
    assert n & (n - 1) == 0
    return x & (n - 1)


def _pow2_neg(k):
    return lax.bitcast_convert_type((127 - k) << 23, F32)


def _ffn_kernel(x_ref, prew_ref, postw_ref, wg_ref, wu_ref, wd_ref, o_ref, h_ref, *, nf):
    f = pl.program_id(1)

    @pl.when(f == 0)
    def _():
        h_ref[...] = _rms(x_ref[...], prew_ref[...]).astype(BF16)
        o_ref[...] = jnp.zeros_like(o_ref)

    h = h_ref[...]
    g = _dot(h, wg_ref[...])
    u = _dot(h, wu_ref[...])
    o_ref[...] += _dot((_silu(g) * u).astype(BF16), wd_ref[...])

    @pl.when(f == nf - 1)
    def _():
        o_ref[...] = x_ref[...] + 0.5 * _rms(o_ref[...], postw_ref[...])


def _ffn(x, pre_w, post_w, wg, wu, wd, *, tm, tf):
    m, d = x.shape
    ff = wg.shape[1]
    nf = ff // tf
    return pl.pallas_call(
        functools.partial(_ffn_kernel, nf=nf),
        grid=(m // tm, nf),
        in_specs=[
            pl.BlockSpec((tm, d), lambda i, f: (i, 0)),
            pl.BlockSpec((1, d), lambda i, f: (0, 0)),
            pl.BlockSpec((1, d), lambda i, f: (0, 0)),
            pl.BlockSpec((d, tf), lambda i, f: (0, f)),
            pl.BlockSpec((d, tf), lambda i, f: (0, f)),
            pl.BlockSpec((tf, d), lambda i, f: (f, 0)),
        ],
        out_specs=pl.BlockSpec((tm, d), lambda i, f: (i, 0)),
        out_shape=jax.ShapeDtypeStruct((m, d), F32),
        scratch_shapes=[pltpu.VMEM((tm, d), BF16)],
        compiler_params=pltpu.CompilerParams(
            dimension_semantics=("parallel", "arbitrary"), vmem_limit_bytes=V7X_VMEM_LIMIT_BYTES),
        name="ffn",
    )(x, pre_w, post_w, wg, wu, wd)


def _inproj_kernel(x_ref, prew_ref, w_ref, wg_ref, o_ref, og_ref, h_ref):
    j = pl.program_id(1)

    @pl.when(j == 0)
    def _():
        h = _rms(x_ref[...], prew_ref[...]).astype(BF16)
        h_ref[...] = h
        og_ref[...] = _dot(h, wg_ref[...])

    o_ref[...] = _dot(h_ref[...], w_ref[...])


def _inproj(x, pre_w, w_main, w_gates, *, tm, tn):
    m, d = x.shape
    n = w_main.shape[1]
    ng = w_gates.shape[1]
    return pl.pallas_call(
        _inproj_kernel,
        grid=(m // tm, n // tn),
        in_specs=[
            pl.BlockSpec((tm, d), lambda i, j: (i, 0)),
            pl.BlockSpec((1, d), lambda i, j: (0, 0)),
            pl.BlockSpec((d, tn), lambda i, j: (0, j)),
            pl.BlockSpec((d, ng), lambda i, j: (0, 0)),
        ],
        out_specs=[
            pl.BlockSpec((tm, tn), lambda i, j: (i, j)),
            pl.BlockSpec((tm, ng), lambda i, j: (i, 0)),
        ],
        out_shape=[jax.ShapeDtypeStruct((m, n), F32), jax.ShapeDtypeStruct((m, ng), F32)],
        scratch_shapes=[pltpu.VMEM((tm, d), BF16)],
        compiler_params=pltpu.CompilerParams(
            dimension_semantics=("parallel", "arbitrary"), vmem_limit_bytes=V7X_VMEM_LIMIT_BYTES),
        name="inproj",
    )(x, pre_w, w_main, w_gates)


def _outproj_kernel(x_ref, oa_ref, ob_ref, wa_ref, wb_ref, postw_ref, o_ref):
    y = _dot(oa_ref[...], wa_ref[...]) + _dot(ob_ref[...], wb_ref[...])
    o_ref[...] = x_ref[...] + _rms(y, postw_ref[...])


def _outproj(x, oa, ob, w_out, post_w, *, tm):
    m, d = x.shape
    ka = oa.shape[1]
    return pl.pallas_call(
        _outproj_kernel,
        grid=(m // tm,),
        in_specs=[
            pl.BlockSpec((tm, d), lambda i: (i, 0)),
            pl.BlockSpec((tm, ka), lambda i: (i, 0)),
            pl.BlockSpec((tm, ka), lambda i: (i, 0)),
            pl.BlockSpec((ka, d), lambda i: (0, 0)),
            pl.BlockSpec((ka, d), lambda i: (1, 0)),
            pl.BlockSpec((1, d), lambda i: (0, 0)),
        ],
        out_specs=pl.BlockSpec((tm, d), lambda i: (i, 0)),
        out_shape=jax.ShapeDtypeStruct((m, d), F32),
        compiler_params=pltpu.CompilerParams(
            dimension_semantics=("parallel",), vmem_limit_bytes=V7X_VMEM_LIMIT_BYTES),
        name="outproj",
    )(x, oa, ob, w_out, w_out, post_w)


def _attn_prompt_kernel(lam_ref, subw_ref, q_ref, k_ref, v_ref, o_ref, *, t, tq, lam_init):
    h = pl.program_id(1)
    nq = t // tq
    lam = _lambda_full(lam_ref, lam_init)
    slope = _pow2_neg(jnp.full((1, 1), h + 1, jnp.int32))
    lane = lax.broadcasted_iota(jnp.int32, (1, HEAD_DIM), 1)
    first_map = lane < QK_DIM
    r = _imod(lax.broadcasted_iota(jnp.int32, (2 * tq, tq), 0), tq)
    c = lax.broadcasted_iota(jnp.int32, (2 * tq, tq), 1)
    rel_bias = slope * (c - r).astype(F32)
    on_or_below = r >= c
    scale = QK_DIM ** -0.5

    def q_block(i, _):
        q = q_ref[pl.ds(pl.multiple_of(i * tq, tq), tq), :] * scale
        qs = jnp.concatenate(
            [jnp.where(first_map, q, 0.0), jnp.where(first_map, 0.0, q)], axis=0).astype(BF16)

        def update(s, vb, m, l, acc):
            m_new = jnp.maximum(m, jnp.max(s, axis=-1, keepdims=True))
            p = jnp.exp(s - m_new)
            alpha = jnp.exp(m - m_new)
            l = alpha * l + jnp.sum(p, axis=-1, keepdims=True)
            acc = alpha * acc + _dot(p.astype(BF16), vb)
            return m_new, l, acc

        def kv_block(j, carry):
            m, l, acc = carry
            rows = pl.ds(pl.multiple_of(j * tq, tq), tq)
            kb = k_ref[rows, :].astype(BF16)
            vb = v_ref[rows, :].astype(BF16)
            far = slope * ((i - j) * tq).astype(F32)
            s = _dot_nt(qs, kb) + rel_bias - far
            return update(s, vb, m, l, acc)

        init = (jnp.full((2 * tq, 1), NEG_INF, F32), jnp.zeros((2 * tq, 1), F32),
                jnp.zeros((2 * tq, HEAD_DIM), F32))
        m, l, acc = lax.fori_loop(0, i, kv_block, init)
        rows = pl.ds(pl.multiple_of(i * tq, tq), tq)
        kb = k_ref[rows, :].astype(BF16)
        vb = v_ref[rows, :].astype(BF16)
        s = jnp.where(on_or_below, _dot_nt(qs, kb) + rel_bias, NEG_INF)
        m, l, acc = update(s, vb, m, l, acc)
        o = acc / l
        d = o[:tq] - lam * o[tq:]
        o_ref[rows, :] = (_rms(d, subw_ref[...]) * (1.0 - lam_init)).astype(o_ref.dtype)
        return 0

    lax.fori_loop(0, nq, q_block, 0)


def _attn_prompt(lam_rows, subln_w, proj, *, batch, t, lam_init, tq):
    m = proj.shape[0]
    return pl.pallas_call(
        functools.partial(_attn_prompt_kernel, t=t, tq=tq, lam_init=lam_init),
        grid=(batch, N_HEADS),
        in_specs=[
            pl.BlockSpec((4, QK_DIM), lambda b, h: (0, 0)),
            pl.BlockSpec((1, HEAD_DIM), lambda b, h: (0, 0)),
            pl.BlockSpec((t, HEAD_DIM), lambda b, h: (b, h)),
            pl.BlockSpec((t, HEAD_DIM), lambda b, h: (b, N_HEADS + h)),
            pl.BlockSpec((t, HEAD_DIM), lambda b, h: (b, 2 * N_HEADS + h)),
        ],
        out_specs=pl.BlockSpec((t, HEAD_DIM), lambda b, h: (b, h)),
        out_shape=jax.ShapeDtypeStruct((m, N_HEADS * HEAD_DIM), BF16),
        compiler_params=pltpu.CompilerParams(
            dimension_semantics=("parallel", "parallel"), vmem_limit_bytes=V7X_VMEM_LIMIT_BYTES),
        name="attn_prompt",
    )(lam_rows, subln_w, proj, proj, proj)


def _attn_sample_kernel(pt_ref, lam_ref, subw_ref, q_ref, kn_ref, vn_ref, kc_ref, vc_ref, o_ref,
                        qb_ref, bias_ref, m_ref, l_ref, acc_ref, *, n_pages, page, tn, lam_init):
    del pt_ref
    p = pl.program_id(1)
    nrow = 2 * tn * N_HEADS
    ncol = page * N_HEADS
    past = n_pages * page
    row = lax.broadcasted_iota(jnp.int32, (nrow, 1), 0)
    slope = _pow2_neg(_imod(row, N_HEADS) + 1)

    @pl.when(p == 0)
    def _():
        q = q_ref[0] * (QK_DIM ** -0.5)
        lane = lax.broadcasted_iota(jnp.int32, (1, HEAD_DIM), 1)
        first_map = lane < QK_DIM
        qb_ref[...] = jnp.concatenate(
            [jnp.where(first_map, q, 0.0), jnp.where(first_map, 0.0, q)], axis=0).astype(BF16)
        rr = lax.broadcasted_iota(jnp.int32, (nrow, ncol), 0)
        cc = lax.broadcasted_iota(jnp.int32, (nrow, ncol), 1)
        tq_ = _idiv(_imod(rr, tn * N_HEADS), N_HEADS)
        same_head = _imod(rr, N_HEADS) == _imod(cc, N_HEADS)
        bias_ref[...] = jnp.where(same_head, slope * (_idiv(cc, N_HEADS) - tq_).astype(F32), NEG_INF)
        m_ref[...] = jnp.full(m_ref.shape, NEG_INF, F32)
        l_ref[...] = jnp.zeros(l_ref.shape, F32)
        acc_ref[...] = jnp.zeros(acc_ref.shape, F32)

    def update(s, off, vb):
        m_old = m_ref[...]
        m_new = jnp.maximum(m_old, jnp.max(s, axis=-1, keepdims=True) - off)
        pe = jnp.exp(s - (off + m_new))
        alpha = jnp.exp(m_old - m_new)
        l_ref[...] = alpha * l_ref[...] + jnp.sum(pe, axis=-1, keepdims=True)
        acc_ref[...] = alpha * acc_ref[...] + _dot(pe.astype(BF16), vb)
        m_ref[...] = m_new

    kf = kc_ref[0].reshape(ncol, HEAD_DIM).astype(BF16)
    vf = vc_ref[0].reshape(ncol, HEAD_DIM).astype(BF16)
    dist0 = (past - p * page).astype(F32)
    update(_dot_nt(qb_ref[...], kf) + bias_ref[...], slope * dist0, vf)

    @pl.when(p == n_pages - 1)
    def _():
        ncn = tn * N_HEADS
        rr = lax.broadcasted_iota(jnp.int32, (nrow, ncn), 0)
        cc = lax.broadcasted_iota(jnp.int32, (nrow, ncn), 1)
        tq_ = _idiv(_imod(rr, ncn), N_HEADS)
        tk_ = _idiv(cc, N_HEADS)
        ok = (_imod(rr, N_HEADS) == _imod(cc, N_HEADS)) & (tk_ <= tq_)
        bias_new = jnp.where(ok, slope * (tk_ - tq_).astype(F32), NEG_INF)
        s = _dot_nt(qb_ref[...], kn_ref[0].astype(BF16)) + bias_new
        update(s, jnp.zeros((nrow, 1), F32), vn_ref[0].astype(BF16))
        lam = _lambda_full(lam_ref, lam_init)
        o = acc_ref[...] / l_ref[...]
        d = o[:ncn] - lam * o[ncn:]
        o_ref[0] = (_rms(d, subw_ref[...]) * (1.0 - lam_init)).astype(o_ref.dtype)


def _attn_sample(page_table, lam_rows, subln_w, q, k_new, v_new, cache_k, cache_v, *, lam_init):
    n_seq, n_pages = page_table.shape
    page = cache_k.shape[1]
    tn = q.shape[1] // N_HEADS
    nrow = 2 * tn * N_HEADS
    ncol = page * N_HEADS
    grid_spec = pltpu.PrefetchScalarGridSpec(
        num_scalar_prefetch=1,
        grid=(n_seq, n_pages),
        in_specs=[
            pl.BlockSpec((4, QK_DIM), lambda b, p, pt: (0, 0)),
            pl.BlockSpec((1, HEAD_DIM), lambda b, p, pt: (0, 0)),
            pl.BlockSpec((1, tn * N_HEADS, HEAD_DIM), lambda b, p, pt: (b, 0, 0)),
            pl.BlockSpec((1, tn * N_HEADS, HEAD_DIM), lambda b, p, pt: (b, 0, 0)),
            pl.BlockSpec((1, tn * N_HEADS, HEAD_DIM), lambda b, p, pt: (b, 0, 0)),
            pl.BlockSpec((1, page, N_HEADS, HEAD_DIM), lambda b, p, pt: (pt[b, p], 0, 0, 0)),
            pl.BlockSpec((1, page, N_HEADS, HEAD_DIM), lambda b, p, pt: (pt[b, p], 0, 0, 0)),
        ],
        out_specs=pl.BlockSpec((1, tn * N_HEADS, HEAD_DIM), lambda b, p, pt: (b, 0, 0)),
        scratch_shapes=[
            pltpu.VMEM((nrow, HEAD_DIM), BF16),
            pltpu.VMEM((nrow, ncol), F32),
            pltpu.VMEM((nrow, 1), F32),
            pltpu.VMEM((nrow, 1), F32),
            pltpu.VMEM((nrow, HEAD_DIM), F32),
        ],
    )
    return pl.pallas_call(
        functools.partial(_attn_sample_kernel, n_pages=n_pages, page=page, tn=tn, lam_init=lam_init),
        grid_spec=grid_spec,
        out_shape=jax.ShapeDtypeStruct((n_seq, tn * N_HEADS, HEAD_DIM), BF16),
        compiler_params=pltpu.CompilerParams(
            dimension_semantics=("parallel", "arbitrary"), vmem_limit_bytes=V7X_VMEM_LIMIT_BYTES),
        name="attn_sample",
    )(page_table, lam_rows, subln_w, q, k_new, v_new, cache_k, cache_v)


def _gdn_intra(aq, ak, av, gates, alog_row, dtb_row, h, *, rows, chunk):
    qn = aq * lax.rsqrt(jnp.sum(aq * aq, axis=-1, keepdims=True) + 1e-6) * (HEAD_DIM ** -0.5)
    kn = ak * lax.rsqrt(jnp.sum(ak * ak, axis=-1, keepdims=True) + 1e-6)
    beta_all = jax.nn.sigmoid(gates)
    g_all = -jnp.exp(alog_row) * _softplus(gates + dtb_row)
    sel_r = lax.broadcasted_iota(jnp.int32, (HEAD_DIM, HEAD_DIM), 0)
    beta = _dot_sel_r(beta_all, (sel_r == h).astype(BF16))
    g = _dot_sel_r(g_all, (sel_r == N_HEADS + h).astype(BF16))

    ri = lax.broadcasted_iota(jnp.int32, (rows, rows), 0)
    ci = lax.broadcasted_iota(jnp.int32, (rows, rows), 1)
    same = _idiv(ri, chunk) == _idiv(ci, chunk)
    causal = same & (ri >= ci)
    strict = same & (ri > ci)
    g_cum = _dot_sel_l(causal.astype(BF16), g)
    g_tot = _dot_sel_l(same.astype(BF16), g)
    if rows >= HEAD_DIM:
        g_sq = jnp.concatenate([g] * (rows // HEAD_DIM), axis=1)
    else:
        g_sq = g[:, :rows]
    g_diff = _dot_sel_l(causal.astype(BF16), jnp.where(strict, g_sq, 0.0))
    decay = jnp.where(causal, jnp.exp(g_diff), 0.0)

    kbeta = kn * beta
    knb = kn.astype(BF16)
    lmat = jnp.where(strict, _dot_nt(kbeta.astype(BF16), knb) * decay, 0.0)
    tinv = (ri == ci).astype(F32)
    s = 1
    while s < chunk:
        lower_left = ((_idiv(ri, 2 * s) == _idiv(ci, 2 * s)) & (_imod(_idiv(ri, s), 2) == 1)
                      & (_imod(_idiv(ci, s), 2) == 0))
        tinv = tinv - _dot_hp(_dot_hp(tinv, jnp.where(lower_left, lmat, 0.0)), tinv)
        s *= 2
    tb = tinv.astype(BF16)
    e_cum = jnp.exp(g_cum)
    u = _dot(tb, (av * beta).astype(BF16))
    w = _dot(tb, (kbeta * e_cum).astype(BF16))
    qk = (_dot_nt(qn.astype(BF16), knb) * decay).astype(BF16)
    fr = lax.broadcasted_iota(jnp.int32, (rows, chunk), 0)
    fc = lax.broadcasted_iota(jnp.int32, (rows, chunk), 1)
    qkc = _dot(qk, (_imod(fr, chunk) == fc).astype(BF16))
    qg = qn * e_cum
    kd = (kn * jnp.exp(g_tot - g_cum)).astype(BF16)
    eye = (sel_r == lax.broadcasted_iota(jnp.int32, (HEAD_DIM, HEAD_DIM), 1)).astype(BF16)
    kdt = _dot_nt(eye, kd)
    return u, w.astype(BF16), qg.astype(BF16), kdt.astype(BF16), qkc.astype(BF16), jnp.exp(g_tot)


def _store_intra(outs, u_ref, w_ref, qg_ref, kdt_ref, qkc_ref, egl_ref):
    u, w, qg, kdt, qkc, egl = outs
    u_ref[...] = u
    w_ref[...] = w
    qg_ref[...] = qg
    kdt_ref[0] = kdt
    qkc_ref[0] = qkc
    egl_ref[...] = egl


def _gdn_intra_prompt_kernel(xq_ref, xk_ref, xv_ref, gates_ref, cq_ref, ck_ref, cv_ref, alog_ref, dtb_ref,
                             u_ref, w_ref, qg_ref, kdt_ref, qkc_ref, egl_ref,
                             pq_ref, pk_ref, pv_ref, *, rows, chunk):
    h = pl.program_id(1)
    t = pl.program_id(2)

    def conv_act(x_ref, cw_ref, pad_ref):
        @pl.when(t == 0)
        def _():
            pad_ref[0:8, :] = jnp.zeros((8, HEAD_DIM), F32)

        pad_ref[8:, :] = x_ref[...]
        cw = cw_ref[...]
        acc = pad_ref[pl.ds(8 - (CONV_W - 1), rows), :] * cw[0:1]
        for j in range(1, CONV_W):
            acc = acc + pad_ref[pl.ds(8 - (CONV_W - 1) + j, rows), :] * cw[j:j + 1]
        pad_ref[0:8, :] = x_ref[rows - 8:, :]
        return _silu(acc)

    aq = conv_act(xq_ref, cq_ref, pq_ref)
    ak = conv_act(xk_ref, ck_ref, pk_ref)
    av = conv_act(xv_ref, cv_ref, pv_ref)
    outs = _gdn_intra(aq, ak, av, gates_ref[...], alog_ref[...], dtb_ref[...], h, rows=rows, chunk=chunk)
    _store_intra(outs, u_ref, w_ref, qg_ref, kdt_ref, qkc_ref, egl_ref)


def _gdn_intra_sample_kernel(xq_ref, xk_ref, xv_ref, gates_ref, cq_ref, ck_ref, cv_ref, alog_ref, dtb_ref,
                             bq_ref, bk_ref, bv_ref,
                             u_ref, w_ref, qg_ref, kdt_ref, qkc_ref, egl_ref,
                             pad_ref, *, n_seq, tn):
    h = pl.program_id(0)
    seg = 8 + tn

    def conv_act(x_ref, cw_ref, buf_ref):
        cw = cw_ref[...]
        pieces = []
        for s in range(n_seq):
            pad_ref[s * seg + 8 - (CONV_W - 1):s * seg + 8, :] = buf_ref[s]
            pad_ref[s * seg + 8:(s + 1) * seg, :] = x_ref[s * tn:(s + 1) * tn, :]
        for s in range(n_seq):
            base = s * seg + 8 - (CONV_W - 1)
            acc = pad_ref[pl.ds(base, tn), :] * cw[0:1]
            for j in range(1, CONV_W):
                acc = acc + pad_ref[pl.ds(base + j, tn), :] * cw[j:j + 1]
            pieces.append(acc)
        return _silu(jnp.concatenate(pieces, axis=0))

    aq = conv_act(xq_ref, cq_ref, bq_ref)
    ak = conv_act(xk_ref, ck_ref, bk_ref)
    av = conv_act(xv_ref, cv_ref, bv_ref)
    outs = _gdn_intra(aq, ak, av, gates_ref[...], alog_ref[...], dtb_ref[...], h,
                      rows=n_seq * tn, chunk=tn)
    _store_intra(outs, u_ref, w_ref, qg_ref, kdt_ref, qkc_ref, egl_ref)


def _intra_out(m, chunk):
    shapes = [
        jax.ShapeDtypeStruct((m, N_HEADS * HEAD_DIM), F32),
        jax.ShapeDtypeStruct((m, N_HEADS * HEAD_DIM), BF16),
        jax.ShapeDtypeStruct((m, N_HEADS * HEAD_DIM), BF16),
        jax.ShapeDtypeStruct((N_HEADS, HEAD_DIM, m), BF16),
        jax.ShapeDtypeStruct((N_HEADS, m, chunk), BF16),
        jax.ShapeDtypeStruct((m, N_HEADS * HEAD_DIM), F32),
    ]
    return shapes


def _gdn_intra_prompt(proj, gates, conv_w, alog_row, dtb_row, *, batch, t):
    m = proj.shape[0]
    rows = GDN_ROWS
    nblk = t // rows
    c0 = 3 * N_HEADS

    def row_blk(b, h, i):
        return b * nblk + i

    in_specs = [
        pl.BlockSpec((rows, HEAD_DIM), lambda b, h, i: (row_blk(b, h, i), c0 + h)),
        pl.BlockSpec((rows, HEAD_DIM), lambda b, h, i: (row_blk(b, h, i), c0 + N_HEADS + h)),
        pl.BlockSpec((rows, HEAD_DIM), lambda b, h, i: (row_blk(b, h, i), c0 + 2 * N_HEADS + h)),
        pl.BlockSpec((rows, HEAD_DIM), lambda b, h, i: (row_blk(b, h, i), 0)),
        pl.BlockSpec((CONV_W, HEAD_DIM), lambda b, h, i: (0, h)),
        pl.BlockSpec((CONV_W, HEAD_DIM), lambda b, h, i: (0, N_HEADS + h)),
        pl.BlockSpec((CONV_W, HEAD_DIM), lambda b, h, i: (0, 2 * N_HEADS + h)),
        pl.BlockSpec((1, HEAD_DIM), lambda b, h, i: (0, 0)),
        pl.BlockSpec((1, HEAD_DIM), lambda b, h, i: (0, 0)),
    ]
    out_specs = [
        pl.BlockSpec((rows, HEAD_DIM), lambda b, h, i: (row_blk(b, h, i), h)),
        pl.BlockSpec((rows, HEAD_DIM), lambda b, h, i: (row_blk(b, h, i), h)),
        pl.BlockSpec((rows, HEAD_DIM), lambda b, h, i: (row_blk(b, h, i), h)),
        pl.BlockSpec((1, HEAD_DIM, rows), lambda b, h, i: (h, 0, row_blk(b, h, i))),
        pl.BlockSpec((1, rows, CHUNK), lambda b, h, i: (h, row_blk(b, h, i), 0)),
        pl.BlockSpec((rows, HEAD_DIM), lambda b, h, i: (row_blk(b, h, i), h)),
    ]
    return pl.pallas_call(
        functools.partial(_gdn_intra_prompt_kernel, rows=rows, chunk=CHUNK),
        grid=(batch, N_HEADS, nblk),
        in_specs=in_specs,
        out_specs=out_specs,
        out_shape=_intra_out(m, CHUNK),
        scratch_shapes=[pltpu.VMEM((rows + 8, HEAD_DIM), F32)] * 3,
        compiler_params=pltpu.CompilerParams(
            dimension_semantics=("parallel", "parallel", "arbitrary"),
            vmem_limit_bytes=V7X_VMEM_LIMIT_BYTES),
        name="gdn_intra_prompt",
    )(proj, proj, proj, gates, conv_w, conv_w, conv_w, alog_row, dtb_row)


def _gdn_intra_sample(proj, gates, conv_w, alog_row, dtb_row, conv_state, *, n_seq, tn):
    m = proj.shape[0]
    c0 = 3 * N_HEADS
    in_specs = [
        pl.BlockSpec((m, HEAD_DIM), lambda h: (0, c0 + h)),
        pl.BlockSpec((m, HEAD_DIM), lambda h: (0, c0 + N_HEADS + h)),
        pl.BlockSpec((m, HEAD_DIM), lambda h: (0, c0 + 2 * N_HEADS + h)),
        pl.BlockSpec((m, HEAD_DIM), lambda h: (0, 0)),
        pl.BlockSpec((CONV_W, HEAD_DIM), lambda h: (0, h)),
        pl.BlockSpec((CONV_W, HEAD_DIM), lambda h: (0, N_HEADS + h)),
        pl.BlockSpec((CONV_W, HEAD_DIM), lambda h: (0, 2 * N_HEADS + h)),
        pl.BlockSpec((1, HEAD_DIM), lambda h: (0, 0)),
        pl.BlockSpec((1, HEAD_DIM), lambda h: (0, 0)),
        pl.BlockSpec((n_seq, CONV_W - 1, HEAD_DIM), lambda h: (0, 0, h)),
        pl.BlockSpec((n_seq, CONV_W - 1, HEAD_DIM), lambda h: (0, 0, N_HEADS + h)),
        pl.BlockSpec((n_seq, CONV_W - 1, HEAD_DIM), lambda h: (0, 0, 2 * N_HEADS + h)),
    ]
    out_specs = [
        pl.BlockSpec((m, HEAD_DIM), lambda h: (0, h)),
        pl.BlockSpec((m, HEAD_DIM), lambda h: (0, h)),
        pl.BlockSpec((m, HEAD_DIM), lambda h: (0, h)),
        pl.BlockSpec((1, HEAD_DIM, m), lambda h: (h, 0, 0)),
        pl.BlockSpec((1, m, tn), lambda h: (h, 0, 0)),
        pl.BlockSpec((m, HEAD_DIM), lambda h: (0, h)),
    ]
    return pl.pallas_call(
        functools.partial(_gdn_intra_sample_kernel, n_seq=n_seq, tn=tn),
        grid=(N_HEADS,),
        in_specs=in_specs,
        out_specs=out_specs,
        out_shape=_intra_out(m, tn),
        scratch_shapes=[pltpu.VMEM((n_seq * (8 + tn), HEAD_DIM), F32)],
        compiler_params=pltpu.CompilerParams(
            dimension_semantics=("parallel",), vmem_limit_bytes=V7X_VMEM_LIMIT_BYTES),
        name="gdn_intra_sample",
    )(proj, proj, proj, gates, conv_w, conv_w, conv_w, alog_row, dtb_row,
      conv_state, conv_state, conv_state)


def _gdn_scan_kernel(u_ref, w_ref, qg_ref, kdt_ref, qkc_ref, egl_ref, z_ref, s0_ref, dnw_ref,
                     o_ref, s_ref, *, chunk, n_chunks, seqs_per_step):
    t = pl.program_id(1)

    @pl.when(t == 0)
    def _():
        s_ref[...] = s0_ref[...]

    dnw = dnw_ref[...]
    for c in range(n_chunks):
        si = c if seqs_per_step > 1 else 0
        rows = slice(c * chunk, (c + 1) * chunk)
        for h in range(N_HEADS):
            cols = slice(h * HEAD_DIM, (h + 1) * HEAD_DIM)
            s_old = s_ref[si, h]
            sb = s_old.astype(BF16)
            both = _dot(jnp.concatenate([w_ref[rows, cols], qg_ref[rows, cols]], axis=0), sb)
            vn = (u_ref[rows, cols] - both[:chunk]).astype(BF16)
            o = both[chunk:] + _dot(qkc_ref[h, rows, :], vn)
            s_ref[si, h] = s_old * egl_ref[c * chunk:c * chunk + 1, cols] + _dot(kdt_ref[h, :, rows], vn)
            o_ref[rows, cols] = (_rms(o, dnw) * _silu(z_ref[rows, cols])).astype(o_ref.dtype)


def _gdn_scan(intra, z_src, z_col_block, s0, dnw, *, n_seq, t, chunk, chunks_per_step, seqs_per_step=1):
    u, w, qg, kdt, qkc, egl = intra
    m = u.shape[0]
    width = N_HEADS * HEAD_DIM
    rows = chunk * chunks_per_step
    if seqs_per_step > 1:
        assert seqs_per_step == n_seq == chunks_per_step and t == chunk
        nsteps, n_outer = 1, 1
    else:
        nsteps, n_outer = t // rows, n_seq

    def rb(b, i):
        return b * nsteps + i

    in_specs = [
        pl.BlockSpec((rows, width), lambda b, i: (rb(b, i), 0)),
        pl.BlockSpec((rows, width), lambda b, i: (rb(b, i), 0)),
        pl.BlockSpec((rows, width), lambda b, i: (rb(b, i), 0)),
        pl.BlockSpec((N_HEADS, HEAD_DIM, rows), lambda b, i: (0, 0, rb(b, i))),
        pl.BlockSpec((N_HEADS, rows, chunk), lambda b, i: (0, rb(b, i), 0)),
        pl.BlockSpec((rows, width), lambda b, i: (rb(b, i), 0)),
        pl.BlockSpec((rows, width), lambda b, i: (rb(b, i), z_col_block)),
        pl.BlockSpec((seqs_per_step, N_HEADS, HEAD_DIM, HEAD_DIM), lambda b, i: (b, 0, 0, 0)),
        pl.BlockSpec((1, HEAD_DIM), lambda b, i: (0, 0)),
    ]
    out_specs = [
        pl.BlockSpec((rows, width), lambda b, i: (rb(b, i), 0)),
        pl.BlockSpec((seqs_per_step, N_HEADS, HEAD_DIM, HEAD_DIM), lambda b, i: (b, 0, 0, 0)),
    ]
    return pl.pallas_call(
        functools.partial(_gdn_scan_kernel, chunk=chunk, n_chunks=chunks_per_step,
                          seqs_per_step=seqs_per_step),
        grid=(n_outer, nsteps),
        in_specs=in_specs,
        out_specs=out_specs,
        out_shape=[jax.ShapeDtypeStruct((m, width), BF16),
                   jax.ShapeDtypeStruct((n_seq, N_HEADS, HEAD_DIM, HEAD_DIM), F32)],
        compiler_params=pltpu.CompilerParams(
            dimension_semantics=("parallel", "arbitrary"), vmem_limit_bytes=V7X_VMEM_LIMIT_BYTES),
        name="gdn_scan",
    )(u, w, qg, kdt, qkc, egl, z_src, s0, dnw)


def _pad_lanes(v, offset):
    out = jnp.zeros((1, HEAD_DIM), F32)
    return lax.dynamic_update_slice(out, v.reshape(1, -1).astype(F32), (0, offset))


def kernel(x_prompt, x_sample, cache_k, cache_v, state_ssm, state_conv, page_table, ffn1_pre_w, ffn1_post_w, ffn1_gate, ffn1_up, ffn1_down, mix_pre_w, mix_post_w, w_in, conv_w, a_log, dt_bias, delta_norm_w, lambda_q1, lambda_k1, lambda_q2, lambda_k2, subln_w, w_out, ffn2_pre_w, ffn2_post_w, ffn2_gate, ffn2_up, ffn2_down):
    depth = w_in.shape[0]
    batch, t, d = x_prompt.shape
    n_seq, tn, _ = x_sample.shape
    a_w = N_HEADS * HEAD_DIM
    n_main = 3 * a_w + 3 * a_w + a_w
    yp = x_prompt.reshape(batch * t, d)
    ys = x_sample.reshape(n_seq * tn, d)
    outs = [[] for _ in range(8)]
    for l in range(depth):
        lam_init = 0.8 - 0.6 * math.exp(-0.3 * l)
        lam_rows = jnp.stack([lambda_q1[l], lambda_k1[l], lambda_q2[l], lambda_k2[l]]).astype(F32)
        row = lambda v: v[l].reshape(1, -1).astype(F32)
        wg1, wu1, wd1 = ffn1_gate[l].astype(BF16), ffn1_up[l].astype(BF16), ffn1_down[l].astype(BF16)
        wg2, wu2, wd2 = ffn2_gate[l].astype(BF16), ffn2_up[l].astype(BF16), ffn2_down[l].astype(BF16)
        w_main = w_in[l][:, :n_main].astype(BF16)
        w_gates = jnp.pad(w_in[l][:, n_main:], ((0, 0), (0, HEAD_DIM - 2 * N_HEADS))).astype(BF16)
        wo = w_out[l].astype(BF16)
        alog_row = _pad_lanes(a_log[l], N_HEADS)
        dtb_row = _pad_lanes(dt_bias[l], N_HEADS)
        dnw = row(delta_norm_w)
        subw = row(subln_w)
        cw = conv_w[l].astype(F32)

        yp = _ffn(yp, row(ffn1_pre_w), row(ffn1_post_w), wg1, wu1, wd1, tm=512, tf=512)
        ys = _ffn(ys, row(ffn1_pre_w), row(ffn1_post_w), wg1, wu1, wd1, tm=n_seq * tn, tf=512)

        pp, gp = _inproj(yp, row(mix_pre_w), w_main, w_gates, tm=512, tn=1024)
        oa_p = _attn_prompt(lam_rows, subw, pp, batch=batch, t=t, lam_init=lam_init, tq=256)
        intra_p = _gdn_intra_prompt(pp, gp, cw, alog_row, dtb_row, batch=batch, t=t)
        zero_s = jnp.zeros((batch, N_HEADS, HEAD_DIM, HEAD_DIM), F32)
        ob_p, s_p = _gdn_scan(intra_p, pp, 6, zero_s, dnw, n_seq=batch, t=t, chunk=CHUNK,
                              chunks_per_step=GDN_ROWS // CHUNK)
        yp = _outproj(yp, oa_p, ob_p, wo, row(mix_post_w), tm=512)

        ps, gs = _inproj(ys, row(mix_pre_w), w_main, w_gates, tm=n_seq * tn, tn=1024)
        qs = ps[:, :a_w].reshape(n_seq, tn * N_HEADS, HEAD_DIM)
        ks = ps[:, a_w:2 * a_w].reshape(n_seq, tn * N_HEADS, HEAD_DIM)
        vs = ps[:, 2 * a_w:3 * a_w].reshape(n_seq, tn * N_HEADS, HEAD_DIM)
        oa_s = _attn_sample(page_table, lam_rows, subw, qs, ks, vs, cache_k[l], cache_v[l],
                            lam_init=lam_init)
        oa_s = oa_s.reshape(n_seq * tn, a_w)
        intra_s = _gdn_intra_sample(ps, gs, cw, alog_row, dtb_row, state_conv[l], n_seq=n_seq, tn=tn)
        ob_s, s_s = _gdn_scan(intra_s, ps, 6, state_ssm[l].astype(F32), dnw, n_seq=n_seq, t=tn,
                              chunk=tn, chunks_per_step=n_seq, seqs_per_step=n_seq)
        ys = _outproj(ys, oa_s, ob_s, wo, row(mix_post_w), tm=n_seq * tn)

        yp = _ffn(yp, row(ffn2_pre_w), row(ffn2_post_w), wg2, wu2, wd2, tm=512, tf=512)
        ys = _ffn(ys, row(ffn2_pre_w), row(ffn2_post_w), wg2, wu2, wd2, tm=n_seq * tn, tf=512)

        ppb = pp.reshape(batch, t, -1)
        psb = ps.reshape(n_seq, tn, -1)
        outs[0].append(ppb[:, :, a_w:2 * a_w].reshape(batch, t, N_HEADS, HEAD_DIM))
        outs[1].append(ppb[:, :, 2 * a_w:3 * a_w].reshape(batch, t, N_HEADS, HEAD_DIM))
        outs[2].append(s_p.astype(state_ssm.dtype))
        outs[3].append(ppb[:, t - (CONV_W - 1):, 3 * a_w:6 * a_w])
        outs[4].append(psb[:, :, a_w:2 * a_w].reshape(n_seq, tn, N_HEADS, HEAD_DIM))
        outs[5].append(psb[:, :, 2 * a_w:3 * a_w].reshape(n_seq, tn, N_HEADS, HEAD_DIM))
        outs[6].append(s_s.astype(state_ssm.dtype))
        outs[7].append(jnp.concatenate([state_conv[l].astype(psb.dtype), psb[:, :, 3 * a_w:6 * a_w]],
                                       axis=1)[:, tn:])
    return (yp.reshape(batch, t, d), ys.reshape(n_seq, tn, d)) + tuple(jnp.stack(o) for o in outs)
```

```python
import functools
import math

import jax
import jax.numpy as jnp
from jax import lax
from jax.experimental import pallas as pl
from jax.experimental.pallas import tpu as pltpu

F32 = jnp.float32
BF16 = jnp.bfloat16
EPS = 1e-6
NEG_INF = float("-inf")
LOG2E = 1.4426950408889634

V7X_VMEM_LIMIT_BYTES = 56 * 1024 * 1024

QK_DIM = 64
HEAD_DIM = 128
N_HEADS = 8
CONV_W = 4
CHUNK = 64
GDN_ROWS = 256


def _dot(a, b):
    return jnp.dot(a, b, preferred_element_type=F32)


def _dot_nt(a, b):
    return lax.dot_general(a, b, (((1,), (1,)), ((), ())), preferred_element_type=F32)


def _split3(x):
    hi = x.astype(BF16)
    r = x - hi.astype(F32)
    mid = r.astype(BF16)
    lo = (r - mid.astype(F32)).astype(BF16)
    return hi, mid, lo


def _dot_sel_l(sel, x):
    hi, mid, lo = _split3(x)
    return _dot(sel, hi) + _dot(sel, mid) + _dot(sel, lo)


def _dot_sel_r(x, sel):
    hi, mid, lo = _split3(x)
    return _dot(hi, sel) + _dot(mid, sel) + _dot(lo, sel)


def _dot_hp(a, b):
    ah = a.astype(BF16)
    al = (a - ah.astype(F32)).astype(BF16)
    bh = b.astype(BF16)
    bl = (b - bh.astype(F32)).astype(BF16)
    return _dot(ah, bh) + _dot(ah, bl) + _dot(al, bh)


def _rms(x, w):
    return x * lax.rsqrt(jnp.mean(x * x, axis=-1, keepdims=True) + EPS) * w


def _silu(x):
    return x * jax.nn.sigmoid(x)


def _softplus(x):
    return jnp.maximum(x, 0.0) + jnp.log1p(jnp.exp(-jnp.abs(x)))


def _lambda_full(lam_ref, lam_init):
    l = lam_ref[...]
    a = jnp.sum(l[0:1] * l[1:2], axis=-1, keepdims=True)
    b = jnp.sum(l[2:3] * l[3:4], axis=-1, keepdims=True)
    return jnp.exp(a) - jnp.exp(b) + lam_init


def _idiv(x, n):
    assert n & (n - 1) == 0
    return x >> (n.bit_length() - 1)


def _imod(x, n):
    assert n & (n - 1) == 0
    return x & (n - 1)


def _pow2_neg(k):
    return lax.bitcast_convert_type((127 - k) << 23, F32)


def _ffn_kernel(x_ref, prew_ref, postw_ref, wg_ref, wu_ref, wd_ref, o_ref, h_ref, *, nf):
    f = pl.program_id(1)

    @pl.when(f == 0)
    def _():
        h_ref[...] = _rms(x_ref[...], prew_ref[...]).astype(BF16)
        o_ref[...] = jnp.zeros_like(o_ref)

    h = h_ref[...]
    g = _dot(h, wg_ref[...])
    u = _dot(h, wu_ref[...])
    o_ref[...] += _dot((_silu(g) * u).astype(BF16), wd_ref[...])

    @pl.when(f == nf - 1)
    def _():
        o_ref[...] = x_ref[...] + 0.5 * _rms(o_ref[...], postw_ref[...])


def _ffn(x, pre_w, post_w, wg, wu, wd, *, tm, tf):
    m, d = x.shape
    ff = wg.shape[1]
    nf = ff // tf
    return pl.pallas_call(
        functools.partial(_ffn_kernel, nf=nf),
        grid=(m // tm, nf),
        in_specs=[
            pl.BlockSpec((tm, d), lambda i, f: (i, 0)),
            pl.BlockSpec((1, d), lambda i, f: (0, 0)),
            pl.BlockSpec((1, d), lambda i, f: (0, 0)),
            pl.BlockSpec((d, tf), lambda i, f: (0, f)),
            pl.BlockSpec((d, tf), lambda i, f: (0, f)),
            pl.BlockSpec((tf, d), lambda i, f: (f, 0)),
        ],
        out_specs=pl.BlockSpec((tm, d), lambda i, f: (i, 0)),
        out_shape=jax.ShapeDtypeStruct((m, d), F32),
        scratch_shapes=[pltpu.VMEM((tm, d), BF16)],
        compiler_params=pltpu.CompilerParams(
            dimension_semantics=("parallel", "arbitrary"), vmem_limit_bytes=V7X_VMEM_LIMIT_BYTES),
        name="ffn",
    )(x, pre_w, post_w, wg, wu, wd)


def _inproj_kernel(x_ref, prew_ref, w_ref, wg_ref, o_ref, og_ref, h_ref):
    j = pl.program_id(1)

    @pl.when(j == 0)
    def _():
        h = _rms(x_ref[...], prew_ref[...]).astype(BF16)
        h_ref[...] = h
        og_ref[...] = _dot(h, wg_ref[...])

    o_ref[...] = _dot(h_ref[...], w_ref[...])


def _inproj(x, pre_w, w_main, w_gates, *, tm, tn):
    m, d = x.shape
    n = w_main.shape[1]
    ng = w_gates.shape[1]
    return pl.pallas_call(
        _inproj_kernel,
        grid=(m // tm, n // tn),
        in_specs=[
            pl.BlockSpec((tm, d), lambda i, j: (i, 0)),
            pl.BlockSpec((1, d), lambda i, j: (0, 0)),
            pl.BlockSpec((d, tn), lambda i, j: (0, j)),
            pl.BlockSpec((d, ng), lambda i, j: (0, 0)),
        ],
        out_specs=[
            pl.BlockSpec((tm, tn), lambda i, j: (i, j)),
            pl.BlockSpec((tm, ng), lambda i, j: (i, 0)),
        ],
        out_shape=[jax.ShapeDtypeStruct((m, n), F32), jax.ShapeDtypeStruct((m, ng), F32)],
        scratch_shapes=[pltpu.VMEM((tm, d), BF16)],
        compiler_params=pltpu.CompilerParams(
            dimension_semantics=("parallel", "arbitrary"), vmem_limit_bytes=V7X_VMEM_LIMIT_BYTES),
        name="inproj",
    )(x, pre_w, w_main, w_gates)


def _outproj_kernel(x_ref, oa_ref, ob_ref, wa_ref, wb_ref, postw_ref, o_ref):
    y = _dot(oa_ref[...], wa_ref[...]) + _dot(ob_ref[...], wb_ref[...])
    o_ref[...] = x_ref[...] + _rms(y, postw_ref[...])


def _outproj(x, oa, ob, w_out, post_w, *, tm):
    m, d = x.shape
    ka = oa.shape[1]
    return pl.pallas_call(
        _outproj_kernel,
        grid=(m // tm,),
        in_specs=[
            pl.BlockSpec((tm, d), lambda i: (i, 0)),
            pl.BlockSpec((tm, ka), lambda i: (i, 0)),
            pl.BlockSpec((tm, ka), lambda i: (i, 0)),
            pl.BlockSpec((ka, d), lambda i: (0, 0)),
            pl.BlockSpec((ka, d), lambda i: (1, 0)),
            pl.BlockSpec((1, d), lambda i: (0, 0)),
        ],
        out_specs=pl.BlockSpec((tm, d), lambda i: (i, 0)),
        out_shape=jax.ShapeDtypeStruct((m, d), F32),
        compiler_params=pltpu.CompilerParams(
            dimension_semantics=("parallel",), vmem_limit_bytes=V7X_VMEM_LIMIT_BYTES),
        name="outproj",
    )(x, oa, ob, w_out, w_out, post_w)


def _attn_prompt_kernel(lam_ref, subw_ref, q_ref, k_ref, v_ref, o_ref,
                        kb_ref, vb_ref, bias_ref, *, t, tq, hpp, lam_init):
    hg = pl.program_id(1)
    nq = t // tq
    lam = _lambda_full(lam_ref, lam_init)
    lane = lax.broadcasted_iota(jnp.int32, (1, HEAD_DIM), 1)
    first_map = lane < QK_DIM
    r = lax.broadcasted_iota(jnp.int32, (tq, tq), 0)
    c = lax.broadcasted_iota(jnp.int32, (tq, tq), 1)
    on_or_below = jnp.concatenate([r >= c, r >= c], axis=0)
    slopes = [_pow2_neg(jnp.full((1, 1), hg * hpp + hh + 1, jnp.int32)) * LOG2E for hh in range(hpp)]
    for hh in range(hpp):
        bias_ref[hh] = slopes[hh] * (c - r).astype(F32)
    kb_ref[...] = k_ref[...].astype(BF16)
    vb_ref[...] = v_ref[...].astype(BF16)
    scale = QK_DIM ** -0.5 * LOG2E

    def lane_fold(x):
        out = x[:, :HEAD_DIM]
        for i in range(1, x.shape[1] // HEAD_DIM):
            out = out + x[:, i * HEAD_DIM:(i + 1) * HEAD_DIM]
        return out

    def q_block(i, _):
        rows_i = pl.ds(pl.multiple_of(i * tq, tq), tq)
        qs = []
        for hh in range(hpp):
            q = q_ref[rows_i, hh * HEAD_DIM:(hh + 1) * HEAD_DIM] * scale
            qs.append(jnp.concatenate(
                [jnp.where(first_map, q, 0.0), jnp.where(first_map, 0.0, q)], axis=0).astype(BF16))
        def kv_block(j, state, masked):
            rows_j = pl.ds(pl.multiple_of(j * tq, tq), tq)
            blocks_away = ((i - j) * tq).astype(F32)
            out = []
            for hh in range(hpp):
                cols = slice(hh * HEAD_DIM, (hh + 1) * HEAD_DIM)
                m_old, l_old, acc_old = state[hh]
                b = bias_ref[hh]
                s = _dot_nt(qs[hh], kb_ref[rows_j, cols]) + jnp.concatenate([b, b], axis=0)
                if masked:
                    s = jnp.where(on_or_below, s, NEG_INF)
                far = slopes[hh] * blocks_away
                m_new = jnp.maximum(m_old, jnp.max(s, axis=-1, keepdims=True) - far)
                p = jnp.exp2(s - (far + m_new))
                alpha = jnp.exp2(m_old - m_new)
                out.append((m_new, alpha * l_old + lane_fold(p),
                            alpha * acc_old + _dot(p.astype(BF16), vb_ref[rows_j, cols])))
            return tuple(out)

        init = tuple((jnp.full((2 * tq, 1), NEG_INF, F32), jnp.zeros((2 * tq, HEAD_DIM), F32),
                      jnp.zeros((2 * tq, HEAD_DIM), F32)) for _ in range(hpp))
        state = lax.fori_loop(0, i, lambda j, st: kv_block(j, st, False), init)
        state = kv_block(i, state, True)
        for hh in range(hpp):
            _, l_fin, acc_fin = state[hh]
            o = acc_fin / jnp.sum(l_fin, axis=-1, keepdims=True)
            d = o[:tq] - lam * o[tq:]
            o_ref[rows_i, hh * HEAD_DIM:(hh + 1) * HEAD_DIM] = (
                _rms(d, subw_ref[...]) * (1.0 - lam_init)).astype(o_ref.dtype)
        return 0

    lax.fori_loop(0, nq, q_block, 0)


def _attn_prompt(lam_rows, subln_w, proj, *, batch, t, lam_init, tq, hpp):
    m = proj.shape[0]
    ng = N_HEADS // hpp
    w = hpp * HEAD_DIM
    return pl.pallas_call(
        functools.partial(_attn_prompt_kernel, t=t, tq=tq, hpp=hpp, lam_init=lam_init),
        grid=(batch, ng),
        in_specs=[
            pl.BlockSpec((4, QK_DIM), lambda b, g: (0, 0)),
            pl.BlockSpec((1, HEAD_DIM), lambda b, g: (0, 0)),
            pl.BlockSpec((t, w), lambda b, g: (b, g)),
            pl.BlockSpec((t, w), lambda b, g: (b, ng + g)),
            pl.BlockSpec((t, w), lambda b, g: (b, 2 * ng + g)),
        ],
        out_specs=pl.BlockSpec((t, w), lambda b, g: (b, g)),
        out_shape=jax.ShapeDtypeStruct((m, N_HEADS * HEAD_DIM), BF16),
        scratch_shapes=[
            pltpu.VMEM((t, w), BF16),
            pltpu.VMEM((t, w), BF16),
            pltpu.VMEM((hpp, tq, tq), F32),
        ],
        compiler_params=pltpu.CompilerParams(
            dimension_semantics=("parallel", "parallel"), vmem_limit_bytes=V7X_VMEM_LIMIT_BYTES),
        name="attn_prompt",
    )(lam_rows, subln_w, proj, proj, proj)


def _attn_sample_kernel(pt_ref, lam_ref, subw_ref, q_ref, kn_ref, vn_ref, *rest,
                        n_pages, page, tn, group, lam_init):
    del pt_ref
    kc_refs = rest[:group]
    vc_refs = rest[group:2 * group]
    o_ref, qb_ref, bias_ref, m_ref, l_ref, acc_ref = rest[2 * group:]
    p = pl.program_id(1)
    nrow = 2 * tn * N_HEADS
    ncol = page * N_HEADS
    past = n_pages * page
    row = lax.broadcasted_iota(jnp.int32, (nrow, 1), 0)
    slope = _pow2_neg(_imod(row, N_HEADS) + 1)

    @pl.when(p == 0)
    def _():
        q = q_ref[0] * (QK_DIM ** -0.5)
        lane = lax.broadcasted_iota(jnp.int32, (1, HEAD_DIM), 1)
        first_map = lane < QK_DIM
        qb_ref[...] = jnp.concatenate(
            [jnp.where(first_map, q, 0.0), jnp.where(first_map, 0.0, q)], axis=0).astype(BF16)
        rr = lax.broadcasted_iota(jnp.int32, (nrow, ncol), 0)
        cc = lax.broadcasted_iota(jnp.int32, (nrow, ncol), 1)
        tq_ = _idiv(_imod(rr, tn * N_HEADS), N_HEADS)
        same_head = _imod(rr, N_HEADS) == _imod(cc, N_HEADS)
        bias_ref[...] = jnp.where(same_head, slope * (_idiv(cc, N_HEADS) - tq_).astype(F32), NEG_INF)
        m_ref[...] = jnp.full(m_ref.shape, NEG_INF, F32)
        l_ref[...] = jnp.zeros(l_ref.shape, F32)
        acc_ref[...] = jnp.zeros(acc_ref.shape, F32)

    def lane_fold(x, op):
        out = x[:, :HEAD_DIM]
        for i in range(1, x.shape[1] // HEAD_DIM):
            out = op(out, x[:, i * HEAD_DIM:(i + 1) * HEAD_DIM])
        return out

    def update(scores, offs, vals):
        m_old = m_ref[...]
        m_loc = lane_fold(scores[0], jnp.maximum) - offs[0]
        for s, off in zip(scores[1:], offs[1:]):
            m_loc = jnp.maximum(m_loc, lane_fold(s, jnp.maximum) - off)
        m_new = jnp.maximum(m_old, jnp.max(m_loc, axis=-1, keepdims=True))
        alpha = jnp.exp(m_old - m_new)
        pes = [jnp.exp(s - (off + m_new)) for s, off in zip(scores, offs)]
        l_part = lane_fold(pes[0], jnp.add)
        for pe in pes[1:]:
            l_part = l_part + lane_fold(pe, jnp.add)
        pv = _dot(jnp.concatenate([pe.astype(BF16) for pe in pes], axis=1),
                  jnp.concatenate(vals, axis=0))
        l_ref[...] = alpha * l_ref[...] + l_part
        acc_ref[...] = alpha * acc_ref[...] + pv
        m_ref[...] = m_new

    qb = qb_ref[...]
    bias = bias_ref[...]
    scores, offs, vals = [], [], []
    for g in range(group):
        kf = kc_refs[g][0].reshape(ncol, HEAD_DIM).astype(BF16)
        vals.append(vc_refs[g][0].reshape(ncol, HEAD_DIM).astype(BF16))
        scores.append(_dot_nt(qb, kf) + bias)
        offs.append(slope * (past - (p * group + g) * page).astype(F32))
    update(scores, offs, vals)

    @pl.when(p == n_pages // group - 1)
    def _():
        ncn = tn * N_HEADS
        rr = lax.broadcasted_iota(jnp.int32, (nrow, HEAD_DIM), 0)
        cc = lax.broadcasted_iota(jnp.int32, (nrow, HEAD_DIM), 1)
        tq_ = _idiv(_imod(rr, ncn), N_HEADS)
        tk_ = _idiv(cc, N_HEADS)
        ok = (cc < ncn) & (_imod(rr, N_HEADS) == _imod(cc, N_HEADS)) & (tk_ <= tq_)
        bias_new = jnp.where(ok, slope * (tk_ - tq_).astype(F32), NEG_INF)
        pad = jnp.zeros((HEAD_DIM - ncn, HEAD_DIM), BF16)
        kn = jnp.concatenate([kn_ref[0].astype(BF16), pad], axis=0)
        vn = jnp.concatenate([vn_ref[0].astype(BF16), pad], axis=0)
        update([_dot_nt(qb, kn) + bias_new], [jnp.zeros((nrow, 1), F32)], [vn])
        lam = _lambda_full(lam_ref, lam_init)
        o = acc_ref[...] / jnp.sum(l_ref[...], axis=-1, keepdims=True)
        d = o[:ncn] - lam * o[ncn:]
        o_ref[0] = (_rms(d, subw_ref[...]) * (1.0 - lam_init)).astype(o_ref.dtype)


def _attn_sample(page_table, lam_rows, subln_w, q, k_new, v_new, cache_k, cache_v, *, lam_init, group):
    n_seq, n_pages = page_table.shape
    page = cache_k.shape[1]
    tn = q.shape[1] // N_HEADS
    nrow = 2 * tn * N_HEADS
    ncol = page * N_HEADS
    assert n_pages % group == 0

    def page_spec(g):
        return pl.BlockSpec((1, page, N_HEADS, HEAD_DIM),
                            lambda b, p, pt: (pt[b, p * group + g], 0, 0, 0))

    new_spec = pl.BlockSpec((1, tn * N_HEADS, HEAD_DIM), lambda b, p, pt: (b, 0, 0))
    grid_spec = pltpu.PrefetchScalarGridSpec(
        num_scalar_prefetch=1,
        grid=(n_seq, n_pages // group),
        in_specs=[
            pl.BlockSpec((4, QK_DIM), lambda b, p, pt: (0, 0)),
            pl.BlockSpec((1, HEAD_DIM), lambda b, p, pt: (0, 0)),
            new_spec, new_spec, new_spec,
        ] + [page_spec(g) for g in range(group)] * 2,
        out_specs=new_spec,
        scratch_shapes=[
            pltpu.VMEM((nrow, HEAD_DIM), BF16),
            pltpu.VMEM((nrow, ncol), F32),
            pltpu.VMEM((nrow, 1), F32),
            pltpu.VMEM((nrow, HEAD_DIM), F32),
            pltpu.VMEM((nrow, HEAD_DIM), F32),
        ],
    )
    return pl.pallas_call(
        functools.partial(_attn_sample_kernel, n_pages=n_pages, page=page, tn=tn, group=group,
                          lam_init=lam_init),
        grid_spec=grid_spec,
        out_shape=jax.ShapeDtypeStruct((n_seq, tn * N_HEADS, HEAD_DIM), BF16),
        compiler_params=pltpu.CompilerParams(
            dimension_semantics=("parallel", "arbitrary"), vmem_limit_bytes=V7X_VMEM_LIMIT_BYTES),
        name="attn_sample",
    )(page_table, lam_rows, subln_w, q, k_new, v_new, *([cache_k] * group), *([cache_v] * group))


def _gdn_intra(aq, ak, av, gates, alog_row, dtb_row, h, *, rows, chunk):
    qn = aq * lax.rsqrt(jnp.sum(aq * aq, axis=-1, keepdims=True) + 1e-6) * (HEAD_DIM ** -0.5)
    kn = ak * lax.rsqrt(jnp.sum(ak * ak, axis=-1, keepdims=True) + 1e-6)
    beta_all = jax.nn.sigmoid(gates)
    g_all = -jnp.exp(alog_row) * _softplus(gates + dtb_row)
    sel_r = lax.broadcasted_iota(jnp.int32, (HEAD_DIM, HEAD_DIM), 0)
    beta = _dot_sel_r(beta_all, (sel_r == h).astype(BF16))
    g = _dot_sel_r(g_all, (sel_r == N_HEADS + h).astype(BF16))

    ri = lax.broadcasted_iota(jnp.int32, (rows, rows), 0)
    ci = lax.broadcasted_iota(jnp.int32, (rows, rows), 1)
    same = _idiv(ri, chunk) == _idiv(ci, chunk)
    causal = same & (ri >= ci)
    strict = same & (ri > ci)
    g_cum = _dot_sel_l(causal.astype(BF16), g)
    g_tot = _dot_sel_l(same.astype(BF16), g)
    if rows >= HEAD_DIM:
        g_sq = jnp.concatenate([g] * (rows // HEAD_DIM), axis=1)
    else:
        g_sq = g[:, :rows]
    g_diff = _dot_sel_l(causal.astype(BF16), jnp.where(strict, g_sq, 0.0))
    decay = jnp.where(causal, jnp.exp(g_diff), 0.0)

    kbeta = kn * beta
    knb = kn.astype(BF16)
    lmat = jnp.where(strict, _dot_nt(kbeta.astype(BF16), knb) * decay, 0.0)
    tinv = (ri == ci).astype(F32)
    s = 1
    while s < chunk:
        lower_left = ((_idiv(ri, 2 * s) == _idiv(ci, 2 * s)) & (_imod(_idiv(ri, s), 2) == 1)
                      & (_imod(_idiv(ci, s), 2) == 0))
        tinv = tinv - _dot_hp(_dot_hp(tinv, jnp.where(lower_left, lmat, 0.0)), tinv)
        s *= 2
    tb = tinv.astype(BF16)
    e_cum = jnp.exp(g_cum)
    u = _dot(tb, (av * beta).astype(BF16))
    w = _dot(tb, (kbeta * e_cum).astype(BF16))
    qk = (_dot_nt(qn.astype(BF16), knb) * decay).astype(BF16)
    fr = lax.broadcasted_iota(jnp.int32, (rows, chunk), 0)
    fc = lax.broadcasted_iota(jnp.int32, (rows, chunk), 1)
    qkc = _dot(qk, (_imod(fr, chunk) == fc).astype(BF16))
    qg = qn * e_cum
    kd = (kn * jnp.exp(g_tot - g_cum)).astype(BF16)
    eye = (sel_r == lax.broadcasted_iota(jnp.int32, (HEAD_DIM, HEAD_DIM), 1)).astype(BF16)
    kdt = _dot_nt(eye, kd)
    return u, w.astype(BF16), qg.astype(BF16), kdt.astype(BF16), qkc.astype(BF16), jnp.exp(g_tot)


def _store_intra(outs, u_ref, w_ref, qg_ref, kdt_ref, qkc_ref, egl_ref):
    u, w, qg, kdt, qkc, egl = outs
    u_ref[...] = u
    w_ref[...] = w
    qg_ref[...] = qg
    kdt_ref[0] = kdt
    qkc_ref[0] = qkc
    egl_ref[...] = egl


def _gdn_intra_prompt_kernel(xq_ref, xk_ref, xv_ref, gates_ref, cq_ref, ck_ref, cv_ref, alog_ref, dtb_ref,
                             u_ref, w_ref, qg_ref, kdt_ref, qkc_ref, egl_ref,
                             pq_ref, pk_ref, pv_ref, *, rows, chunk):
    h = pl.program_id(1)
    t = pl.program_id(2)

    def conv_act(x_ref, cw_ref, pad_ref):
        @pl.when(t == 0)
        def _():
            pad_ref[0:8, :] = jnp.zeros((8, HEAD_DIM), F32)

        pad_ref[8:, :] = x_ref[...]
        cw = cw_ref[...]
        acc = pad_ref[pl.ds(8 - (CONV_W - 1), rows), :] * cw[0:1]
        for j in range(1, CONV_W):
            acc = acc + pad_ref[pl.ds(8 - (CONV_W - 1) + j, rows), :] * cw[j:j + 1]
        pad_ref[0:8, :] = x_ref[rows - 8:, :]
        return _silu(acc)

    aq = conv_act(xq_ref, cq_ref, pq_ref)
    ak = conv_act(xk_ref, ck_ref, pk_ref)
    av = conv_act(xv_ref, cv_ref, pv_ref)
    outs = _gdn_intra(aq, ak, av, gates_ref[...], alog_ref[...], dtb_ref[...], h, rows=rows, chunk=chunk)
    _store_intra(outs, u_ref, w_ref, qg_ref, kdt_ref, qkc_ref, egl_ref)


def _gdn_intra_sample_kernel(xq_ref, xk_ref, xv_ref, gates_ref, cq_ref, ck_ref, cv_ref, alog_ref, dtb_ref,
                             bq_ref, bk_ref, bv_ref,
                             u_ref, w_ref, qg_ref, kdt_ref, qkc_ref, egl_ref,
                             pad_ref, *, n_seq, tn):
    h = pl.program_id(0)
    seg = 8 + tn

    def conv_act(x_ref, cw_ref, buf_ref):
        cw = cw_ref[...]
        pieces = []
        for s in range(n_seq):
            pad_ref[s * seg + 8 - (CONV_W - 1):s * seg + 8, :] = buf_ref[s]
            pad_ref[s * seg + 8:(s + 1) * seg, :] = x_ref[s * tn:(s + 1) * tn, :]
        for s in range(n_seq):
            base = s * seg + 8 - (CONV_W - 1)
            acc = pad_ref[pl.ds(base, tn), :] * cw[0:1]
            for j in range(1, CONV_W):
                acc = acc + pad_ref[pl.ds(base + j, tn), :] * cw[j:j + 1]
            pieces.append(acc)
        return _silu(jnp.concatenate(pieces, axis=0))

    aq = conv_act(xq_ref, cq_ref, bq_ref)
    ak = conv_act(xk_ref, ck_ref, bk_ref)
    av = conv_act(xv_ref, cv_ref, bv_ref)
    outs = _gdn_intra(aq, ak, av, gates_ref[...], alog_ref[...], dtb_ref[...], h,
                      rows=n_seq * tn, chunk=tn)
    _store_intra(outs, u_ref, w_ref, qg_ref, kdt_ref, qkc_ref, egl_ref)


def _intra_out(m, chunk):
    shapes = [
        jax.ShapeDtypeStruct((m, N_HEADS * HEAD_DIM), F32),
        jax.ShapeDtypeStruct((m, N_HEADS * HEAD_DIM), BF16),
        jax.ShapeDtypeStruct((m, N_HEADS * HEAD_DIM), BF16),
        jax.ShapeDtypeStruct((N_HEADS, HEAD_DIM, m), BF16),
        jax.ShapeDtypeStruct((N_HEADS, m, chunk), BF16),
        jax.ShapeDtypeStruct((m, N_HEADS * HEAD_DIM), F32),
    ]
    return shapes


def _gdn_intra_prompt(proj, gates, conv_w, alog_row, dtb_row, *, batch, t):
    m = proj.shape[0]
    rows = GDN_ROWS
    nblk = t // rows
    c0 = 3 * N_HEADS

    def row_blk(b, h, i):
        return b * nblk + i

    in_specs = [
        pl.BlockSpec((rows, HEAD_DIM), lambda b, h, i: (row_blk(b, h, i), c0 + h)),
        pl.BlockSpec((rows, HEAD_DIM), lambda b, h, i: (row_blk(b, h, i), c0 + N_HEADS + h)),
        pl.BlockSpec((rows, HEAD_DIM), lambda b, h, i: (row_blk(b, h, i), c0 + 2 * N_HEADS + h)),
        pl.BlockSpec((rows, HEAD_DIM), lambda b, h, i: (row_blk(b, h, i), 0)),
        pl.BlockSpec((CONV_W, HEAD_DIM), lambda b, h, i: (0, h)),
        pl.BlockSpec((CONV_W, HEAD_DIM), lambda b, h, i: (0, N_HEADS + h)),
        pl.BlockSpec((CONV_W, HEAD_DIM), lambda b, h, i: (0, 2 * N_HEADS + h)),
        pl.BlockSpec((1, HEAD_DIM), lambda b, h, i: (0, 0)),
        pl.BlockSpec((1, HEAD_DIM), lambda b, h, i: (0, 0)),
    ]
    out_specs = [
        pl.BlockSpec((rows, HEAD_DIM), lambda b, h, i: (row_blk(b, h, i), h)),
        pl.BlockSpec((rows, HEAD_DIM), lambda b, h, i: (row_blk(b, h, i), h)),
        pl.BlockSpec((rows, HEAD_DIM), lambda b, h, i: (row_blk(b, h, i), h)),
        pl.BlockSpec((1, HEAD_DIM, rows), lambda b, h, i: (h, 0, row_blk(b, h, i))),
        pl.BlockSpec((1, rows, CHUNK), lambda b, h, i: (h, row_blk(b, h, i), 0)),
        pl.BlockSpec((rows, HEAD_DIM), lambda b, h, i: (row_blk(b, h, i), h)),
    ]
    return pl.pallas_call(
        functools.partial(_gdn_intra_prompt_kernel, rows=rows, chunk=CHUNK),
        grid=(batch, N_HEADS, nblk),
        in_specs=in_specs,
        out_specs=out_specs,
        out_shape=_intra_out(m, CHUNK),
        scratch_shapes=[pltpu.VMEM((rows + 8, HEAD_DIM), F32)] * 3,
        compiler_params=pltpu.CompilerParams(
            dimension_semantics=("parallel", "parallel", "arbitrary"),
            vmem_limit_bytes=V7X_VMEM_LIMIT_BYTES),
        name="gdn_intra_prompt",
    )(proj, proj, proj, gates, conv_w, conv_w, conv_w, alog_row, dtb_row)


def _gdn_intra_sample(proj, gates, conv_w, alog_row, dtb_row, conv_state, *, n_seq, tn):
    m = proj.shape[0]
    c0 = 3 * N_HEADS
    in_specs = [
        pl.BlockSpec((m, HEAD_DIM), lambda h: (0, c0 + h)),
        pl.BlockSpec((m, HEAD_DIM), lambda h: (0, c0 + N_HEADS + h)),
        pl.BlockSpec((m, HEAD_DIM), lambda h: (0, c0 + 2 * N_HEADS + h)),
        pl.BlockSpec((m, HEAD_DIM), lambda h: (0, 0)),
        pl.BlockSpec((CONV_W, HEAD_DIM), lambda h: (0, h)),
        pl.BlockSpec((CONV_W, HEAD_DIM), lambda h: (0, N_HEADS + h)),
        pl.BlockSpec((CONV_W, HEAD_DIM), lambda h: (0, 2 * N_HEADS + h)),
        pl.BlockSpec((1, HEAD_DIM), lambda h: (0, 0)),
        pl.BlockSpec((1, HEAD_DIM), lambda h: (0, 0)),
        pl.BlockSpec((n_seq, CONV_W - 1, HEAD_DIM), lambda h: (0, 0, h)),
        pl.BlockSpec((n_seq, CONV_W - 1, HEAD_DIM), lambda h: (0, 0, N_HEADS + h)),
        pl.BlockSpec((n_seq, CONV_W - 1, HEAD_DIM), lambda h: (0, 0, 2 * N_HEADS + h)),
    ]
    out_specs = [
        pl.BlockSpec((m, HEAD_DIM), lambda h: (0, h)),
        pl.BlockSpec((m, HEAD_DIM), lambda h: (0, h)),
        pl.BlockSpec((m, HEAD_DIM), lambda h: (0, h)),
        pl.BlockSpec((1, HEAD_DIM, m), lambda h: (h, 0, 0)),
        pl.BlockSpec((1, m, tn), lambda h: (h, 0, 0)),
        pl.BlockSpec((m, HEAD_DIM), lambda h: (0, h)),
    ]
    return pl.pallas_call(
        functools.partial(_gdn_intra_sample_kernel, n_seq=n_seq, tn=tn),
        grid=(N_HEADS,),
        in_specs=in_specs,
        out_specs=out_specs,
        out_shape=_intra_out(m, tn),
        scratch_shapes=[pltpu.VMEM((n_seq * (8 + tn), HEAD_DIM), F32)],
        compiler_params=pltpu.CompilerParams(
            dimension_semantics=("parallel",), vmem_limit_bytes=V7X_VMEM_LIMIT_BYTES),
        name="gdn_intra_sample",
    )(proj, proj, proj, gates, conv_w, conv_w, conv_w, alog_row, dtb_row,
      conv_state, conv_state, conv_state)


def _gdn_scan_kernel(u_ref, w_ref, qg_ref, kdt_ref, qkc_ref, egl_ref, z_ref, s0_ref, dnw_ref,
                     o_ref, s_ref, *, chunk, n_chunks, seqs_per_step):
    t = pl.program_id(1)

    @pl.when(t == 0)
    def _():
        s_ref[...] = s0_ref[...]

    dnw = dnw_ref[...]
    for c in range(n_chunks):
        si = c if seqs_per_step > 1 else 0
        rows = slice(c * chunk, (c + 1) * chunk)
        for h in range(N_HEADS):
            cols = slice(h * HEAD_DIM, (h + 1) * HEAD_DIM)
            s_old = s_ref[si, h]
            sb = s_old.astype(BF16)
            both = _dot(jnp.concatenate([w_ref[rows, cols], qg_ref[rows, cols]], axis=0), sb)
            vn = (u_ref[rows, cols] - both[:chunk]).astype(BF16)
            o = both[chunk:] + _dot(qkc_ref[h, rows, :], vn)
            s_ref[si, h] = s_old * egl_ref[c * chunk:c * chunk + 1, cols] + _dot(kdt_ref[h, :, rows], vn)
            o_ref[rows, cols] = (_rms(o, dnw) * _silu(z_ref[rows, cols])).astype(o_ref.dtype)


def _gdn_scan(intra, z_src, z_col_block, s0, dnw, *, n_seq, t, chunk, chunks_per_step, seqs_per_step=1):
    u, w, qg, kdt, qkc, egl = intra
    m = u.shape[0]
    width = N_HEADS * HEAD_DIM
    rows = chunk * chunks_per_step
    if seqs_per_step > 1:
        assert seqs_per_step == n_seq == chunks_per_step and t == chunk
        nsteps, n_outer = 1, 1
    else:
        nsteps, n_outer = t // rows, n_seq

    def rb(b, i):
        return b * nsteps + i

    in_specs = [
        pl.BlockSpec((rows, width), lambda b, i: (rb(b, i), 0)),
        pl.BlockSpec((rows, width), lambda b, i: (rb(b, i), 0)),
        pl.BlockSpec((rows, width), lambda b, i: (rb(b, i), 0)),
        pl.BlockSpec((N_HEADS, HEAD_DIM, rows), lambda b, i: (0, 0, rb(b, i))),
        pl.BlockSpec((N_HEADS, rows, chunk), lambda b, i: (0, rb(b, i), 0)),
        pl.BlockSpec((rows, width), lambda b, i: (rb(b, i), 0)),
        pl.BlockSpec((rows, width), lambda b, i: (rb(b, i), z_col_block)),
        pl.BlockSpec((seqs_per_step, N_HEADS, HEAD_DIM, HEAD_DIM), lambda b, i: (b, 0, 0, 0)),
        pl.BlockSpec((1, HEAD_DIM), lambda b, i: (0, 0)),
    ]
    out_specs = [
        pl.BlockSpec((rows, width), lambda b, i: (rb(b, i), 0)),
        pl.BlockSpec((seqs_per_step, N_HEADS, HEAD_DIM, HEAD_DIM), lambda b, i: (b, 0, 0, 0)),
    ]
    return pl.pallas_call(
        functools.partial(_gdn_scan_kernel, chunk=chunk, n_chunks=chunks_per_step,
                          seqs_per_step=seqs_per_step),
        grid=(n_outer, nsteps),
        in_specs=in_specs,
        out_specs=out_specs,
        out_shape=[jax.ShapeDtypeStruct((m, width), BF16),
                   jax.ShapeDtypeStruct((n_seq, N_HEADS, HEAD_DIM, HEAD_DIM), F32)],
        compiler_params=pltpu.CompilerParams(
            dimension_semantics=("parallel", "arbitrary"), vmem_limit_bytes=V7X_VMEM_LIMIT_BYTES),
        name="gdn_scan",
    )(u, w, qg, kdt, qkc, egl, z_src, s0, dnw)


def _pad_lanes(v, offset):
    out = jnp.zeros((1, HEAD_DIM), F32)
    return lax.dynamic_update_slice(out, v.reshape(1, -1).astype(F32), (0, offset))


def kernel(x_prompt, x_sample, cache_k, cache_v, state_ssm, state_conv, page_table, ffn1_pre_w, ffn1_post_w, ffn1_gate, ffn1_up, ffn1_down, mix_pre_w, mix_post_w, w_in, conv_w, a_log, dt_bias, delta_norm_w, lambda_q1, lambda_k1, lambda_q2, lambda_k2, subln_w, w_out, ffn2_pre_w, ffn2_post_w, ffn2_gate, ffn2_up, ffn2_down):
    depth = w_in.shape[0]
    batch, t, d = x_prompt.shape
    n_seq, tn, _ = x_sample.shape
    a_w = N_HEADS * HEAD_DIM
    n_main = 3 * a_w + 3 * a_w + a_w
    yp = x_prompt.reshape(batch * t, d)
    ys = x_sample.reshape(n_seq * tn, d)
    outs = [[] for _ in range(8)]
    for l in range(depth):
        lam_init = 0.8 - 0.6 * math.exp(-0.3 * l)
        lam_rows = jnp.stack([lambda_q1[l], lambda_k1[l], lambda_q2[l], lambda_k2[l]]).astype(F32)
        row = lambda v: v[l].reshape(1, -1).astype(F32)
        wg1, wu1, wd1 = ffn1_gate[l].astype(BF16), ffn1_up[l].astype(BF16), ffn1_down[l].astype(BF16)
        wg2, wu2, wd2 = ffn2_gate[l].astype(BF16), ffn2_up[l].astype(BF16), ffn2_down[l].astype(BF16)
        w_main = w_in[l][:, :n_main].astype(BF16)
        w_gates = jnp.pad(w_in[l][:, n_main:], ((0, 0), (0, HEAD_DIM - 2 * N_HEADS))).astype(BF16)
        wo = w_out[l].astype(BF16)
        alog_row = _pad_lanes(a_log[l], N_HEADS)
        dtb_row = _pad_lanes(dt_bias[l], N_HEADS)
        dnw = row(delta_norm_w)
        subw = row(subln_w)
        cw = conv_w[l].astype(F32)

        yp = _ffn(yp, row(ffn1_pre_w), row(ffn1_post_w), wg1, wu1, wd1, tm=512, tf=512)
        ys = _ffn(ys, row(ffn1_pre_w), row(ffn1_post_w), wg1, wu1, wd1, tm=n_seq * tn, tf=512)

        pp, gp = _inproj(yp, row(mix_pre_w), w_main, w_gates, tm=512, tn=1024)
        oa_p = _attn_prompt(lam_rows, subw, pp, batch=batch, t=t, lam_init=lam_init, tq=256, hpp=2)
        intra_p = _gdn_intra_prompt(pp, gp, cw, alog_row, dtb_row, batch=batch, t=t)
        zero_s = jnp.zeros((batch, N_HEADS, HEAD_DIM, HEAD_DIM), F32)
        ob_p, s_p = _gdn_scan(intra_p, pp, 6, zero_s, dnw, n_seq=batch, t=t, chunk=CHUNK,
                              chunks_per_step=GDN_ROWS // CHUNK)
        yp = _outproj(yp, oa_p, ob_p, wo, row(mix_post_w), tm=512)

        ps, gs = _inproj(ys, row(mix_pre_w), w_main, w_gates, tm=n_seq * tn, tn=1024)
        qs = ps[:, :a_w].reshape(n_seq, tn * N_HEADS, HEAD_DIM)
        ks = ps[:, a_w:2 * a_w].reshape(n_seq, tn * N_HEADS, HEAD_DIM)
        vs = ps[:, 2 * a_w:3 * a_w].reshape(n_seq, tn * N_HEADS, HEAD_DIM)
        oa_s = _attn_sample(page_table, lam_rows, subw, qs, ks, vs, cache_k[l], cache_v[l],
                            lam_init=lam_init, group=8)
        oa_s = oa_s.reshape(n_seq * tn, a_w)
        intra_s = _gdn_intra_sample(ps, gs, cw, alog_row, dtb_row, state_conv[l], n_seq=n_seq, tn=tn)
        ob_s, s_s = _gdn_scan(intra_s, ps, 6, state_ssm[l].astype(F32), dnw, n_seq=n_seq, t=tn,
                              chunk=tn, chunks_per_step=n_seq, seqs_per_step=n_seq)
        ys = _outproj(ys, oa_s, ob_s, wo, row(mix_post_w), tm=n_seq * tn)

        yp = _ffn(yp, row(ffn2_pre_w), row(ffn2_post_w), wg2, wu2, wd2, tm=512, tf=512)
        ys = _ffn(ys, row(ffn2_pre_w), row(ffn2_post_w), wg2, wu2, wd2, tm=n_seq * tn, tf=512)

        ppb = pp.reshape(batch, t, -1)
        psb = ps.reshape(n_seq, tn, -1)
        outs[0].append(ppb[:, :, a_w:2 * a_w].reshape(batch, t, N_HEADS, HEAD_DIM))
        outs[1].append(ppb[:, :, 2 * a_w:3 * a_w].reshape(batch, t, N_HEADS, HEAD_DIM))
        outs[2].append(s_p.astype(state_ssm.dtype))
        outs[3].append(ppb[:, t - (CONV_W - 1):, 3 * a_w:6 * a_w])
        outs[4].append(psb[:, :, a_w:2 * a_w].reshape(n_seq, tn, N_HEADS, HEAD_DIM))
        outs[5].append(psb[:, :, 2 * a_w:3 * a_w].reshape(n_seq, tn, N_HEADS, HEAD_DIM))
        outs[6].append(s_s.astype(state_ssm.dtype))
        outs[7].append(jnp.concatenate([state_conv[l].astype(psb.dtype), psb[:, :, 3 * a_w:6 * a_w]],
                                       axis=1)[:, tn:])
    return (yp.reshape(batch, t, d), ys.reshape(n_seq, tn, d)) + tuple(jnp.stack(o) for o in outs)
```

```python
import functools
import math

import jax
import jax.numpy as jnp
from jax import lax
from jax.experimental import pallas as pl
from jax.experimental.pallas import tpu as pltpu

F32 = jnp.float32
BF16 = jnp.bfloat16
EPS = 1e-6
NEG_INF = float("-inf")
LOG2E = 1.4426950408889634

V7X_VMEM_LIMIT_BYTES = 60 * 1024 * 1024

QK_DIM = 64
HEAD_DIM = 128
N_HEADS = 8
CONV_W = 4
CHUNK = 64
GDN_ROWS = 256


def _dot(a, b):
    return jnp.dot(a, b, preferred_element_type=F32)


def _dot_nt(a, b):
    return lax.dot_general(a, b, (((1,), (1,)), ((), ())), preferred_element_type=F32)


def _split3(x):
    hi = x.astype(BF16)
    r = x - hi.astype(F32)
    mid = r.astype(BF16)
    lo = (r - mid.astype(F32)).astype(BF16)
    return hi, mid, lo


def _dot_sel_l(sel, x):
    hi, mid, lo = _split3(x)
    return _dot(sel, hi) + _dot(sel, mid) + _dot(sel, lo)


def _dot_sel_r(x, sel):
    hi, mid, lo = _split3(x)
    return _dot(hi, sel) + _dot(mid, sel) + _dot(lo, sel)


def _dot_hp(a, b):
    ah = a.astype(BF16)
    al = (a - ah.astype(F32)).astype(BF16)
    bh = b.astype(BF16)
    bl = (b - bh.astype(F32)).astype(BF16)
    return _dot(ah, bh) + _dot(ah, bl) + _dot(al, bh)


def _rms(x, w):
    return x * lax.rsqrt(jnp.mean(x * x, axis=-1, keepdims=True) + EPS) * w


def _silu(x):
    return x * jax.nn.sigmoid(x)


def _softplus(x):
    return jnp.maximum(x, 0.0) + jnp.log1p(jnp.exp(-jnp.abs(x)))


def _lambda_full(lam_ref, lam_init):
    l = lam_ref[...]
    a = jnp.sum(l[0:1] * l[1:2], axis=-1, keepdims=True)
    b = jnp.sum(l[2:3] * l[3:4], axis=-1, keepdims=True)
    return jnp.exp(a) - jnp.exp(b) + lam_init


def _idiv(x, n):
    assert n & (n - 1) == 0
    return x >> (n.bit_length() - 1)


def _imod(x, n):
    assert n & (n - 1) == 0
    return x & (n - 1)


def _pow2_neg(k):
    return lax.bitcast_convert_type((127 - k) << 23, F32)


def _ffn_kernel(x_ref, prew_ref, postw_ref, wg_ref, wu_ref, wd_ref, o_ref, h_ref, *, nf):
    f = pl.program_id(1)

    @pl.when(f == 0)
    def _():
        h_ref[...] = _rms(x_ref[...], prew_ref[...]).astype(BF16)
        o_ref[...] = jnp.zeros_like(o_ref)

    h = h_ref[...]
    g = _dot(h, wg_ref[...])
    u = _dot(h, wu_ref[...])
    o_ref[...] += _dot((_silu(g) * u).astype(BF16), wd_ref[...])

    @pl.when(f == nf - 1)
    def _():
        o_ref[...] = x_ref[...] + 0.5 * _rms(o_ref[...], postw_ref[...])


def _ffn(x, pre_w, post_w, wg, wu, wd, *, tm, tf):
    m, d = x.shape
    ff = wg.shape[1]
    nf = ff // tf
    return pl.pallas_call(
        functools.partial(_ffn_kernel, nf=nf),
        grid=(m // tm, nf),
        in_specs=[
            pl.BlockSpec((tm, d), lambda i, f: (i, 0), pipeline_mode=pl.Buffered(1)),
            pl.BlockSpec((1, d), lambda i, f: (0, 0)),
            pl.BlockSpec((1, d), lambda i, f: (0, 0)),
            pl.BlockSpec((d, tf), lambda i, f: (0, f)),
            pl.BlockSpec((d, tf), lambda i, f: (0, f)),
            pl.BlockSpec((tf, d), lambda i, f: (f, 0)),
        ],
        out_specs=pl.BlockSpec((tm, d), lambda i, f: (i, 0)),
        out_shape=jax.ShapeDtypeStruct((m, d), F32),
        scratch_shapes=[pltpu.VMEM((tm, d), BF16)],
        compiler_params=pltpu.CompilerParams(
            dimension_semantics=("parallel", "arbitrary"), vmem_limit_bytes=V7X_VMEM_LIMIT_BYTES),
        name="ffn",
    )(x, pre_w, post_w, wg, wu, wd)


def _inproj_kernel(x_ref, prew_ref, w_ref, wg_ref, o_ref, og_ref, h_ref):
    j = pl.program_id(1)

    @pl.when(j == 0)
    def _():
        h = _rms(x_ref[...], prew_ref[...]).astype(BF16)
        h_ref[...] = h
        og_ref[...] = _dot(h, wg_ref[...])

    o_ref[...] = _dot(h_ref[...], w_ref[...])


def _inproj(x, pre_w, w_main, w_gates, *, tm, tn):
    m, d = x.shape
    n = w_main.shape[1]
    ng = w_gates.shape[1]
    return pl.pallas_call(
        _inproj_kernel,
        grid=(m // tm, n // tn),
        in_specs=[
            pl.BlockSpec((tm, d), lambda i, j: (i, 0), pipeline_mode=pl.Buffered(1)),
            pl.BlockSpec((1, d), lambda i, j: (0, 0)),
            pl.BlockSpec((d, tn), lambda i, j: (0, j)),
            pl.BlockSpec((d, ng), lambda i, j: (0, 0)),
        ],
        out_specs=[
            pl.BlockSpec((tm, tn), lambda i, j: (i, j)),
            pl.BlockSpec((tm, ng), lambda i, j: (i, 0)),
        ],
        out_shape=[jax.ShapeDtypeStruct((m, n), F32), jax.ShapeDtypeStruct((m, ng), F32)],
        scratch_shapes=[pltpu.VMEM((tm, d), BF16)],
        compiler_params=pltpu.CompilerParams(
            dimension_semantics=("parallel", "arbitrary"), vmem_limit_bytes=V7X_VMEM_LIMIT_BYTES),
        name="inproj",
    )(x, pre_w, w_main, w_gates)


def _outproj_kernel(x_ref, oa_ref, ob_ref, wa_ref, wb_ref, postw_ref, o_ref):
    y = _dot(oa_ref[...], wa_ref[...]) + _dot(ob_ref[...], wb_ref[...])
    o_ref[...] = x_ref[...] + _rms(y, postw_ref[...])


def _outproj(x, oa, ob, w_out, post_w, *, tm):
    m, d = x.shape
    ka = oa.shape[1]
    return pl.pallas_call(
        _outproj_kernel,
        grid=(m // tm,),
        in_specs=[
            pl.BlockSpec((tm, d), lambda i: (i, 0)),
            pl.BlockSpec((tm, ka), lambda i: (i, 0)),
            pl.BlockSpec((tm, ka), lambda i: (i, 0)),
            pl.BlockSpec((ka, d), lambda i: (0, 0)),
            pl.BlockSpec((ka, d), lambda i: (1, 0)),
            pl.BlockSpec((1, d), lambda i: (0, 0)),
        ],
        out_specs=pl.BlockSpec((tm, d), lambda i: (i, 0)),
        out_shape=jax.ShapeDtypeStruct((m, d), F32),
        compiler_params=pltpu.CompilerParams(
            dimension_semantics=("parallel",), vmem_limit_bytes=V7X_VMEM_LIMIT_BYTES),
        name="outproj",
    )(x, oa, ob, w_out, w_out, post_w)


def _attn_prompt_kernel(lam_ref, subw_ref, q_ref, k_ref, v_ref, o_ref,
                        kb_ref, vb_ref, bias_ref, *, t, tq, hpp, lam_init):
    hg = pl.program_id(1)
    nq = t // tq
    lam = _lambda_full(lam_ref, lam_init)
    lane = lax.broadcasted_iota(jnp.int32, (1, HEAD_DIM), 1)
    first_map = lane < QK_DIM
    r = lax.broadcasted_iota(jnp.int32, (tq, tq), 0)
    c = lax.broadcasted_iota(jnp.int32, (tq, tq), 1)
    on_or_below = jnp.concatenate([r >= c, r >= c], axis=0)
    slopes = [_pow2_neg(jnp.full((1, 1), hg * hpp + hh + 1, jnp.int32)) * LOG2E for hh in range(hpp)]
    for hh in range(hpp):
        bias_ref[hh] = slopes[hh] * (c - r).astype(F32)
    kb_ref[...] = k_ref[...].astype(BF16)
    vb_ref[...] = v_ref[...].astype(BF16)
    scale = QK_DIM ** -0.5 * LOG2E

    def lane_fold(x):
        out = x[:, :HEAD_DIM]
        for i in range(1, x.shape[1] // HEAD_DIM):
            out = out + x[:, i * HEAD_DIM:(i + 1) * HEAD_DIM]
        return out

    def q_block(i, _):
        rows_i = pl.ds(pl.multiple_of(i * tq, tq), tq)
        qs = []
        for hh in range(hpp):
            q = q_ref[rows_i, hh * HEAD_DIM:(hh + 1) * HEAD_DIM] * scale
            qs.append(jnp.concatenate(
                [jnp.where(first_map, q, 0.0), jnp.where(first_map, 0.0, q)], axis=0).astype(BF16))
        def kv_block(j, state, masked):
            rows_j = pl.ds(pl.multiple_of(j * tq, tq), tq)
            blocks_away = jnp.asarray((i - j) * tq, F32)
            out = []
            for hh in range(hpp):
                cols = slice(hh * HEAD_DIM, (hh + 1) * HEAD_DIM)
                m_old, l_old, acc_old = state[hh]
                b = bias_ref[hh]
                s = _dot_nt(qs[hh], kb_ref[rows_j, cols]) + jnp.concatenate([b, b], axis=0)
                if masked:
                    s = jnp.where(on_or_below, s, NEG_INF)
                far = slopes[hh] * blocks_away
                m_new = jnp.maximum(m_old, jnp.max(s, axis=-1, keepdims=True) - far)
                p = jnp.exp2(s - (far + m_new))
                alpha = jnp.exp2(m_old - m_new)
                out.append((m_new, alpha * l_old + lane_fold(p),
                            alpha * acc_old + _dot(p.astype(BF16), vb_ref[rows_j, cols])))
            return tuple(out)

        init = tuple((jnp.full((2 * tq, 1), NEG_INF, F32), jnp.zeros((2 * tq, HEAD_DIM), F32),
                      jnp.zeros((2 * tq, HEAD_DIM), F32)) for _ in range(hpp))
        state = lax.fori_loop(0, i, lambda j, st: kv_block(j, st, False), init)
        state = kv_block(i, state, True)
        for hh in range(hpp):
            _, l_fin, acc_fin = state[hh]
            o = acc_fin / jnp.sum(l_fin, axis=-1, keepdims=True)
            d = o[:tq] - lam * o[tq:]
            o_ref[rows_i, hh * HEAD_DIM:(hh + 1) * HEAD_DIM] = (
                _rms(d, subw_ref[...]) * (1.0 - lam_init)).astype(o_ref.dtype)
        return 0

    lax.fori_loop(0, nq, q_block, 0)


def _attn_prompt(lam_rows, subln_w, proj, *, batch, t, lam_init, tq, hpp):
    m = proj.shape[0]
    ng = N_HEADS // hpp
    w = hpp * HEAD_DIM
    return pl.pallas_call(
        functools.partial(_attn_prompt_kernel, t=t, tq=tq, hpp=hpp, lam_init=lam_init),
        grid=(batch, ng),
        in_specs=[
            pl.BlockSpec((4, QK_DIM), lambda b, g: (0, 0)),
            pl.BlockSpec((1, HEAD_DIM), lambda b, g: (0, 0)),
            pl.BlockSpec((t, w), lambda b, g: (b, g)),
            pl.BlockSpec((t, w), lambda b, g: (b, ng + g)),
            pl.BlockSpec((t, w), lambda b, g: (b, 2 * ng + g)),
        ],
        out_specs=pl.BlockSpec((t, w), lambda b, g: (b, g)),
        out_shape=jax.ShapeDtypeStruct((m, N_HEADS * HEAD_DIM), BF16),
        scratch_shapes=[
            pltpu.VMEM((t, w), BF16),
            pltpu.VMEM((t, w), BF16),
            pltpu.VMEM((hpp, tq, tq), F32),
        ],
        compiler_params=pltpu.CompilerParams(
            dimension_semantics=("parallel", "parallel"), vmem_limit_bytes=V7X_VMEM_LIMIT_BYTES),
        name="attn_prompt",
    )(lam_rows, subln_w, proj, proj, proj)


def _attn_sample_kernel(pt_ref, lam_ref, subw_ref, q_ref, kn_ref, vn_ref, *rest,
                        n_pages, page, tn, group, lam_init):
    del pt_ref
    kc_refs = rest[:group]
    vc_refs = rest[group:2 * group]
    o_ref, qb_ref, bias_ref, m_ref, l_ref, acc_ref = rest[2 * group:]
    p = pl.program_id(1)
    nrow = 2 * tn * N_HEADS
    ncol = page * N_HEADS
    past = n_pages * page
    row = lax.broadcasted_iota(jnp.int32, (nrow, 1), 0)
    slope = _pow2_neg(_imod(row, N_HEADS) + 1)

    @pl.when(p == 0)
    def _():
        q = q_ref[0] * (QK_DIM ** -0.5)
        lane = lax.broadcasted_iota(jnp.int32, (1, HEAD_DIM), 1)
        first_map = lane < QK_DIM
        qb_ref[...] = jnp.concatenate(
            [jnp.where(first_map, q, 0.0), jnp.where(first_map, 0.0, q)], axis=0).astype(BF16)
        rr = lax.broadcasted_iota(jnp.int32, (nrow, ncol), 0)
        cc = lax.broadcasted_iota(jnp.int32, (nrow, ncol), 1)
        tq_ = _idiv(_imod(rr, tn * N_HEADS), N_HEADS)
        same_head = _imod(rr, N_HEADS) == _imod(cc, N_HEADS)
        bias_ref[...] = jnp.where(same_head, slope * (_idiv(cc, N_HEADS) - tq_).astype(F32), NEG_INF)
        m_ref[...] = jnp.full(m_ref.shape, NEG_INF, F32)
        l_ref[...] = jnp.zeros(l_ref.shape, F32)
        acc_ref[...] = jnp.zeros(acc_ref.shape, F32)

    def lane_fold(x, op):
        out = x[:, :HEAD_DIM]
        for i in range(1, x.shape[1] // HEAD_DIM):
            out = op(out, x[:, i * HEAD_DIM:(i + 1) * HEAD_DIM])
        return out

    def update(scores, offs, vals):
        m_old = m_ref[...]
        m_loc = lane_fold(scores[0], jnp.maximum) - offs[0]
        for s, off in zip(scores[1:], offs[1:]):
            m_loc = jnp.maximum(m_loc, lane_fold(s, jnp.maximum) - off)
        m_new = jnp.maximum(m_old, jnp.max(m_loc, axis=-1, keepdims=True))
        alpha = jnp.exp(m_old - m_new)
        pes = [jnp.exp(s - (off + m_new)) for s, off in zip(scores, offs)]
        l_part = lane_fold(pes[0], jnp.add)
        for pe in pes[1:]:
            l_part = l_part + lane_fold(pe, jnp.add)
        pv = _dot(jnp.concatenate([pe.astype(BF16) for pe in pes], axis=1),
                  jnp.concatenate(vals, axis=0))
        l_ref[...] = alpha * l_ref[...] + l_part
        acc_ref[...] = alpha * acc_ref[...] + pv
        m_ref[...] = m_new

    qb = qb_ref[...]
    bias = bias_ref[...]
    scores, offs, vals = [], [], []
    for g in range(group):
        kf = kc_refs[g][0].reshape(ncol, HEAD_DIM).astype(BF16)
        vals.append(vc_refs[g][0].reshape(ncol, HEAD_DIM).astype(BF16))
        scores.append(_dot_nt(qb, kf) + bias)
        offs.append(slope * jnp.asarray(past - (p * group + g) * page, F32))
    update(scores, offs, vals)

    @pl.when(p == n_pages // group - 1)
    def _():
        ncn = tn * N_HEADS
        rr = lax.broadcasted_iota(jnp.int32, (nrow, HEAD_DIM), 0)
        cc = lax.broadcasted_iota(jnp.int32, (nrow, HEAD_DIM), 1)
        tq_ = _idiv(_imod(rr, ncn), N_HEADS)
        tk_ = _idiv(cc, N_HEADS)
        ok = (cc < ncn) & (_imod(rr, N_HEADS) == _imod(cc, N_HEADS)) & (tk_ <= tq_)
        bias_new = jnp.where(ok, slope * (tk_ - tq_).astype(F32), NEG_INF)
        pad = jnp.zeros((HEAD_DIM - ncn, HEAD_DIM), BF16)
        kn = jnp.concatenate([kn_ref[0].astype(BF16), pad], axis=0)
        vn = jnp.concatenate([vn_ref[0].astype(BF16), pad], axis=0)
        update([_dot_nt(qb, kn) + bias_new], [jnp.zeros((nrow, 1), F32)], [vn])
        lam = _lambda_full(lam_ref, lam_init)
        o = acc_ref[...] / jnp.sum(l_ref[...], axis=-1, keepdims=True)
        d = o[:ncn] - lam * o[ncn:]
        o_ref[0] = (_rms(d, subw_ref[...]) * (1.0 - lam_init)).astype(o_ref.dtype)


def _attn_sample(page_table, lam_rows, subln_w, q, k_new, v_new, cache_k, cache_v, *, lam_init, group):
    n_seq, n_pages = page_table.shape
    page = cache_k.shape[1]
    tn = q.shape[1] // N_HEADS
    nrow = 2 * tn * N_HEADS
    ncol = page * N_HEADS
    assert n_pages % group == 0

    def page_spec(g):
        return pl.BlockSpec((1, page, N_HEADS, HEAD_DIM),
                            lambda b, p, pt: (pt[b, p * group + g], 0, 0, 0))

    new_spec = pl.BlockSpec((1, tn * N_HEADS, HEAD_DIM), lambda b, p, pt: (b, 0, 0))
    grid_spec = pltpu.PrefetchScalarGridSpec(
        num_scalar_prefetch=1,
        grid=(n_seq, n_pages // group),
        in_specs=[
            pl.BlockSpec((4, QK_DIM), lambda b, p, pt: (0, 0)),
            pl.BlockSpec((1, HEAD_DIM), lambda b, p, pt: (0, 0)),
            new_spec, new_spec, new_spec,
        ] + [page_spec(g) for g in range(group)] * 2,
        out_specs=new_spec,
        scratch_shapes=[
            pltpu.VMEM((nrow, HEAD_DIM), BF16),
            pltpu.VMEM((nrow, ncol), F32),
            pltpu.VMEM((nrow, 1), F32),
            pltpu.VMEM((nrow, HEAD_DIM), F32),
            pltpu.VMEM((nrow, HEAD_DIM), F32),
        ],
    )
    return pl.pallas_call(
        functools.partial(_attn_sample_kernel, n_pages=n_pages, page=page, tn=tn, group=group,
                          lam_init=lam_init),
        grid_spec=grid_spec,
        out_shape=jax.ShapeDtypeStruct((n_seq, tn * N_HEADS, HEAD_DIM), BF16),
        compiler_params=pltpu.CompilerParams(
            dimension_semantics=("parallel", "arbitrary"), vmem_limit_bytes=V7X_VMEM_LIMIT_BYTES),
        name="attn_sample",
    )(page_table, lam_rows, subln_w, q, k_new, v_new, *([cache_k] * group), *([cache_v] * group))


def _gdn_n_masks(chunk):
    return 3 + (chunk.bit_length() - 1)


def _gdn_fill_masks(mask_ref, *, rows, chunk):
    ri = lax.broadcasted_iota(jnp.int32, (rows, rows), 0)
    ci = lax.broadcasted_iota(jnp.int32, (rows, rows), 1)
    same = _idiv(ri, chunk) == _idiv(ci, chunk)
    mask_ref[0] = (same & (ri >= ci)).astype(F32)
    mask_ref[1] = (same & (ri > ci)).astype(F32)
    mask_ref[2] = (ri == ci).astype(F32)
    s, k = 1, 3
    while s < chunk:
        mask_ref[k] = ((_idiv(ri, 2 * s) == _idiv(ci, 2 * s)) & (_imod(_idiv(ri, s), 2) == 1)
                       & (_imod(_idiv(ci, s), 2) == 0)).astype(F32)
        s, k = 2 * s, k + 1


def _gdn_intra_all(act, gates, alog_row, dtb_row, mask_ref, u_ref, w_ref, qg_ref, kdt_ref, qkc_ref,
                   egl_ref, *, rows, chunk):
    n_chunks = rows // chunk
    n_levels = _gdn_n_masks(chunk) - 3
    beta_all = jax.nn.sigmoid(gates)
    g_all = -jnp.exp(alog_row) * _softplus(gates + dtb_row)
    gcum_all = _dot_sel_l(mask_ref[0].astype(BF16), g_all)
    gtot_all = jnp.concatenate(
        [jnp.broadcast_to(gcum_all[(c + 1) * chunk - 1:(c + 1) * chunk], (chunk, HEAD_DIM))
         for c in range(n_chunks)], axis=0)
    gcum_t = gcum_all.T
    fr = lax.broadcasted_iota(jnp.int32, (rows, chunk), 0)
    fc = lax.broadcasted_iota(jnp.int32, (rows, chunk), 1)
    fold = (_imod(fr, chunk) == fc).astype(BF16)
    er = lax.broadcasted_iota(jnp.int32, (HEAD_DIM, HEAD_DIM), 0)
    eye_b = (er == lax.broadcasted_iota(jnp.int32, (HEAD_DIM, HEAD_DIM), 1)).astype(BF16)

    def lanes(col):
        return jnp.broadcast_to(col, (rows, HEAD_DIM))

    def square(x):
        if rows >= HEAD_DIM:
            return jnp.concatenate([x] * (rows // HEAD_DIM), axis=1)
        return x[:, :rows]

    heads = range(N_HEADS)
    egl_rows, lmats, lbs, tinvs, rhss = [], [], [], [], []
    for h in heads:
        cols = slice(h * HEAD_DIM, (h + 1) * HEAD_DIM)
        aq = act[:, h * HEAD_DIM:(h + 1) * HEAD_DIM]
        ak = act[:, (N_HEADS + h) * HEAD_DIM:(N_HEADS + h + 1) * HEAD_DIM]
        av = act[:, (2 * N_HEADS + h) * HEAD_DIM:(2 * N_HEADS + h + 1) * HEAD_DIM]
        qn = aq * lax.rsqrt(jnp.sum(aq * aq, axis=-1, keepdims=True) + 1e-6) * (HEAD_DIM ** -0.5)
        kn = ak * lax.rsqrt(jnp.sum(ak * ak, axis=-1, keepdims=True) + 1e-6)
        beta = lanes(beta_all[:, h:h + 1])
        g_cum = lanes(gcum_all[:, N_HEADS + h:N_HEADS + h + 1])
        g_tot = lanes(gtot_all[:, N_HEADS + h:N_HEADS + h + 1])
        g_row = gcum_t[N_HEADS + h:N_HEADS + h + 1, :]
        decay = jnp.exp(jnp.minimum(square(g_cum) - g_row, 0.0)) * mask_ref[0]

        kbeta = kn * beta
        knb = kn.astype(BF16)
        lmat = mask_ref[1] * (_dot_nt(kbeta.astype(BF16), knb) * decay)
        lmats.append(lmat)
        lbs.append(lmat.astype(BF16))
        tinvs.append(mask_ref[2] - lmat * mask_ref[3])
        e_cum = jnp.exp(g_cum)
        rhss.append(jnp.concatenate([av * beta, kbeta * e_cum], axis=1).astype(BF16))
        qk = (_dot_nt(qn.astype(BF16), knb) * decay).astype(BF16)
        qkc_ref[h] = _dot(qk, fold).astype(BF16)
        qg_ref[:, cols] = (qn * e_cum).astype(BF16)
        kd = (kn * jnp.exp(g_tot - g_cum)).astype(BF16)
        kdt_ref[h] = _dot_nt(eye_b, kd).astype(BF16)
        egl = jnp.exp(g_tot)
        egl_rows.append(jnp.concatenate(
            [egl[c * chunk:c * chunk + 1] for c in range(n_chunks)]
            + [jnp.zeros((8 - n_chunks, HEAD_DIM), F32)] * (n_chunks < 8), axis=0))
    egl_ref[0] = jnp.concatenate(egl_rows, axis=1)

    for k in range(1, n_levels):
        tbs = [tinvs[h].astype(BF16) for h in heads]
        mids = [_dot(tbs[h], lbs[h]).astype(BF16) for h in heads]
        tinvs = [tinvs[h] - mask_ref[3 + k] * _dot(mids[h], tbs[h]) for h in heads]
    resids = [mask_ref[2] - tinvs[h] - _dot_hp(lmats[h], tinvs[h]) for h in heads]
    tinvs = [tinvs[h] + _dot(tinvs[h].astype(BF16), resids[h].astype(BF16)) for h in heads]
    for h in heads:
        cols = slice(h * HEAD_DIM, (h + 1) * HEAD_DIM)
        uw = _dot(tinvs[h].astype(BF16), rhss[h])
        u_ref[:, cols] = uw[:, :HEAD_DIM]
        w_ref[:, cols] = uw[:, HEAD_DIM:].astype(BF16)


def _gdn_intra_prompt_kernel(x_ref, gates_ref, cw_ref, alog_ref, dtb_ref,
                             u_ref, w_ref, qg_ref, kdt_ref, qkc_ref, egl_ref, pad_ref, mask_ref,
                             *, rows, chunk):
    t = pl.program_id(1)

    @pl.when(t == 0)
    def _():
        pad_ref[0:8, :] = jnp.zeros((8, pad_ref.shape[1]), F32)
        _gdn_fill_masks(mask_ref, rows=rows, chunk=chunk)

    pad_ref[8:, :] = x_ref[...]
    cw = cw_ref[...]
    acc = pad_ref[pl.ds(8 - (CONV_W - 1), rows), :] * cw[0:1]
    for j in range(1, CONV_W):
        acc = acc + pad_ref[pl.ds(8 - (CONV_W - 1) + j, rows), :] * cw[j:j + 1]
    pad_ref[0:8, :] = x_ref[rows - 8:, :]
    _gdn_intra_all(_silu(acc), gates_ref[...], alog_ref[...], dtb_ref[...], mask_ref,
                   u_ref, w_ref, qg_ref, kdt_ref, qkc_ref, egl_ref, rows=rows, chunk=chunk)


def _gdn_intra_sample_kernel(x_ref, gates_ref, cw_ref, alog_ref, dtb_ref, buf_ref,
                             u_ref, w_ref, qg_ref, kdt_ref, qkc_ref, egl_ref, pad_ref, mask_ref,
                             *, n_seq, tn):
    seg = 8 + tn
    _gdn_fill_masks(mask_ref, rows=n_seq * tn, chunk=tn)
    cw = cw_ref[...]
    for s in range(n_seq):
        pad_ref[s * seg + 8 - (CONV_W - 1):s * seg + 8, :] = buf_ref[s]
        pad_ref[s * seg + 8:(s + 1) * seg, :] = x_ref[s * tn:(s + 1) * tn, :]
    pieces = []
    for s in range(n_seq):
        base = s * seg + 8 - (CONV_W - 1)
        acc = pad_ref[pl.ds(base, tn), :] * cw[0:1]
        for j in range(1, CONV_W):
            acc = acc + pad_ref[pl.ds(base + j, tn), :] * cw[j:j + 1]
        pieces.append(acc)
    _gdn_intra_all(_silu(jnp.concatenate(pieces, axis=0)), gates_ref[...], alog_ref[...], dtb_ref[...],
                   mask_ref, u_ref, w_ref, qg_ref, kdt_ref, qkc_ref, egl_ref, rows=n_seq * tn, chunk=tn)


def _intra_out(m, chunk, n_steps):
    width = N_HEADS * HEAD_DIM
    return [
        jax.ShapeDtypeStruct((m, width), F32),
        jax.ShapeDtypeStruct((m, width), BF16),
        jax.ShapeDtypeStruct((m, width), BF16),
        jax.ShapeDtypeStruct((N_HEADS, HEAD_DIM, m), BF16),
        jax.ShapeDtypeStruct((N_HEADS, m, chunk), BF16),
        jax.ShapeDtypeStruct((n_steps, 8, width), F32),
    ]


def _gdn_intra_prompt(proj, gates, conv_w, alog_row, dtb_row, *, batch, t):
    m = proj.shape[0]
    rows = GDN_ROWS
    nblk = t // rows
    width = N_HEADS * HEAD_DIM
    cwid = 3 * width

    def rb(b, i):
        return b * nblk + i

    in_specs = [
        pl.BlockSpec((rows, cwid), lambda b, i: (rb(b, i), 1)),
        pl.BlockSpec((rows, HEAD_DIM), lambda b, i: (rb(b, i), 0)),
        pl.BlockSpec((CONV_W, cwid), lambda b, i: (0, 0)),
        pl.BlockSpec((1, HEAD_DIM), lambda b, i: (0, 0)),
        pl.BlockSpec((1, HEAD_DIM), lambda b, i: (0, 0)),
    ]
    out_specs = [
        pl.BlockSpec((rows, width), lambda b, i: (rb(b, i), 0)),
        pl.BlockSpec((rows, width), lambda b, i: (rb(b, i), 0)),
        pl.BlockSpec((rows, width), lambda b, i: (rb(b, i), 0)),
        pl.BlockSpec((N_HEADS, HEAD_DIM, rows), lambda b, i: (0, 0, rb(b, i))),
        pl.BlockSpec((N_HEADS, rows, CHUNK), lambda b, i: (0, rb(b, i), 0)),
        pl.BlockSpec((1, 8, width), lambda b, i: (rb(b, i), 0, 0)),
    ]
    return pl.pallas_call(
        functools.partial(_gdn_intra_prompt_kernel, rows=rows, chunk=CHUNK),
        grid=(batch, nblk),
        in_specs=in_specs,
        out_specs=out_specs,
        out_shape=_intra_out(m, CHUNK, batch * nblk),
        scratch_shapes=[pltpu.VMEM((rows + 8, cwid), F32),
                        pltpu.VMEM((_gdn_n_masks(CHUNK), rows, rows), F32)],
        compiler_params=pltpu.CompilerParams(
            dimension_semantics=("parallel", "arbitrary"), vmem_limit_bytes=V7X_VMEM_LIMIT_BYTES),
        name="gdn_intra_prompt",
    )(proj, gates, conv_w, alog_row, dtb_row)


def _gdn_intra_sample(proj, gates, conv_w, alog_row, dtb_row, conv_state, *, n_seq, tn):
    m = proj.shape[0]
    width = N_HEADS * HEAD_DIM
    cwid = 3 * width
    in_specs = [
        pl.BlockSpec((m, cwid), lambda i: (0, 1)),
        pl.BlockSpec((m, HEAD_DIM), lambda i: (0, 0)),
        pl.BlockSpec((CONV_W, cwid), lambda i: (0, 0)),
        pl.BlockSpec((1, HEAD_DIM), lambda i: (0, 0)),
        pl.BlockSpec((1, HEAD_DIM), lambda i: (0, 0)),
        pl.BlockSpec((n_seq, CONV_W - 1, cwid), lambda i: (0, 0, 0)),
    ]
    out_specs = [
        pl.BlockSpec((m, width), lambda i: (0, 0)),
        pl.BlockSpec((m, width), lambda i: (0, 0)),
        pl.BlockSpec((m, width), lambda i: (0, 0)),
        pl.BlockSpec((N_HEADS, HEAD_DIM, m), lambda i: (0, 0, 0)),
        pl.BlockSpec((N_HEADS, m, tn), lambda i: (0, 0, 0)),
        pl.BlockSpec((1, 8, width), lambda i: (0, 0, 0)),
    ]
    return pl.pallas_call(
        functools.partial(_gdn_intra_sample_kernel, n_seq=n_seq, tn=tn),
        grid=(1,),
        in_specs=in_specs,
        out_specs=out_specs,
        out_shape=_intra_out(m, tn, 1),
        scratch_shapes=[pltpu.VMEM((n_seq * (8 + tn), cwid), F32),
                        pltpu.VMEM((_gdn_n_masks(tn), m, m), F32)],
        compiler_params=pltpu.CompilerParams(
            dimension_semantics=("arbitrary",), vmem_limit_bytes=V7X_VMEM_LIMIT_BYTES),
        name="gdn_intra_sample",
    )(proj, gates, conv_w, alog_row, dtb_row, conv_state)


def _gdn_scan_kernel(u_ref, w_ref, qg_ref, kdt_ref, qkc_ref, egl_ref, z_ref, s0_ref, dnw_ref,
                     o_ref, s_ref, *, chunk, n_chunks, seqs_per_step):
    t = pl.program_id(1)

    @pl.when(t == 0)
    def _():
        s_ref[...] = s0_ref[...]

    dnw = dnw_ref[...]
    for c in range(n_chunks):
        si = c if seqs_per_step > 1 else 0
        rows = slice(c * chunk, (c + 1) * chunk)
        for h in range(N_HEADS):
            cols = slice(h * HEAD_DIM, (h + 1) * HEAD_DIM)
            s_old = s_ref[si, h]
            sb = s_old.astype(BF16)
            both = _dot(jnp.concatenate([w_ref[rows, cols], qg_ref[rows, cols]], axis=0), sb)
            vn = (u_ref[rows, cols] - both[:chunk]).astype(BF16)
            o = both[chunk:] + _dot(qkc_ref[h, rows, :], vn)
            s_ref[si, h] = s_old * egl_ref[0, c:c + 1, cols] + _dot(kdt_ref[h, :, rows], vn)
            o_ref[rows, cols] = (_rms(o, dnw) * _silu(z_ref[rows, cols])).astype(o_ref.dtype)


def _gdn_scan(intra, z_src, z_col_block, s0, dnw, *, n_seq, t, chunk, chunks_per_step, seqs_per_step=1):
    u, w, qg, kdt, qkc, egl = intra
    m = u.shape[0]
    width = N_HEADS * HEAD_DIM
    rows = chunk * chunks_per_step
    if seqs_per_step > 1:
        assert seqs_per_step == n_seq == chunks_per_step and t == chunk
        nsteps, n_outer = 1, 1
    else:
        nsteps, n_outer = t // rows, n_seq

    def rb(b, i):
        return b * nsteps + i

    in_specs = [
        pl.BlockSpec((rows, width), lambda b, i: (rb(b, i), 0)),
        pl.BlockSpec((rows, width), lambda b, i: (rb(b, i), 0)),
        pl.BlockSpec((rows, width), lambda b, i: (rb(b, i), 0)),
        pl.BlockSpec((N_HEADS, HEAD_DIM, rows), lambda b, i: (0, 0, rb(b, i))),
        pl.BlockSpec((N_HEADS, rows, chunk), lambda b, i: (0, rb(b, i), 0)),
        pl.BlockSpec((1, 8, width), lambda b, i: (rb(b, i), 0, 0)),
        pl.BlockSpec((rows, width), lambda b, i: (rb(b, i), z_col_block)),
        pl.BlockSpec((seqs_per_step, N_HEADS, HEAD_DIM, HEAD_DIM), lambda b, i: (b, 0, 0, 0)),
        pl.BlockSpec((1, HEAD_DIM), lambda b, i: (0, 0)),
    ]
    out_specs = [
        pl.BlockSpec((rows, width), lambda b, i: (rb(b, i), 0)),
        pl.BlockSpec((seqs_per_step, N_HEADS, HEAD_DIM, HEAD_DIM), lambda b, i: (b, 0, 0, 0)),
    ]
    return pl.pallas_call(
        functools.partial(_gdn_scan_kernel, chunk=chunk, n_chunks=chunks_per_step,
                          seqs_per_step=seqs_per_step),
        grid=(n_outer, nsteps),
        in_specs=in_specs,
        out_specs=out_specs,
        out_shape=[jax.ShapeDtypeStruct((m, width), BF16),
                   jax.ShapeDtypeStruct((n_seq, N_HEADS, HEAD_DIM, HEAD_DIM), F32)],
        compiler_params=pltpu.CompilerParams(
            dimension_semantics=("parallel", "arbitrary"), vmem_limit_bytes=V7X_VMEM_LIMIT_BYTES),
        name="gdn_scan",
    )(u, w, qg, kdt, qkc, egl, z_src, s0, dnw)


def _pad_lanes(v, offset):
    out = jnp.zeros((1, HEAD_DIM), F32)
    return lax.dynamic_update_slice(out, v.reshape(1, -1).astype(F32), (0, offset))


def kernel(x_prompt, x_sample, cache_k, cache_v, state_ssm, state_conv, page_table, ffn1_pre_w, ffn1_post_w, ffn1_gate, ffn1_up, ffn1_down, mix_pre_w, mix_post_w, w_in, conv_w, a_log, dt_bias, delta_norm_w, lambda_q1, lambda_k1, lambda_q2, lambda_k2, subln_w, w_out, ffn2_pre_w, ffn2_post_w, ffn2_gate, ffn2_up, ffn2_down):
    depth = w_in.shape[0]
    batch, t, d = x_prompt.shape
    n_seq, tn, _ = x_sample.shape
    a_w = N_HEADS * HEAD_DIM
    n_main = 3 * a_w + 3 * a_w + a_w
    yp = x_prompt.reshape(batch * t, d)
    ys = x_sample.reshape(n_seq * tn, d)
    outs = [[] for _ in range(8)]
    for l in range(depth):
        lam_init = 0.8 - 0.6 * math.exp(-0.3 * l)
        lam_rows = jnp.stack([lambda_q1[l], lambda_k1[l], lambda_q2[l], lambda_k2[l]]).astype(F32)
        row = lambda v: v[l].reshape(1, -1).astype(F32)
        wg1, wu1, wd1 = ffn1_gate[l].astype(BF16), ffn1_up[l].astype(BF16), ffn1_down[l].astype(BF16)
        wg2, wu2, wd2 = ffn2_gate[l].astype(BF16), ffn2_up[l].astype(BF16), ffn2_down[l].astype(BF16)
        w_main = w_in[l][:, :n_main].astype(BF16)
        w_gates = jnp.pad(w_in[l][:, n_main:], ((0, 0), (0, HEAD_DIM - 2 * N_HEADS))).astype(BF16)
        wo = w_out[l].astype(BF16)
        alog_row = _pad_lanes(a_log[l], N_HEADS)
        dtb_row = _pad_lanes(dt_bias[l], N_HEADS)
        dnw = row(delta_norm_w)
        subw = row(subln_w)
        cw = conv_w[l].astype(F32)

        yp = _ffn(yp, row(ffn1_pre_w), row(ffn1_post_w), wg1, wu1, wd1, tm=512, tf=1408)
        ys = _ffn(ys, row(ffn1_pre_w), row(ffn1_post_w), wg1, wu1, wd1, tm=n_seq * tn, tf=1408)

        pp, gp = _inproj(yp, row(mix_pre_w), w_main, w_gates, tm=1024, tn=1792)
        oa_p = _attn_prompt(lam_rows, subw, pp, batch=batch, t=t, lam_init=lam_init, tq=256, hpp=2)
        intra_p = _gdn_intra_prompt(pp, gp, cw, alog_row, dtb_row, batch=batch, t=t)
        zero_s = jnp.zeros((batch, N_HEADS, HEAD_DIM, HEAD_DIM), F32)
        ob_p, s_p = _gdn_scan(intra_p, pp, 6, zero_s, dnw, n_seq=batch, t=t, chunk=CHUNK,
                              chunks_per_step=GDN_ROWS // CHUNK)
        yp = _outproj(yp, oa_p, ob_p, wo, row(mix_post_w), tm=512)

        ps, gs = _inproj(ys, row(mix_pre_w), w_main, w_gates, tm=n_seq * tn, tn=1792)
        qs = ps[:, :a_w].reshape(n_seq, tn * N_HEADS, HEAD_DIM)
        ks = ps[:, a_w:2 * a_w].reshape(n_seq, tn * N_HEADS, HEAD_DIM)
        vs = ps[:, 2 * a_w:3 * a_w].reshape(n_seq, tn * N_HEADS, HEAD_DIM)
        oa_s = _attn_sample(page_table, lam_rows, subw, qs, ks, vs, cache_k[l], cache_v[l],
                            lam_init=lam_init, group=8)
        oa_s = oa_s.reshape(n_seq * tn, a_w)
        intra_s = _gdn_intra_sample(ps, gs, cw, alog_row, dtb_row, state_conv[l], n_seq=n_seq, tn=tn)
        ob_s, s_s = _gdn_scan(intra_s, ps, 6, state_ssm[l].astype(F32), dnw, n_seq=n_seq, t=tn,
                              chunk=tn, chunks_per_step=n_seq, seqs_per_step=n_seq)
        ys = _outproj(ys, oa_s, ob_s, wo, row(mix_post_w), tm=n_seq * tn)

        yp = _ffn(yp, row(ffn2_pre_w), row(ffn2_post_w), wg2, wu2, wd2, tm=512, tf=1408)
        ys = _ffn(ys, row(ffn2_pre_w), row(ffn2_post_w), wg2, wu2, wd2, tm=n_seq * tn, tf=1408)

        ppb = pp.reshape(batch, t, -1)
        psb = ps.reshape(n_seq, tn, -1)
        outs[0].append(ppb[:, :, a_w:2 * a_w].reshape(batch, t, N_HEADS, HEAD_DIM))
        outs[1].append(ppb[:, :, 2 * a_w:3 * a_w].reshape(batch, t, N_HEADS, HEAD_DIM))
        outs[2].append(s_p.astype(state_ssm.dtype))
        outs[3].append(ppb[:, t - (CONV_W - 1):, 3 * a_w:6 * a_w])
        outs[4].append(psb[:, :, a_w:2 * a_w].reshape(n_seq, tn, N_HEADS, HEAD_DIM))
        outs[5].append(psb[:, :, 2 * a_w:3 * a_w].reshape(n_seq, tn, N_HEADS, HEAD_DIM))
        outs[6].append(s_s.astype(state_ssm.dtype))
        outs[7].append(jnp.concatenate([state_conv[l].astype(psb.dtype), psb[:, :, 3 * a_w:6 * a_w]],
                                       axis=1)[:, tn:])
    return (yp.reshape(batch, t, d), ys.reshape(n_seq, tn, d)) + tuple(jnp.stack(o) for o in outs)
```

```python
import functools
import math

import jax
import jax.numpy as jnp
from jax import lax
from jax.experimental import pallas as pl
from jax.experimental.pallas import tpu as pltpu

F32 = jnp.float32
BF16 = jnp.bfloat16
EPS = 1e-6
NEG_INF = float("-inf")
LOG2E = 1.4426950408889634

V7X_VMEM_LIMIT_BYTES = 60 * 1024 * 1024

QK_DIM = 64
HEAD_DIM = 128
N_HEADS = 8
CONV_W = 4
CHUNK = 64
GDN_ROWS = 256


def _dot(a, b):
    return jnp.dot(a, b, preferred_element_type=F32)


def _dot_nt(a, b):
    return lax.dot_general(a, b, (((1,), (1,)), ((), ())), preferred_element_type=F32)


def _split3(x):
    hi = x.astype(BF16)
    r = x - hi.astype(F32)
    mid = r.astype(BF16)
    lo = (r - mid.astype(F32)).astype(BF16)
    return hi, mid, lo


def _dot_sel_l(sel, x):
    hi, mid, lo = _split3(x)
    return _dot(sel, hi) + _dot(sel, mid) + _dot(sel, lo)


def _dot_sel_r(x, sel):
    hi, mid, lo = _split3(x)
    return _dot(hi, sel) + _dot(mid, sel) + _dot(lo, sel)


def _dot_hp(a, b):
    ah = a.astype(BF16)
    al = (a - ah.astype(F32)).astype(BF16)
    bh = b.astype(BF16)
    bl = (b - bh.astype(F32)).astype(BF16)
    return _dot(ah, bh) + _dot(ah, bl) + _dot(al, bh)


def _rms(x, w):
    return x * lax.rsqrt(jnp.mean(x * x, axis=-1, keepdims=True) + EPS) * w


def _silu(x):
    return x * jax.nn.sigmoid(x)


def _softplus(x):
    return jnp.maximum(x, 0.0) + jnp.log1p(jnp.exp(-jnp.abs(x)))


def _lambda_full(lam_ref, lam_init):
    l = lam_ref[...]
    a = jnp.sum(l[0:1] * l[1:2], axis=-1, keepdims=True)
    b = jnp.sum(l[2:3] * l[3:4], axis=-1, keepdims=True)
    return jnp.exp(a) - jnp.exp(b) + lam_init


def _idiv(x, n):
    assert n & (n - 1) == 0
    return x >> (n.bit_length() - 1)


def _imod(x, n):
    assert n & (n - 1) == 0
    return x & (n - 1)


def _pow2_neg(k):
    return lax.bitcast_convert_type((127 - k) << 23, F32)


def _ffn_kernel(x_ref, prew_ref, postw_ref, wg_ref, wu_ref, wd_ref, o_ref, h_ref, *, nf):
    f = pl.program_id(1)

    @pl.when(f == 0)
    def _():
        h_ref[...] = _rms(x_ref[...], prew_ref[...]).astype(BF16)
        o_ref[...] = jnp.zeros_like(o_ref)

    h = h_ref[...]
    g = _dot(h, wg_ref[...])
    u = _dot(h, wu_ref[...])
    o_ref[...] += _dot((_silu(g) * u).astype(BF16), wd_ref[...])

    @pl.when(f == nf - 1)
    def _():
        o_ref[...] = x_ref[...] + 0.5 * _rms(o_ref[...], postw_ref[...])


def _ffn(x, pre_w, post_w, wg, wu, wd, *, tm, tf):
    m, d = x.shape
    ff = wg.shape[1]
    nf = ff // tf
    return pl.pallas_call(
        functools.partial(_ffn_kernel, nf=nf),
        grid=(m // tm, nf),
        in_specs=[
            pl.BlockSpec((tm, d), lambda i, f: (i, 0), pipeline_mode=pl.Buffered(1)),
            pl.BlockSpec((1, d), lambda i, f: (0, 0)),
            pl.BlockSpec((1, d), lambda i, f: (0, 0)),
            pl.BlockSpec((d, tf), lambda i, f: (0, f)),
            pl.BlockSpec((d, tf), lambda i, f: (0, f)),
            pl.BlockSpec((tf, d), lambda i, f: (f, 0)),
        ],
        out_specs=pl.BlockSpec((tm, d), lambda i, f: (i, 0)),
        out_shape=jax.ShapeDtypeStruct((m, d), F32),
        scratch_shapes=[pltpu.VMEM((tm, d), BF16)],
        compiler_params=pltpu.CompilerParams(
            dimension_semantics=("parallel", "arbitrary"), vmem_limit_bytes=V7X_VMEM_LIMIT_BYTES),
        name="ffn",
    )(x, pre_w, post_w, wg, wu, wd)


def _inproj_kernel(x_ref, prew_ref, w_ref, wg_ref, o_ref, og_ref, h_ref):
    j = pl.program_id(1)

    @pl.when(j == 0)
    def _():
        h = _rms(x_ref[...], prew_ref[...]).astype(BF16)
        h_ref[...] = h
        og_ref[...] = _dot(h, wg_ref[...])

    o_ref[...] = _dot(h_ref[...], w_ref[...])


def _inproj(x, pre_w, w_main, w_gates, *, tm, tn):
    m, d = x.shape
    n = w_main.shape[1]
    ng = w_gates.shape[1]
    return pl.pallas_call(
        _inproj_kernel,
        grid=(m // tm, n // tn),
        in_specs=[
            pl.BlockSpec((tm, d), lambda i, j: (i, 0), pipeline_mode=pl.Buffered(1)),
            pl.BlockSpec((1, d), lambda i, j: (0, 0)),
            pl.BlockSpec((d, tn), lambda i, j: (0, j)),
            pl.BlockSpec((d, ng), lambda i, j: (0, 0)),
        ],
        out_specs=[
            pl.BlockSpec((tm, tn), lambda i, j: (i, j)),
            pl.BlockSpec((tm, ng), lambda i, j: (i, 0)),
        ],
        out_shape=[jax.ShapeDtypeStruct((m, n), F32), jax.ShapeDtypeStruct((m, ng), F32)],
        scratch_shapes=[pltpu.VMEM((tm, d), BF16)],
        compiler_params=pltpu.CompilerParams(
            dimension_semantics=("parallel", "arbitrary"), vmem_limit_bytes=V7X_VMEM_LIMIT_BYTES),
        name="inproj",
    )(x, pre_w, w_main, w_gates)


def _outproj_kernel(x_ref, oa_ref, ob_ref, wa_ref, wb_ref, postw_ref, o_ref):
    y = _dot(oa_ref[...], wa_ref[...]) + _dot(ob_ref[...], wb_ref[...])
    o_ref[...] = x_ref[...] + _rms(y, postw_ref[...])


def _outproj(x, oa, ob, w_out, post_w, *, tm):
    m, d = x.shape
    ka = oa.shape[1]
    return pl.pallas_call(
        _outproj_kernel,
        grid=(m // tm,),
        in_specs=[
            pl.BlockSpec((tm, d), lambda i: (i, 0)),
            pl.BlockSpec((tm, ka), lambda i: (i, 0)),
            pl.BlockSpec((tm, ka), lambda i: (i, 0)),
            pl.BlockSpec((ka, d), lambda i: (0, 0)),
            pl.BlockSpec((ka, d), lambda i: (1, 0)),
            pl.BlockSpec((1, d), lambda i: (0, 0)),
        ],
        out_specs=pl.BlockSpec((tm, d), lambda i: (i, 0)),
        out_shape=jax.ShapeDtypeStruct((m, d), F32),
        compiler_params=pltpu.CompilerParams(
            dimension_semantics=("parallel",), vmem_limit_bytes=V7X_VMEM_LIMIT_BYTES),
        name="outproj",
    )(x, oa, ob, w_out, w_out, post_w)


def _attn_prompt_kernel(lam_ref, subw_ref, q_ref, k_ref, v_ref, o_ref,
                        kb_ref, vt_ref, bias_ref, *, t, tq, hpp, lam_init):
    hg = pl.program_id(1)
    nq = t // tq
    lam = _lambda_full(lam_ref, lam_init)
    lane = lax.broadcasted_iota(jnp.int32, (1, HEAD_DIM), 1)
    first_map = lane < QK_DIM
    kk = lax.broadcasted_iota(jnp.int32, (tq, tq), 0)
    qq = lax.broadcasted_iota(jnp.int32, (tq, tq), 1)
    slopes = [_pow2_neg(jnp.full((1, 1), hg * hpp + hh + 1, jnp.int32)) * LOG2E for hh in range(hpp)]
    er = lax.broadcasted_iota(jnp.int32, (HEAD_DIM, HEAD_DIM), 0)
    eye_b = (er == lax.broadcasted_iota(jnp.int32, (HEAD_DIM, HEAD_DIM), 1)).astype(BF16)
    for hh in range(hpp):
        cols = slice(hh * HEAD_DIM, (hh + 1) * HEAD_DIM)
        b = slopes[hh] * (kk - qq).astype(F32)
        bias_ref[hh, 0] = b
        bias_ref[hh, 1] = jnp.where(qq >= kk, b, NEG_INF)
        vt_ref[hh] = _dot_nt(eye_b, v_ref[:, cols].astype(BF16)).astype(BF16)
    kb_ref[...] = k_ref[...].astype(BF16)
    scale = QK_DIM ** -0.5 * LOG2E
    subw_col = subw_ref[...]

    def q_block(i, _):
        rows_i = pl.ds(pl.multiple_of(i * tq, tq), tq)
        qs = []
        for hh in range(hpp):
            q = q_ref[rows_i, hh * HEAD_DIM:(hh + 1) * HEAD_DIM] * scale
            qs.append(jnp.concatenate(
                [jnp.where(first_map, q, 0.0), jnp.where(first_map, 0.0, q)], axis=0).astype(BF16))

        def scores(j):
            rows_j = pl.ds(pl.multiple_of(j * tq, tq), tq)
            diag = jnp.asarray(j == i, jnp.int32)
            out = []
            for hh in range(hpp):
                b = bias_ref[hh, diag]
                out.append(_dot_nt(kb_ref[rows_j, hh * HEAD_DIM:(hh + 1) * HEAD_DIM], qs[hh])
                           + jnp.concatenate([b, b], axis=1))
            return tuple(out)

        def consume(j, sts, state):
            rows_j = pl.ds(pl.multiple_of(j * tq, tq), tq)
            blocks_away = jnp.asarray((i - j) * tq, F32)
            out = []
            for hh in range(hpp):
                m_old, l_old, acc_old = state[hh]
                far = slopes[hh] * blocks_away
                m_new = jnp.maximum(m_old, jnp.max(sts[hh], axis=0, keepdims=True) - far)
                p = jnp.exp2(sts[hh] - (far + m_new))
                alpha = jnp.exp2(m_old - m_new)
                out.append((m_new, alpha * l_old + jnp.sum(p, axis=0, keepdims=True),
                            alpha * acc_old + _dot(vt_ref[hh, :, rows_j], p.astype(BF16))))
            return tuple(out)

        def step(j, carry):
            sts, state = carry
            nxt = scores(j + 1)
            return nxt, consume(j, sts, state)

        init = tuple((jnp.full((1, 2 * tq), NEG_INF, F32), jnp.zeros((1, 2 * tq), F32),
                      jnp.zeros((HEAD_DIM, 2 * tq), F32)) for _ in range(hpp))
        sts, state = lax.fori_loop(0, i, step, (scores(0), init))
        state = consume(i, sts, state)
        for hh in range(hpp):
            _, l_fin, acc_fin = state[hh]
            ot = acc_fin / l_fin
            dt = ot[:, :tq] - lam * ot[:, tq:]
            dt = dt * lax.rsqrt(jnp.mean(dt * dt, axis=0, keepdims=True) + EPS) * subw_col
            o_ref[rows_i, hh * HEAD_DIM:(hh + 1) * HEAD_DIM] = (
                (dt * (1.0 - lam_init)).T.astype(o_ref.dtype))
        return 0

    lax.fori_loop(0, nq, q_block, 0)


def _attn_prompt(lam_rows, subln_col, proj, *, batch, t, lam_init, tq, hpp):
    m = proj.shape[0]
    ng = N_HEADS // hpp
    w = hpp * HEAD_DIM
    return pl.pallas_call(
        functools.partial(_attn_prompt_kernel, t=t, tq=tq, hpp=hpp, lam_init=lam_init),
        grid=(batch, ng),
        in_specs=[
            pl.BlockSpec((4, QK_DIM), lambda b, g: (0, 0)),
            pl.BlockSpec((HEAD_DIM, 1), lambda b, g: (0, 0)),
            pl.BlockSpec((t, w), lambda b, g: (b, g)),
            pl.BlockSpec((t, w), lambda b, g: (b, ng + g)),
            pl.BlockSpec((t, w), lambda b, g: (b, 2 * ng + g)),
        ],
        out_specs=pl.BlockSpec((t, w), lambda b, g: (b, g)),
        out_shape=jax.ShapeDtypeStruct((m, N_HEADS * HEAD_DIM), BF16),
        scratch_shapes=[
            pltpu.VMEM((t, w), BF16),
            pltpu.VMEM((hpp, HEAD_DIM, t), BF16),
            pltpu.VMEM((hpp, 2, tq, tq), F32),
        ],
        compiler_params=pltpu.CompilerParams(
            dimension_semantics=("parallel", "parallel"), vmem_limit_bytes=V7X_VMEM_LIMIT_BYTES),
        name="attn_prompt",
    )(lam_rows, subln_col, proj, proj, proj)


def _attn_sample_kernel(pt_ref, lam_ref, subw_ref, q_ref, kn_ref, vn_ref, *rest,
                        n_pages, page, tn, group, lam_init):
    del pt_ref
    kc_refs = rest[:group]
    vc_refs = rest[group:2 * group]
    o_ref, qb_ref, bias_ref, m_ref, l_ref, acc_ref = rest[2 * group:]
    p = pl.program_id(1)
    nrow = 2 * tn * N_HEADS
    ncol = page * N_HEADS
    past = n_pages * page
    row = lax.broadcasted_iota(jnp.int32, (nrow, 1), 0)
    slope = _pow2_neg(_imod(row, N_HEADS) + 1)

    @pl.when(p == 0)
    def _():
        q = q_ref[0] * (QK_DIM ** -0.5)
        lane = lax.broadcasted_iota(jnp.int32, (1, HEAD_DIM), 1)
        first_map = lane < QK_DIM
        qb_ref[...] = jnp.concatenate(
            [jnp.where(first_map, q, 0.0), jnp.where(first_map, 0.0, q)], axis=0).astype(BF16)
        rr = lax.broadcasted_iota(jnp.int32, (nrow, ncol), 0)
        cc = lax.broadcasted_iota(jnp.int32, (nrow, ncol), 1)
        tq_ = _idiv(_imod(rr, tn * N_HEADS), N_HEADS)
        same_head = _imod(rr, N_HEADS) == _imod(cc, N_HEADS)
        bias_ref[...] = jnp.where(same_head, slope * (_idiv(cc, N_HEADS) - tq_).astype(F32), NEG_INF)
        m_ref[...] = jnp.full(m_ref.shape, NEG_INF, F32)
        l_ref[...] = jnp.zeros(l_ref.shape, F32)
        acc_ref[...] = jnp.zeros(acc_ref.shape, F32)

    def lane_fold(x, op):
        out = x[:, :HEAD_DIM]
        for i in range(1, x.shape[1] // HEAD_DIM):
            out = op(out, x[:, i * HEAD_DIM:(i + 1) * HEAD_DIM])
        return out

    def update(scores, offs, vals):
        m_old = m_ref[...]
        m_loc = lane_fold(scores[0], jnp.maximum) - offs[0]
        for s, off in zip(scores[1:], offs[1:]):
            m_loc = jnp.maximum(m_loc, lane_fold(s, jnp.maximum) - off)
        m_new = jnp.maximum(m_old, jnp.max(m_loc, axis=-1, keepdims=True))
        alpha = jnp.exp(m_old - m_new)
        pes = [jnp.exp(s - (off + m_new)) for s, off in zip(scores, offs)]
        l_part = lane_fold(pes[0], jnp.add)
        for pe in pes[1:]:
            l_part = l_part + lane_fold(pe, jnp.add)
        pv = _dot(jnp.concatenate([pe.astype(BF16) for pe in pes], axis=1),
                  jnp.concatenate(vals, axis=0))
        l_ref[...] = alpha * l_ref[...] + l_part
        acc_ref[...] = alpha * acc_ref[...] + pv
        m_ref[...] = m_new

    qb = qb_ref[...]
    bias = bias_ref[...]
    scores, offs, vals = [], [], []
    for g in range(group):
        kf = kc_refs[g][0].reshape(ncol, HEAD_DIM).astype(BF16)
        vals.append(vc_refs[g][0].reshape(ncol, HEAD_DIM).astype(BF16))
        scores.append(_dot_nt(qb, kf) + bias)
        offs.append(slope * jnp.asarray(past - (p * group + g) * page, F32))
    update(scores, offs, vals)

    @pl.when(p == n_pages // group - 1)
    def _():
        ncn = tn * N_HEADS
        rr = lax.broadcasted_iota(jnp.int32, (nrow, HEAD_DIM), 0)
        cc = lax.broadcasted_iota(jnp.int32, (nrow, HEAD_DIM), 1)
        tq_ = _idiv(_imod(rr, ncn), N_HEADS)
        tk_ = _idiv(cc, N_HEADS)
        ok = (cc < ncn) & (_imod(rr, N_HEADS) == _imod(cc, N_HEADS)) & (tk_ <= tq_)
        bias_new = jnp.where(ok, slope * (tk_ - tq_).astype(F32), NEG_INF)
        pad = jnp.zeros((HEAD_DIM - ncn, HEAD_DIM), BF16)
        kn = jnp.concatenate([kn_ref[0].astype(BF16), pad], axis=0)
        vn = jnp.concatenate([vn_ref[0].astype(BF16), pad], axis=0)
        update([_dot_nt(qb, kn) + bias_new], [jnp.zeros((nrow, 1), F32)], [vn])
        lam = _lambda_full(lam_ref, lam_init)
        o = acc_ref[...] / jnp.sum(l_ref[...], axis=-1, keepdims=True)
        d = o[:ncn] - lam * o[ncn:]
        o_ref[0] = (_rms(d, subw_ref[...]) * (1.0 - lam_init)).astype(o_ref.dtype)


def _attn_sample(page_table, lam_rows, subln_w, q, k_new, v_new, cache_k, cache_v, *, lam_init, group):
    n_seq, n_pages = page_table.shape
    page = cache_k.shape[1]
    tn = q.shape[1] // N_HEADS
    nrow = 2 * tn * N_HEADS
    ncol = page * N_HEADS
    assert n_pages % group == 0

    def page_spec(g):
        return pl.BlockSpec((1, page, N_HEADS, HEAD_DIM),
                            lambda b, p, pt: (pt[b, p * group + g], 0, 0, 0))

    new_spec = pl.BlockSpec((1, tn * N_HEADS, HEAD_DIM), lambda b, p, pt: (b, 0, 0))
    grid_spec = pltpu.PrefetchScalarGridSpec(
        num_scalar_prefetch=1,
        grid=(n_seq, n_pages // group),
        in_specs=[
            pl.BlockSpec((4, QK_DIM), lambda b, p, pt: (0, 0)),
            pl.BlockSpec((1, HEAD_DIM), lambda b, p, pt: (0, 0)),
            new_spec, new_spec, new_spec,
        ] + [page_spec(g) for g in range(group)] * 2,
        out_specs=new_spec,
        scratch_shapes=[
            pltpu.VMEM((nrow, HEAD_DIM), BF16),
            pltpu.VMEM((nrow, ncol), F32),
            pltpu.VMEM((nrow, 1), F32),
            pltpu.VMEM((nrow, HEAD_DIM), F32),
            pltpu.VMEM((nrow, HEAD_DIM), F32),
        ],
    )
    return pl.pallas_call(
        functools.partial(_attn_sample_kernel, n_pages=n_pages, page=page, tn=tn, group=group,
                          lam_init=lam_init),
        grid_spec=grid_spec,
        out_shape=jax.ShapeDtypeStruct((n_seq, tn * N_HEADS, HEAD_DIM), BF16),
        compiler_params=pltpu.CompilerParams(
            dimension_semantics=("parallel", "arbitrary"), vmem_limit_bytes=V7X_VMEM_LIMIT_BYTES),
        name="attn_sample",
    )(page_table, lam_rows, subln_w, q, k_new, v_new, *([cache_k] * group), *([cache_v] * group))


def _gdn_n_masks(chunk):
    return 3 + (chunk.bit_length() - 1)


def _gdn_fill_masks(mask_ref, *, rows, chunk):
    ri = lax.broadcasted_iota(jnp.int32, (rows, rows), 0)
    ci = lax.broadcasted_iota(jnp.int32, (rows, rows), 1)
    same = _idiv(ri, chunk) == _idiv(ci, chunk)
    mask_ref[0] = (same & (ri >= ci)).astype(F32)
    mask_ref[1] = (same & (ri > ci)).astype(F32)
    mask_ref[2] = (ri == ci).astype(F32)
    s, k = 1, 3
    while s < chunk:
        mask_ref[k] = ((_idiv(ri, 2 * s) == _idiv(ci, 2 * s)) & (_imod(_idiv(ri, s), 2) == 1)
                       & (_imod(_idiv(ci, s), 2) == 0)).astype(F32)
        s, k = 2 * s, k + 1


def _gdn_intra_all(act, gates, alog_row, dtb_row, mask_ref, u_ref, w_ref, qg_ref, kdt_ref, qkc_ref,
                   egl_ref, *, rows, chunk):
    n_chunks = rows // chunk
    n_levels = _gdn_n_masks(chunk) - 3
    beta_all = jax.nn.sigmoid(gates)
    g_all = -jnp.exp(alog_row) * _softplus(gates + dtb_row)
    gcum_all = _dot_sel_l(mask_ref[0].astype(BF16), g_all)
    gtot_all = jnp.concatenate(
        [jnp.broadcast_to(gcum_all[(c + 1) * chunk - 1:(c + 1) * chunk], (chunk, HEAD_DIM))
         for c in range(n_chunks)], axis=0)
    gcum_t = gcum_all.T
    fr = lax.broadcasted_iota(jnp.int32, (rows, chunk), 0)
    fc = lax.broadcasted_iota(jnp.int32, (rows, chunk), 1)
    fold = (_imod(fr, chunk) == fc).astype(BF16)
    er = lax.broadcasted_iota(jnp.int32, (HEAD_DIM, HEAD_DIM), 0)
    eye_b = (er == lax.broadcasted_iota(jnp.int32, (HEAD_DIM, HEAD_DIM), 1)).astype(BF16)

    def lanes(col):
        return jnp.broadcast_to(col, (rows, HEAD_DIM))

    def square(x):
        if rows >= HEAD_DIM:
            return jnp.concatenate([x] * (rows // HEAD_DIM), axis=1)
        return x[:, :rows]

    heads = range(N_HEADS)
    egl_rows, lmats, lbs, tinvs, rhss = [], [], [], [], []
    for h in heads:
        cols = slice(h * HEAD_DIM, (h + 1) * HEAD_DIM)
        aq = act[:, h * HEAD_DIM:(h + 1) * HEAD_DIM]
        ak = act[:, (N_HEADS + h) * HEAD_DIM:(N_HEADS + h + 1) * HEAD_DIM]
        av = act[:, (2 * N_HEADS + h) * HEAD_DIM:(2 * N_HEADS + h + 1) * HEAD_DIM]
        qn = aq * lax.rsqrt(jnp.sum(aq * aq, axis=-1, keepdims=True) + 1e-6) * (HEAD_DIM ** -0.5)
        kn = ak * lax.rsqrt(jnp.sum(ak * ak, axis=-1, keepdims=True) + 1e-6)
        beta = lanes(beta_all[:, h:h + 1])
        g_cum = lanes(gcum_all[:, N_HEADS + h:N_HEADS + h + 1])
        g_tot = lanes(gtot_all[:, N_HEADS + h:N_HEADS + h + 1])
        g_row = gcum_t[N_HEADS + h:N_HEADS + h + 1, :]
        decay = jnp.exp(jnp.minimum(square(g_cum) - g_row, 0.0)) * mask_ref[0]

        kbeta = kn * beta
        knb = kn.astype(BF16)
        lmat = mask_ref[1] * (_dot_nt(kbeta.astype(BF16), knb) * decay)
        lmats.append(lmat)
        lbs.append(lmat.astype(BF16))
        tinvs.append(mask_ref[2] - lmat * mask_ref[3])
        e_cum = jnp.exp(g_cum)
        rhss.append(jnp.concatenate([av * beta, kbeta * e_cum], axis=1).astype(BF16))
        qk = (_dot_nt(qn.astype(BF16), knb) * decay).astype(BF16)
        qkc_ref[h] = _dot(qk, fold).astype(BF16)
        qg_ref[:, cols] = (qn * e_cum).astype(BF16)
        kd = (kn * jnp.exp(g_tot - g_cum)).astype(BF16)
        kdt_ref[h] = _dot_nt(eye_b, kd).astype(BF16)
        egl = jnp.exp(g_tot)
        egl_rows.append(jnp.concatenate(
            [egl[c * chunk:c * chunk + 1] for c in range(n_chunks)]
            + [jnp.zeros((8 - n_chunks, HEAD_DIM), F32)] * (n_chunks < 8), axis=0))
    egl_ref[0] = jnp.concatenate(egl_rows, axis=1)

    for k in range(1, n_levels):
        tbs = [tinvs[h].astype(BF16) for h in heads]
        mids = [_dot(tbs[h], lbs[h]).astype(BF16) for h in heads]
        tinvs = [tinvs[h] - mask_ref[3 + k] * _dot(mids[h], tbs[h]) for h in heads]
    resids = [mask_ref[2] - tinvs[h] - _dot_hp(lmats[h], tinvs[h]) for h in heads]
    tinvs = [tinvs[h] + _dot(tinvs[h].astype(BF16), resids[h].astype(BF16)) for h in heads]
    for h in heads:
        cols = slice(h * HEAD_DIM, (h + 1) * HEAD_DIM)
        uw = _dot(tinvs[h].astype(BF16), rhss[h])
        u_ref[:, cols] = uw[:, :HEAD_DIM]
        w_ref[:, cols] = uw[:, HEAD_DIM:].astype(BF16)


def _gdn_intra_prompt_kernel(x_ref, gates_ref, cw_ref, alog_ref, dtb_ref,
                             u_ref, w_ref, qg_ref, kdt_ref, qkc_ref, egl_ref, pad_ref, mask_ref,
                             *, rows, chunk):
    t = pl.program_id(1)

    @pl.when(t == 0)
    def _():
        pad_ref[0:8, :] = jnp.zeros((8, pad_ref.shape[1]), F32)
        _gdn_fill_masks(mask_ref, rows=rows, chunk=chunk)

    pad_ref[8:, :] = x_ref[...]
    cw = cw_ref[...]
    acc = pad_ref[pl.ds(8 - (CONV_W - 1), rows), :] * cw[0:1]
    for j in range(1, CONV_W):
        acc = acc + pad_ref[pl.ds(8 - (CONV_W - 1) + j, rows), :] * cw[j:j + 1]
    pad_ref[0:8, :] = x_ref[rows - 8:, :]
    _gdn_intra_all(_silu(acc), gates_ref[...], alog_ref[...], dtb_ref[...], mask_ref,
                   u_ref, w_ref, qg_ref, kdt_ref, qkc_ref, egl_ref, rows=rows, chunk=chunk)


def _gdn_intra_sample_kernel(x_ref, gates_ref, cw_ref, alog_ref, dtb_ref, buf_ref,
                             u_ref, w_ref, qg_ref, kdt_ref, qkc_ref, egl_ref, pad_ref, mask_ref,
                             *, n_seq, tn):
    seg = 8 + tn
    _gdn_fill_masks(mask_ref, rows=n_seq * tn, chunk=tn)
    cw = cw_ref[...]
    for s in range(n_seq):
        pad_ref[s * seg + 8 - (CONV_W - 1):s * seg + 8, :] = buf_ref[s]
        pad_ref[s * seg + 8:(s + 1) * seg, :] = x_ref[s * tn:(s + 1) * tn, :]
    pieces = []
    for s in range(n_seq):
        base = s * seg + 8 - (CONV_W - 1)
        acc = pad_ref[pl.ds(base, tn), :] * cw[0:1]
        for j in range(1, CONV_W):
            acc = acc + pad_ref[pl.ds(base + j, tn), :] * cw[j:j + 1]
        pieces.append(acc)
    _gdn_intra_all(_silu(jnp.concatenate(pieces, axis=0)), gates_ref[...], alog_ref[...], dtb_ref[...],
                   mask_ref, u_ref, w_ref, qg_ref, kdt_ref, qkc_ref, egl_ref, rows=n_seq * tn, chunk=tn)


def _intra_out(m, chunk, n_steps):
    width = N_HEADS * HEAD_DIM
    return [
        jax.ShapeDtypeStruct((m, width), F32),
        jax.ShapeDtypeStruct((m, width), BF16),
        jax.ShapeDtypeStruct((m, width), BF16),
        jax.ShapeDtypeStruct((N_HEADS, HEAD_DIM, m), BF16),
        jax.ShapeDtypeStruct((N_HEADS, m, chunk), BF16),
        jax.ShapeDtypeStruct((n_steps, 8, width), F32),
    ]


def _gdn_intra_prompt(proj, gates, conv_w, alog_row, dtb_row, *, batch, t):
    m = proj.shape[0]
    rows = GDN_ROWS
    nblk = t // rows
    width = N_HEADS * HEAD_DIM
    cwid = 3 * width

    def rb(b, i):
        return b * nblk + i

    in_specs = [
        pl.BlockSpec((rows, cwid), lambda b, i: (rb(b, i), 1)),
        pl.BlockSpec((rows, HEAD_DIM), lambda b, i: (rb(b, i), 0)),
        pl.BlockSpec((CONV_W, cwid), lambda b, i: (0, 0)),
        pl.BlockSpec((1, HEAD_DIM), lambda b, i: (0, 0)),
        pl.BlockSpec((1, HEAD_DIM), lambda b, i: (0, 0)),
    ]
    out_specs = [
        pl.BlockSpec((rows, width), lambda b, i: (rb(b, i), 0)),
        pl.BlockSpec((rows, width), lambda b, i: (rb(b, i), 0)),
        pl.BlockSpec((rows, width), lambda b, i: (rb(b, i), 0)),
        pl.BlockSpec((N_HEADS, HEAD_DIM, rows), lambda b, i: (0, 0, rb(b, i))),
        pl.BlockSpec((N_HEADS, rows, CHUNK), lambda b, i: (0, rb(b, i), 0)),
        pl.BlockSpec((1, 8, width), lambda b, i: (rb(b, i), 0, 0)),
    ]
    return pl.pallas_call(
        functools.partial(_gdn_intra_prompt_kernel, rows=rows, chunk=CHUNK),
        grid=(batch, nblk),
        in_specs=in_specs,
        out_specs=out_specs,
        out_shape=_intra_out(m, CHUNK, batch * nblk),
        scratch_shapes=[pltpu.VMEM((rows + 8, cwid), F32),
                        pltpu.VMEM((_gdn_n_masks(CHUNK), rows, rows), F32)],
        compiler_params=pltpu.CompilerParams(
            dimension_semantics=("parallel", "arbitrary"), vmem_limit_bytes=V7X_VMEM_LIMIT_BYTES),
        name="gdn_intra_prompt",
    )(proj, gates, conv_w, alog_row, dtb_row)


def _gdn_intra_sample(proj, gates, conv_w, alog_row, dtb_row, conv_state, *, n_seq, tn):
    m = proj.shape[0]
    width = N_HEADS * HEAD_DIM
    cwid = 3 * width
    in_specs = [
        pl.BlockSpec((m, cwid), lambda i: (0, 1)),
        pl.BlockSpec((m, HEAD_DIM), lambda i: (0, 0)),
        pl.BlockSpec((CONV_W, cwid), lambda i: (0, 0)),
        pl.BlockSpec((1, HEAD_DIM), lambda i: (0, 0)),
        pl.BlockSpec((1, HEAD_DIM), lambda i: (0, 0)),
        pl.BlockSpec((n_seq, CONV_W - 1, cwid), lambda i: (0, 0, 0)),
    ]
    out_specs = [
        pl.BlockSpec((m, width), lambda i: (0, 0)),
        pl.BlockSpec((m, width), lambda i: (0, 0)),
        pl.BlockSpec((m, width), lambda i: (0, 0)),
        pl.BlockSpec((N_HEADS, HEAD_DIM, m), lambda i: (0, 0, 0)),
        pl.BlockSpec((N_HEADS, m, tn), lambda i: (0, 0, 0)),
        pl.BlockSpec((1, 8, width), lambda i: (0, 0, 0)),
    ]
    return pl.pallas_call(
        functools.partial(_gdn_intra_sample_kernel, n_seq=n_seq, tn=tn),
        grid=(1,),
        in_specs=in_specs,
        out_specs=out_specs,
        out_shape=_intra_out(m, tn, 1),
        scratch_shapes=[pltpu.VMEM((n_seq * (8 + tn), cwid), F32),
                        pltpu.VMEM((_gdn_n_masks(tn), m, m), F32)],
        compiler_params=pltpu.CompilerParams(
            dimension_semantics=("arbitrary",), vmem_limit_bytes=V7X_VMEM_LIMIT_BYTES),
        name="gdn_intra_sample",
    )(proj, gates, conv_w, alog_row, dtb_row, conv_state)


def _gdn_scan_kernel(u_ref, w_ref, qg_ref, kdt_ref, qkc_ref, egl_ref, z_ref, s0_ref, dnw_ref,
                     o_ref, s_ref, *, chunk, n_chunks, seqs_per_step):
    t = pl.program_id(1)

    @pl.when(t == 0)
    def _():
        s_ref[...] = s0_ref[...]

    dnw = dnw_ref[...]
    for c in range(n_chunks):
        si = c if seqs_per_step > 1 else 0
        rows = slice(c * chunk, (c + 1) * chunk)
        for h in range(N_HEADS):
            cols = slice(h * HEAD_DIM, (h + 1) * HEAD_DIM)
            s_old = s_ref[si, h]
            sb = s_old.astype(BF16)
            both = _dot(jnp.concatenate([w_ref[rows, cols], qg_ref[rows, cols]], axis=0), sb)
            vn = (u_ref[rows, cols] - both[:chunk]).astype(BF16)
            o = both[chunk:] + _dot(qkc_ref[h, rows, :], vn)
            s_ref[si, h] = s_old * egl_ref[0, c:c + 1, cols] + _dot(kdt_ref[h, :, rows], vn)
            o_ref[rows, cols] = (_rms(o, dnw) * _silu(z_ref[rows, cols])).astype(o_ref.dtype)


def _gdn_scan(intra, z_src, z_col_block, s0, dnw, *, n_seq, t, chunk, chunks_per_step, seqs_per_step=1):
    u, w, qg, kdt, qkc, egl = intra
    m = u.shape[0]
    width = N_HEADS * HEAD_DIM
    rows = chunk * chunks_per_step
    if seqs_per_step > 1:
        assert seqs_per_step == n_seq == chunks_per_step and t == chunk
        nsteps, n_outer = 1, 1
    else:
        nsteps, n_outer = t // rows, n_seq

    def rb(b, i):
        return b * nsteps + i

    in_specs = [
        pl.BlockSpec((rows, width), lambda b, i: (rb(b, i), 0)),
        pl.BlockSpec((rows, width), lambda b, i: (rb(b, i), 0)),
        pl.BlockSpec((rows, width), lambda b, i: (rb(b, i), 0)),
        pl.BlockSpec((N_HEADS, HEAD_DIM, rows), lambda b, i: (0, 0, rb(b, i))),
        pl.BlockSpec((N_HEADS, rows, chunk), lambda b, i: (0, rb(b, i), 0)),
        pl.BlockSpec((1, 8, width), lambda b, i: (rb(b, i), 0, 0)),
        pl.BlockSpec((rows, width), lambda b, i: (rb(b, i), z_col_block)),
        pl.BlockSpec((seqs_per_step, N_HEADS, HEAD_DIM, HEAD_DIM), lambda b, i: (b, 0, 0, 0)),
        pl.BlockSpec((1, HEAD_DIM), lambda b, i: (0, 0)),
    ]
    out_specs = [
        pl.BlockSpec((rows, width), lambda b, i: (rb(b, i), 0)),
        pl.BlockSpec((seqs_per_step, N_HEADS, HEAD_DIM, HEAD_DIM), lambda b, i: (b, 0, 0, 0)),
    ]
    return pl.pallas_call(
        functools.partial(_gdn_scan_kernel, chunk=chunk, n_chunks=chunks_per_step,
                          seqs_per_step=seqs_per_step),
        grid=(n_outer, nsteps),
        in_specs=in_specs,
        out_specs=out_specs,
        out_shape=[jax.ShapeDtypeStruct((m, width), BF16),
                   jax.ShapeDtypeStruct((n_seq, N_HEADS, HEAD_DIM, HEAD_DIM), F32)],
        compiler_params=pltpu.CompilerParams(
            dimension_semantics=("parallel", "arbitrary"), vmem_limit_bytes=V7X_VMEM_LIMIT_BYTES),
        name="gdn_scan",
    )(u, w, qg, kdt, qkc, egl, z_src, s0, dnw)


def _pad_lanes(v, offset):
    out = jnp.zeros((1, HEAD_DIM), F32)
    return lax.dynamic_update_slice(out, v.reshape(1, -1).astype(F32), (0, offset))


def kernel(x_prompt, x_sample, cache_k, cache_v, state_ssm, state_conv, page_table, ffn1_pre_w, ffn1_post_w, ffn1_gate, ffn1_up, ffn1_down, mix_pre_w, mix_post_w, w_in, conv_w, a_log, dt_bias, delta_norm_w, lambda_q1, lambda_k1, lambda_q2, lambda_k2, subln_w, w_out, ffn2_pre_w, ffn2_post_w, ffn2_gate, ffn2_up, ffn2_down):
    depth = w_in.shape[0]
    batch, t, d = x_prompt.shape
    n_seq, tn, _ = x_sample.shape
    a_w = N_HEADS * HEAD_DIM
    n_main = 3 * a_w + 3 * a_w + a_w
    yp = x_prompt.reshape(batch * t, d)
    ys = x_sample.reshape(n_seq * tn, d)
    outs = [[] for _ in range(8)]
    for l in range(depth):
        lam_init = 0.8 - 0.6 * math.exp(-0.3 * l)
        lam_rows = jnp.stack([lambda_q1[l], lambda_k1[l], lambda_q2[l], lambda_k2[l]]).astype(F32)
        row = lambda v: v[l].reshape(1, -1).astype(F32)
        wg1, wu1, wd1 = ffn1_gate[l].astype(BF16), ffn1_up[l].astype(BF16), ffn1_down[l].astype(BF16)
        wg2, wu2, wd2 = ffn2_gate[l].astype(BF16), ffn2_up[l].astype(BF16), ffn2_down[l].astype(BF16)
        w_main = w_in[l][:, :n_main].astype(BF16)
        w_gates = jnp.pad(w_in[l][:, n_main:], ((0, 0), (0, HEAD_DIM - 2 * N_HEADS))).astype(BF16)
        wo = w_out[l].astype(BF16)
        alog_row = _pad_lanes(a_log[l], N_HEADS)
        dtb_row = _pad_lanes(dt_bias[l], N_HEADS)
        dnw = row(delta_norm_w)
        subw = row(subln_w)
        cw = conv_w[l].astype(F32)

        yp = _ffn(yp, row(ffn1_pre_w), row(ffn1_post_w), wg1, wu1, wd1, tm=1024, tf=512)
        ys = _ffn(ys, row(ffn1_pre_w), row(ffn1_post_w), wg1, wu1, wd1, tm=n_seq * tn, tf=512)

        pp, gp = _inproj(yp, row(mix_pre_w), w_main, w_gates, tm=1024, tn=1792)
        oa_p = _attn_prompt(lam_rows, subw.reshape(HEAD_DIM, 1), pp, batch=batch, t=t, lam_init=lam_init, tq=256, hpp=2)
        intra_p = _gdn_intra_prompt(pp, gp, cw, alog_row, dtb_row, batch=batch, t=t)
        zero_s = jnp.zeros((batch, N_HEADS, HEAD_DIM, HEAD_DIM), F32)
        ob_p, s_p = _gdn_scan(intra_p, pp, 6, zero_s, dnw, n_seq=batch, t=t, chunk=CHUNK,
                              chunks_per_step=GDN_ROWS // CHUNK)
        yp = _outproj(yp, oa_p, ob_p, wo, row(mix_post_w), tm=512)

        ps, gs = _inproj(ys, row(mix_pre_w), w_main, w_gates, tm=n_seq * tn, tn=1792)
        qs = ps[:, :a_w].reshape(n_seq, tn * N_HEADS, HEAD_DIM)
        ks = ps[:, a_w:2 * a_w].reshape(n_seq, tn * N_HEADS, HEAD_DIM)
        vs = ps[:, 2 * a_w:3 * a_w].reshape(n_seq, tn * N_HEADS, HEAD_DIM)
        oa_s = _attn_sample(page_table, lam_rows, subw, qs, ks, vs, cache_k[l], cache_v[l],
                            lam_init=lam_init, group=8)
        oa_s = oa_s.reshape(n_seq * tn, a_w)
        intra_s = _gdn_intra_sample(ps, gs, cw, alog_row, dtb_row, state_conv[l], n_seq=n_seq, tn=tn)
        ob_s, s_s = _gdn_scan(intra_s, ps, 6, state_ssm[l].astype(F32), dnw, n_seq=n_seq, t=tn,
                              chunk=tn, chunks_per_step=n_seq, seqs_per_step=n_seq)
        ys = _outproj(ys, oa_s, ob_s, wo, row(mix_post_w), tm=n_seq * tn)

        yp = _ffn(yp, row(ffn2_pre_w), row(ffn2_post_w), wg2, wu2, wd2, tm=1024, tf=512)
        ys = _ffn(ys, row(ffn2_pre_w), row(ffn2_post_w), wg2, wu2, wd2, tm=n_seq * tn, tf=512)

        ppb = pp.reshape(batch, t, -1)
        psb = ps.reshape(n_seq, tn, -1)
        outs[0].append(ppb[:, :, a_w:2 * a_w].reshape(batch, t, N_HEADS, HEAD_DIM))
        outs[1].append(ppb[:, :, 2 * a_w:3 * a_w].reshape(batch, t, N_HEADS, HEAD_DIM))
        outs[2].append(s_p.astype(state_ssm.dtype))
        outs[3].append(ppb[:, t - (CONV_W - 1):, 3 * a_w:6 * a_w])
        outs[4].append(psb[:, :, a_w:2 * a_w].reshape(n_seq, tn, N_HEADS, HEAD_DIM))
        outs[5].append(psb[:, :, 2 * a_w:3 * a_w].reshape(n_seq, tn, N_HEADS, HEAD_DIM))
        outs[6].append(s_s.astype(state_ssm.dtype))
        outs[7].append(jnp.concatenate([state_conv[l].astype(psb.dtype), psb[:, :, 3 * a_w:6 * a_w]],
                                       axis=1)[:, tn:])
    return (yp.reshape(batch, t, d), ys.reshape(n_seq, tn, d)) + tuple(jnp.stack(o) for o in outs)
```

```python
import functools
import math

import jax
import jax.numpy as jnp
from jax import lax
from jax.experimental import pallas as pl
from jax.experimental.pallas import tpu as pltpu

F32 = jnp.float32
BF16 = jnp.bfloat16
EPS = 1e-6
NEG_INF = float("-inf")
LOG2E = 1.4426950408889634

V7X_VMEM_LIMIT_BYTES = 60 * 1024 * 1024

QK_DIM = 64
HEAD_DIM = 128
N_HEADS = 8
CONV_W = 4
CHUNK = 64
GDN_ROWS = 256

def _dot(a, b):
    return jnp.dot(a, b, preferred_element_type=F32)


def _dot_nt(a, b):
    return lax.dot_general(a, b, (((1,), (1,)), ((), ())), preferred_element_type=F32)


def _split3(x):
    hi = x.astype(BF16)
    r = x - hi.astype(F32)
    mid = r.astype(BF16)
    lo = (r - mid.astype(F32)).astype(BF16)
    return hi, mid, lo


def _dot_sel_l(sel, x):
    hi, mid, lo = _split3(x)
    return _dot(sel, hi) + _dot(sel, mid) + _dot(sel, lo)


def _dot_sel_r(x, sel):
    hi, mid, lo = _split3(x)
    return _dot(hi, sel) + _dot(mid, sel) + _dot(lo, sel)


def _dot_hp(a, b):
    ah = a.astype(BF16)
    al = (a - ah.astype(F32)).astype(BF16)
    bh = b.astype(BF16)
    bl = (b - bh.astype(F32)).astype(BF16)
    return _dot(ah, bh) + _dot(ah, bl) + _dot(al, bh)


def _rms(x, w):
    return x * lax.rsqrt(jnp.mean(x * x, axis=-1, keepdims=True) + EPS) * w


def _silu(x):
    return x * jax.nn.sigmoid(x)


def _softplus(x):
    return jnp.maximum(x, 0.0) + jnp.log1p(jnp.exp(-jnp.abs(x)))


def _lambda_full(lam_ref, lam_init):
    l = lam_ref[...]
    a = jnp.sum(l[0:1] * l[1:2], axis=-1, keepdims=True)
    b = jnp.sum(l[2:3] * l[3:4], axis=-1, keepdims=True)
    return jnp.exp(a) - jnp.exp(b) + lam_init


def _idiv(x, n):
    assert n & (n - 1) == 0
    return x >> (n.bit_length() - 1)


def _imod(x, n):
    assert n & (n - 1) == 0
    return x & (n - 1)


def _pow2_neg(k):
    return lax.bitcast_convert_type((127 - k) << 23, F32)


def _ffn_kernel(x_ref, prew_ref, postw_ref, wg_ref, wu_ref, wd_ref, o_ref, *rest, nf, emit_bf16):
    h_ref = rest[-1]
    f = pl.program_id(1)

    @pl.when(f == 0)
    def _():
        h_ref[...] = _rms(x_ref[...], prew_ref[...]).astype(BF16)
        o_ref[...] = jnp.zeros_like(o_ref)

    wg, wu, wd = wg_ref[...], wu_ref[...], wd_ref[...]
    if emit_bf16:
        wg, wu, wd = wg.astype(BF16), wu.astype(BF16), wd.astype(BF16)
        rest[0][...], rest[1][...], rest[2][...] = wg, wu, wd
    h = h_ref[...]
    g = _dot(h, wg)
    u = _dot(h, wu)
    o_ref[...] += _dot((_silu(g) * u).astype(BF16), wd)

    @pl.when(f == nf - 1)
    def _():
        o_ref[...] = x_ref[...] + 0.5 * _rms(o_ref[...], postw_ref[...])


def _ffn(x, pre_w, post_w, wg, wu, wd, *, tm, tf, emit_bf16=False):
    m, d = x.shape
    ff = wg.shape[1]
    nf = ff // tf
    assert not emit_bf16 or m == tm
    gu_spec = pl.BlockSpec((d, tf), lambda i, f: (0, f))
    dn_spec = pl.BlockSpec((tf, d), lambda i, f: (f, 0))
    out_specs = [pl.BlockSpec((tm, d), lambda i, f: (i, 0))]
    out_shape = [jax.ShapeDtypeStruct((m, d), F32)]
    if emit_bf16:
        out_specs += [gu_spec, gu_spec, dn_spec]
        out_shape += [jax.ShapeDtypeStruct(w.shape, BF16) for w in (wg, wu, wd)]
    outs = pl.pallas_call(
        functools.partial(_ffn_kernel, nf=nf, emit_bf16=emit_bf16),
        grid=(m // tm, nf),
        in_specs=[
            pl.BlockSpec((tm, d), lambda i, f: (i, 0), pipeline_mode=pl.Buffered(1)),
            pl.BlockSpec((1, d), lambda i, f: (0, 0)),
            pl.BlockSpec((1, d), lambda i, f: (0, 0)),
            gu_spec, gu_spec, dn_spec,
        ],
        out_specs=out_specs,
        out_shape=out_shape,
        scratch_shapes=[pltpu.VMEM((tm, d), BF16)],
        compiler_params=pltpu.CompilerParams(
            dimension_semantics=("parallel", "arbitrary"), vmem_limit_bytes=V7X_VMEM_LIMIT_BYTES),
        name="ffn",
    )(x, pre_w, post_w, wg, wu, wd)
    return outs if emit_bf16 else outs[0]


def _inproj_kernel(x_ref, prew_ref, w_ref, wg_ref, o_ref, og_ref, ko_ref, vo_ref, *rest, emit_bf16):
    h_ref = rest[-1]
    j = pl.program_id(1)

    @pl.when(j == 0)
    def _():
        h = _rms(x_ref[...], prew_ref[...]).astype(BF16)
        h_ref[...] = h
        og_ref[...] = _dot(h, wg_ref[...])

    w = w_ref[...]
    if emit_bf16:
        w = w.astype(BF16)
        rest[0][...] = w
    y = _dot(h_ref[...], w)
    o_ref[...] = y

    @pl.when(j == 1)
    def _():
        ko_ref[...] = y

    @pl.when(j == 2)
    def _():
        vo_ref[...] = y


def _inproj(x, pre_w, w_main, w_gates, *, tm, tn, n, emit_bf16=False):
    m, d = x.shape
    ng = w_gates.shape[1]
    assert not emit_bf16 or m == tm
    w_spec = pl.BlockSpec((d, tn), lambda i, j: (0, j))
    row_spec = pl.BlockSpec((tm, tn), lambda i, j: (i, 0))
    out_specs = [pl.BlockSpec((tm, tn), lambda i, j: (i, j)), pl.BlockSpec((tm, ng), lambda i, j: (i, 0)),
                 row_spec, row_spec]
    out_shape = [jax.ShapeDtypeStruct((m, n), F32), jax.ShapeDtypeStruct((m, ng), F32),
                 jax.ShapeDtypeStruct((m, tn), F32), jax.ShapeDtypeStruct((m, tn), F32)]
    if emit_bf16:
        out_specs.append(w_spec)
        out_shape.append(jax.ShapeDtypeStruct((d, n), BF16))
    return pl.pallas_call(
        functools.partial(_inproj_kernel, emit_bf16=emit_bf16),
        grid=(m // tm, n // tn),
        in_specs=[
            pl.BlockSpec((tm, d), lambda i, j: (i, 0), pipeline_mode=pl.Buffered(1)),
            pl.BlockSpec((1, d), lambda i, j: (0, 0)),
            w_spec,
            pl.BlockSpec((d, ng), lambda i, j: (0, 0)),
        ],
        out_specs=out_specs,
        out_shape=out_shape,
        scratch_shapes=[pltpu.VMEM((tm, d), BF16)],
        compiler_params=pltpu.CompilerParams(
            dimension_semantics=("parallel", "arbitrary"), vmem_limit_bytes=V7X_VMEM_LIMIT_BYTES),
        name="inproj",
    )(x, pre_w, w_main, w_gates)


def _outproj_kernel(x_ref, oa_ref, ob_ref, wa_ref, wb_ref, postw_ref, o_ref, *rest, emit_bf16):
    wa, wb = wa_ref[...], wb_ref[...]
    if emit_bf16:
        wa, wb = wa.astype(BF16), wb.astype(BF16)
        ka = wa.shape[0]
        rest[0][:ka, :] = wa
        rest[0][ka:, :] = wb
    y = _dot(oa_ref[...], wa) + _dot(ob_ref[...], wb)
    o_ref[...] = x_ref[...] + _rms(y, postw_ref[...])


def _outproj(x, oa, ob, w_out, post_w, *, tm, emit_bf16=False):
    m, d = x.shape
    ka = oa.shape[1]
    assert not emit_bf16 or m == tm
    out_specs = [pl.BlockSpec((tm, d), lambda i: (i, 0))]
    out_shape = [jax.ShapeDtypeStruct((m, d), F32)]
    if emit_bf16:
        out_specs.append(pl.BlockSpec(w_out.shape, lambda i: (0, 0)))
        out_shape.append(jax.ShapeDtypeStruct(w_out.shape, BF16))
    outs = pl.pallas_call(
        functools.partial(_outproj_kernel, emit_bf16=emit_bf16),
        grid=(m // tm,),
        in_specs=[
            pl.BlockSpec((tm, d), lambda i: (i, 0)),
            pl.BlockSpec((tm, ka), lambda i: (i, 0)),
            pl.BlockSpec((tm, ka), lambda i: (i, 0)),
            pl.BlockSpec((ka, d), lambda i: (0, 0)),
            pl.BlockSpec((ka, d), lambda i: (1, 0)),
            pl.BlockSpec((1, d), lambda i: (0, 0)),
        ],
        out_specs=out_specs,
        out_shape=out_shape,
        compiler_params=pltpu.CompilerParams(
            dimension_semantics=("parallel",), vmem_limit_bytes=V7X_VMEM_LIMIT_BYTES),
        name="outproj",
    )(x, oa, ob, w_out, w_out, post_w)
    return outs if emit_bf16 else outs[0]


def _attn_prompt_kernel(lam_ref, subw_ref, q_ref, k_ref, v_ref, o_ref,
                        kb_ref, vt_ref, bias_ref, *, t, tq, hpp, lam_init):
    hg = pl.program_id(1)
    nq = t // tq
    lam = _lambda_full(lam_ref, lam_init)
    lane = lax.broadcasted_iota(jnp.int32, (1, HEAD_DIM), 1)
    first_map = lane < QK_DIM
    kk = lax.broadcasted_iota(jnp.int32, (tq, tq), 0)
    qq = lax.broadcasted_iota(jnp.int32, (tq, tq), 1)
    slopes = [_pow2_neg(jnp.full((1, 1), hg * hpp + hh + 1, jnp.int32)) * LOG2E for hh in range(hpp)]
    er = lax.broadcasted_iota(jnp.int32, (HEAD_DIM, HEAD_DIM), 0)
    eye_b = (er == lax.broadcasted_iota(jnp.int32, (HEAD_DIM, HEAD_DIM), 1)).astype(BF16)
    for hh in range(hpp):
        cols = slice(hh * HEAD_DIM, (hh + 1) * HEAD_DIM)
        b = slopes[hh] * (kk - qq).astype(F32)
        bias_ref[hh, 0] = b
        bias_ref[hh, 1] = jnp.where(qq >= kk, b, NEG_INF)
        vt_ref[hh] = _dot_nt(eye_b, v_ref[:, cols].astype(BF16)).astype(BF16)
    kb_ref[...] = k_ref[...].astype(BF16)
    scale = QK_DIM ** -0.5 * LOG2E
    subw_col = subw_ref[...]

    def q_block(i, _):
        rows_i = pl.ds(pl.multiple_of(i * tq, tq), tq)
        qs = []
        for hh in range(hpp):
            q = q_ref[rows_i, hh * HEAD_DIM:(hh + 1) * HEAD_DIM] * scale
            qs.append(jnp.concatenate(
                [jnp.where(first_map, q, 0.0), jnp.where(first_map, 0.0, q)], axis=0).astype(BF16))

        def scores(j):
            rows_j = pl.ds(pl.multiple_of(j * tq, tq), tq)
            diag = jnp.asarray(j == i, jnp.int32)
            out = []
            for hh in range(hpp):
                b = bias_ref[hh, diag]
                out.append(_dot_nt(kb_ref[rows_j, hh * HEAD_DIM:(hh + 1) * HEAD_DIM], qs[hh])
                           + jnp.concatenate([b, b], axis=1))
            return tuple(out)

        def consume(j, sts, state):
            rows_j = pl.ds(pl.multiple_of(j * tq, tq), tq)
            blocks_away = jnp.asarray((i - j) * tq, F32)
            out = []
            for hh in range(hpp):
                m_old, l_old, acc_old = state[hh]
                far = slopes[hh] * blocks_away
                m_new = jnp.maximum(m_old, jnp.max(sts[hh], axis=0, keepdims=True) - far)
                p = jnp.exp2(sts[hh] - (far + m_new))
                alpha = jnp.exp2(m_old - m_new)
                out.append((m_new, alpha * l_old + jnp.sum(p, axis=0, keepdims=True),
                            alpha * acc_old + _dot(vt_ref[hh, :, rows_j], p.astype(BF16))))
            return tuple(out)

        def step(j, carry):
            sts, state = carry
            nxt = scores(j + 1)
            return nxt, consume(j, sts, state)

        init = tuple((jnp.full((1, 2 * tq), NEG_INF, F32), jnp.zeros((1, 2 * tq), F32),
                      jnp.zeros((HEAD_DIM, 2 * tq), F32)) for _ in range(hpp))
        sts, state = lax.fori_loop(0, i, step, (scores(0), init))
        state = consume(i, sts, state)
        for hh in range(hpp):
            _, l_fin, acc_fin = state[hh]
            ot = acc_fin / l_fin
            dt = ot[:, :tq] - lam * ot[:, tq:]
            dt = dt * lax.rsqrt(jnp.mean(dt * dt, axis=0, keepdims=True) + EPS) * subw_col
            o_ref[rows_i, hh * HEAD_DIM:(hh + 1) * HEAD_DIM] = (
                (dt * (1.0 - lam_init)).T.astype(o_ref.dtype))
        return 0

    lax.fori_loop(0, nq, q_block, 0)


def _attn_prompt(lam_rows, subln_col, proj, k, v, *, batch, t, lam_init, tq, hpp):
    m = proj.shape[0]
    ng = N_HEADS // hpp
    w = hpp * HEAD_DIM
    return pl.pallas_call(
        functools.partial(_attn_prompt_kernel, t=t, tq=tq, hpp=hpp, lam_init=lam_init),
        grid=(batch, ng),
        in_specs=[
            pl.BlockSpec((4, QK_DIM), lambda b, g: (0, 0)),
            pl.BlockSpec((HEAD_DIM, 1), lambda b, g: (0, 0)),
            pl.BlockSpec((t, w), lambda b, g: (b, g)),
            pl.BlockSpec((t, w), lambda b, g: (b, g)),
            pl.BlockSpec((t, w), lambda b, g: (b, g)),
        ],
        out_specs=pl.BlockSpec((t, w), lambda b, g: (b, g)),
        out_shape=jax.ShapeDtypeStruct((m, N_HEADS * HEAD_DIM), BF16),
        scratch_shapes=[
            pltpu.VMEM((t, w), BF16),
            pltpu.VMEM((hpp, HEAD_DIM, t), BF16),
            pltpu.VMEM((hpp, 2, tq, tq), F32),
        ],
        compiler_params=pltpu.CompilerParams(
            dimension_semantics=("parallel", "parallel"), vmem_limit_bytes=V7X_VMEM_LIMIT_BYTES),
        name="attn_prompt",
    )(lam_rows, subln_col, proj, k, v)


def _attn_sample_kernel(pt_ref, lam_ref, subw_ref, q_ref, kn_ref, vn_ref, *rest,
                        n_pages, page, tn, group, lam_init):
    del pt_ref
    kc_refs = rest[:group]
    vc_refs = rest[group:2 * group]
    o_ref, qb_ref, bias_ref, m_ref, l_ref, acc_ref = rest[2 * group:]
    p = pl.program_id(1)
    nrow = 2 * tn * N_HEADS
    ncol = page * N_HEADS
    past = n_pages * page
    row = lax.broadcasted_iota(jnp.int32, (nrow, 1), 0)
    slope = _pow2_neg(_imod(row, N_HEADS) + 1)

    @pl.when(p == 0)
    def _():
        q = q_ref[0] * (QK_DIM ** -0.5)
        lane = lax.broadcasted_iota(jnp.int32, (1, HEAD_DIM), 1)
        first_map = lane < QK_DIM
        qb_ref[...] = jnp.concatenate(
            [jnp.where(first_map, q, 0.0), jnp.where(first_map, 0.0, q)], axis=0).astype(BF16)
        rr = lax.broadcasted_iota(jnp.int32, (nrow, ncol), 0)
        cc = lax.broadcasted_iota(jnp.int32, (nrow, ncol), 1)
        tq_ = _idiv(_imod(rr, tn * N_HEADS), N_HEADS)
        same_head = _imod(rr, N_HEADS) == _imod(cc, N_HEADS)
        bias_ref[...] = jnp.where(same_head, slope * (_idiv(cc, N_HEADS) - tq_).astype(F32), NEG_INF)
        m_ref[...] = jnp.full(m_ref.shape, NEG_INF, F32)
        l_ref[...] = jnp.zeros(l_ref.shape, F32)
        acc_ref[...] = jnp.zeros(acc_ref.shape, F32)

    def lane_fold(x, op):
        out = x[:, :HEAD_DIM]
        for i in range(1, x.shape[1] // HEAD_DIM):
            out = op(out, x[:, i * HEAD_DIM:(i + 1) * HEAD_DIM])
        return out

    def update(scores, offs, vals):
        m_old = m_ref[...]
        m_loc = lane_fold(scores[0], jnp.maximum) - offs[0]
        for s, off in zip(scores[1:], offs[1:]):
            m_loc = jnp.maximum(m_loc, lane_fold(s, jnp.maximum) - off)
        m_new = jnp.maximum(m_old, jnp.max(m_loc, axis=-1, keepdims=True))
        alpha = jnp.exp(m_old - m_new)
        pes = [jnp.exp(s - (off + m_new)) for s, off in zip(scores, offs)]
        l_part = lane_fold(pes[0], jnp.add)
        for pe in pes[1:]:
            l_part = l_part + lane_fold(pe, jnp.add)
        pv = _dot(jnp.concatenate([pe.astype(BF16) for pe in pes], axis=1),
                  jnp.concatenate(vals, axis=0))
        l_ref[...] = alpha * l_ref[...] + l_part
        acc_ref[...] = alpha * acc_ref[...] + pv
        m_ref[...] = m_new

    qb = qb_ref[...]
    bias = bias_ref[...]
    scores, offs, vals = [], [], []
    for g in range(group):
        kf = kc_refs[g][0].reshape(ncol, HEAD_DIM).astype(BF16)
        vals.append(vc_refs[g][0].reshape(ncol, HEAD_DIM).astype(BF16))
        scores.append(_dot_nt(qb, kf) + bias)
        offs.append(slope * jnp.asarray(past - (p * group + g) * page, F32))
    update(scores, offs, vals)

    @pl.when(p == n_pages // group - 1)
    def _():
        ncn = tn * N_HEADS
        rr = lax.broadcasted_iota(jnp.int32, (nrow, HEAD_DIM), 0)
        cc = lax.broadcasted_iota(jnp.int32, (nrow, HEAD_DIM), 1)
        tq_ = _idiv(_imod(rr, ncn), N_HEADS)
        tk_ = _idiv(cc, N_HEADS)
        ok = (cc < ncn) & (_imod(rr, N_HEADS) == _imod(cc, N_HEADS)) & (tk_ <= tq_)
        bias_new = jnp.where(ok, slope * (tk_ - tq_).astype(F32), NEG_INF)
        pad = jnp.zeros((HEAD_DIM - ncn, HEAD_DIM), BF16)
        kn = jnp.concatenate([kn_ref[0].astype(BF16), pad], axis=0)
        vn = jnp.concatenate([vn_ref[0].astype(BF16), pad], axis=0)
        update([_dot_nt(qb, kn) + bias_new], [jnp.zeros((nrow, 1), F32)], [vn])
        lam = _lambda_full(lam_ref, lam_init)
        o = acc_ref[...] / jnp.sum(l_ref[...], axis=-1, keepdims=True)
        d = o[:ncn] - lam * o[ncn:]
        o_ref[0] = (_rms(d, subw_ref[...]) * (1.0 - lam_init)).astype(o_ref.dtype)


def _attn_sample(page_table, lam_rows, subln_w, q, k_new, v_new, cache_k, cache_v, *, lam_init, group):
    n_seq, n_pages = page_table.shape
    page = cache_k.shape[1]
    tn = q.shape[1] // N_HEADS
    nrow = 2 * tn * N_HEADS
    ncol = page * N_HEADS
    assert n_pages % group == 0

    def page_spec(g):
        return pl.BlockSpec((1, page, N_HEADS, HEAD_DIM),
                            lambda b, p, pt: (pt[b, p * group + g], 0, 0, 0))

    new_spec = pl.BlockSpec((1, tn * N_HEADS, HEAD_DIM), lambda b, p, pt: (b, 0, 0))
    grid_spec = pltpu.PrefetchScalarGridSpec(
        num_scalar_prefetch=1,
        grid=(n_seq, n_pages // group),
        in_specs=[
            pl.BlockSpec((4, QK_DIM), lambda b, p, pt: (0, 0)),
            pl.BlockSpec((1, HEAD_DIM), lambda b, p, pt: (0, 0)),
            new_spec, new_spec, new_spec,
        ] + [page_spec(g) for g in range(group)] * 2,
        out_specs=new_spec,
        scratch_shapes=[
            pltpu.VMEM((nrow, HEAD_DIM), BF16),
            pltpu.VMEM((nrow, ncol), F32),
            pltpu.VMEM((nrow, 1), F32),
            pltpu.VMEM((nrow, HEAD_DIM), F32),
            pltpu.VMEM((nrow, HEAD_DIM), F32),
        ],
    )
    return pl.pallas_call(
        functools.partial(_attn_sample_kernel, n_pages=n_pages, page=page, tn=tn, group=group,
                          lam_init=lam_init),
        grid_spec=grid_spec,
        out_shape=jax.ShapeDtypeStruct((n_seq, tn * N_HEADS, HEAD_DIM), BF16),
        compiler_params=pltpu.CompilerParams(
            dimension_semantics=("parallel", "arbitrary"), vmem_limit_bytes=V7X_VMEM_LIMIT_BYTES),
        name="attn_sample",
    )(page_table, lam_rows, subln_w, q, k_new, v_new, *([cache_k] * group), *([cache_v] * group))


def _gdn_n_masks(chunk):
    return 3 + (chunk.bit_length() - 1)


def _gdn_fill_masks(mask_ref, *, rows, chunk):
    ri = lax.broadcasted_iota(jnp.int32, (rows, rows), 0)
    ci = lax.broadcasted_iota(jnp.int32, (rows, rows), 1)
    same = _idiv(ri, chunk) == _idiv(ci, chunk)
    mask_ref[0] = (same & (ri >= ci)).astype(F32)
    mask_ref[1] = (same & (ri > ci)).astype(F32)
    mask_ref[2] = (ri == ci).astype(F32)
    s, k = 1, 3
    while s < chunk:
        mask_ref[k] = ((_idiv(ri, 2 * s) == _idiv(ci, 2 * s)) & (_imod(_idiv(ri, s), 2) == 1)
                       & (_imod(_idiv(ci, s), 2) == 0)).astype(F32)
        s, k = 2 * s, k + 1


def _gdn_intra_all(act, gates, alog_row, dtb_row, mask_ref, u_ref, w_ref, qg_ref, kdt_ref, qkc_ref,
                   egl_ref, *, rows, chunk):
    n_chunks = rows // chunk
    n_levels = _gdn_n_masks(chunk) - 3
    beta_all = jax.nn.sigmoid(gates)
    g_all = -jnp.exp(alog_row) * _softplus(gates + dtb_row)
    gcum_all = _dot_sel_l(mask_ref[0].astype(BF16), g_all)
    gtot_all = jnp.concatenate(
        [jnp.broadcast_to(gcum_all[(c + 1) * chunk - 1:(c + 1) * chunk], (chunk, HEAD_DIM))
         for c in range(n_chunks)], axis=0)
    gcum_t = gcum_all.T
    fr = lax.broadcasted_iota(jnp.int32, (rows, chunk), 0)
    fc = lax.broadcasted_iota(jnp.int32, (rows, chunk), 1)
    fold = (_imod(fr, chunk) == fc).astype(BF16)
    er = lax.broadcasted_iota(jnp.int32, (HEAD_DIM, HEAD_DIM), 0)
    eye_b = (er == lax.broadcasted_iota(jnp.int32, (HEAD_DIM, HEAD_DIM), 1)).astype(BF16)

    def lanes(col):
        return jnp.broadcast_to(col, (rows, HEAD_DIM))

    def square(x):
        if rows >= HEAD_DIM:
            return jnp.concatenate([x] * (rows // HEAD_DIM), axis=1)
        return x[:, :rows]

    heads = range(N_HEADS)
    egl_rows, lmats, lbs, tinvs, rhss = [], [], [], [], []
    for h in heads:
        cols = slice(h * HEAD_DIM, (h + 1) * HEAD_DIM)
        aq = act[:, h * HEAD_DIM:(h + 1) * HEAD_DIM]
        ak = act[:, (N_HEADS + h) * HEAD_DIM:(N_HEADS + h + 1) * HEAD_DIM]
        av = act[:, (2 * N_HEADS + h) * HEAD_DIM:(2 * N_HEADS + h + 1) * HEAD_DIM]
        qn = aq * lax.rsqrt(jnp.sum(aq * aq, axis=-1, keepdims=True) + 1e-6) * (HEAD_DIM ** -0.5)
        kn = ak * lax.rsqrt(jnp.sum(ak * ak, axis=-1, keepdims=True) + 1e-6)
        beta = lanes(beta_all[:, h:h + 1])
        g_cum = lanes(gcum_all[:, N_HEADS + h:N_HEADS + h + 1])
        g_tot = lanes(gtot_all[:, N_HEADS + h:N_HEADS + h + 1])
        g_row = gcum_t[N_HEADS + h:N_HEADS + h + 1, :]
        decay = jnp.exp(jnp.minimum(square(g_cum) - g_row, 0.0)) * mask_ref[0]

        kbeta = kn * beta
        knb = kn.astype(BF16)
        lmat = mask_ref[1] * (_dot_nt(kbeta.astype(BF16), knb) * decay)
        lmats.append(lmat)
        lbs.append(lmat.astype(BF16))
        tinvs.append(mask_ref[2] - lmat * mask_ref[3])
        e_cum = jnp.exp(g_cum)
        rhss.append(jnp.concatenate([av * beta, kbeta * e_cum], axis=1).astype(BF16))
        qk = (_dot_nt(qn.astype(BF16), knb) * decay).astype(BF16)
        qkc_ref[h] = _dot(qk, fold).astype(BF16)
        qg_ref[:, cols] = (qn * e_cum).astype(BF16)
        kd = (kn * jnp.exp(g_tot - g_cum)).astype(BF16)
        kdt_ref[h] = _dot_nt(eye_b, kd).astype(BF16)
        egl = jnp.exp(g_tot)
        egl_rows.append(jnp.concatenate(
            [egl[c * chunk:c * chunk + 1] for c in range(n_chunks)]
            + [jnp.zeros((8 - n_chunks, HEAD_DIM), F32)] * (n_chunks < 8), axis=0))
    egl_ref[0] = jnp.concatenate(egl_rows, axis=1)

    for k in range(1, n_levels):
        tbs = [tinvs[h].astype(BF16) for h in heads]
        mids = [_dot(tbs[h], lbs[h]).astype(BF16) for h in heads]
        tinvs = [tinvs[h] - mask_ref[3 + k] * _dot(mids[h], tbs[h]) for h in heads]
    resids = [mask_ref[2] - tinvs[h] - _dot_hp(lmats[h], tinvs[h]) for h in heads]
    tinvs = [tinvs[h] + _dot(tinvs[h].astype(BF16), resids[h].astype(BF16)) for h in heads]
    for h in heads:
        cols = slice(h * HEAD_DIM, (h + 1) * HEAD_DIM)
        uw = _dot(tinvs[h].astype(BF16), rhss[h])
        u_ref[:, cols] = uw[:, :HEAD_DIM]
        w_ref[:, cols] = uw[:, HEAD_DIM:].astype(BF16)


def _gdn_intra_prompt_kernel(x_ref, gates_ref, cw_ref, alog_ref, dtb_ref,
                             u_ref, w_ref, qg_ref, kdt_ref, qkc_ref, egl_ref, pad_ref, mask_ref,
                             *, rows, chunk):
    t = pl.program_id(1)

    @pl.when(t == 0)
    def _():
        pad_ref[0:8, :] = jnp.zeros((8, pad_ref.shape[1]), F32)
        _gdn_fill_masks(mask_ref, rows=rows, chunk=chunk)

    pad_ref[8:, :] = x_ref[...]
    cw = cw_ref[...]
    acc = pad_ref[pl.ds(8 - (CONV_W - 1), rows), :] * cw[0:1]
    for j in range(1, CONV_W):
        acc = acc + pad_ref[pl.ds(8 - (CONV_W - 1) + j, rows), :] * cw[j:j + 1]
    pad_ref[0:8, :] = x_ref[rows - 8:, :]
    _gdn_intra_all(_silu(acc), gates_ref[...], alog_ref[...], dtb_ref[...], mask_ref,
                   u_ref, w_ref, qg_ref, kdt_ref, qkc_ref, egl_ref, rows=rows, chunk=chunk)


def _gdn_intra_sample_kernel(x_ref, gates_ref, cw_ref, alog_ref, dtb_ref, buf_ref,
                             u_ref, w_ref, qg_ref, kdt_ref, qkc_ref, egl_ref, pad_ref, mask_ref,
                             *, n_seq, tn):
    seg = 8 + tn
    _gdn_fill_masks(mask_ref, rows=n_seq * tn, chunk=tn)
    cw = cw_ref[...]
    for s in range(n_seq):
        pad_ref[s * seg + 8 - (CONV_W - 1):s * seg + 8, :] = buf_ref[s]
        pad_ref[s * seg + 8:(s + 1) * seg, :] = x_ref[s * tn:(s + 1) * tn, :]
    pieces = []
    for s in range(n_seq):
        base = s * seg + 8 - (CONV_W - 1)
        acc = pad_ref[pl.ds(base, tn), :] * cw[0:1]
        for j in range(1, CONV_W):
            acc = acc + pad_ref[pl.ds(base + j, tn), :] * cw[j:j + 1]
        pieces.append(acc)
    _gdn_intra_all(_silu(jnp.concatenate(pieces, axis=0)), gates_ref[...], alog_ref[...], dtb_ref[...],
                   mask_ref, u_ref, w_ref, qg_ref, kdt_ref, qkc_ref, egl_ref, rows=n_seq * tn, chunk=tn)


def _intra_out(m, chunk, n_steps):
    width = N_HEADS * HEAD_DIM
    return [
        jax.ShapeDtypeStruct((m, width), F32),
        jax.ShapeDtypeStruct((m, width), BF16),
        jax.ShapeDtypeStruct((m, width), BF16),
        jax.ShapeDtypeStruct((N_HEADS, HEAD_DIM, m), BF16),
        jax.ShapeDtypeStruct((N_HEADS, m, chunk), BF16),
        jax.ShapeDtypeStruct((n_steps, 8, width), F32),
    ]


def _gdn_intra_prompt(proj, gates, conv_w, alog_row, dtb_row, *, batch, t):
    m = proj.shape[0]
    rows = GDN_ROWS
    nblk = t // rows
    width = N_HEADS * HEAD_DIM
    cwid = 3 * width

    def rb(b, i):
        return b * nblk + i

    in_specs = [
        pl.BlockSpec((rows, cwid), lambda b, i: (rb(b, i), 1)),
        pl.BlockSpec((rows, HEAD_DIM), lambda b, i: (rb(b, i), 0)),
        pl.BlockSpec((CONV_W, cwid), lambda b, i: (0, 0)),
        pl.BlockSpec((1, HEAD_DIM), lambda b, i: (0, 0)),
        pl.BlockSpec((1, HEAD_DIM), lambda b, i: (0, 0)),
    ]
    out_specs = [
        pl.BlockSpec((rows, width), lambda b, i: (rb(b, i), 0)),
        pl.BlockSpec((rows, width), lambda b, i: (rb(b, i), 0)),
        pl.BlockSpec((rows, width), lambda b, i: (rb(b, i), 0)),
        pl.BlockSpec((N_HEADS, HEAD_DIM, rows), lambda b, i: (0, 0, rb(b, i))),
        pl.BlockSpec((N_HEADS, rows, CHUNK), lambda b, i: (0, rb(b, i), 0)),
        pl.BlockSpec((1, 8, width), lambda b, i: (rb(b, i), 0, 0)),
    ]
    return pl.pallas_call(
        functools.partial(_gdn_intra_prompt_kernel, rows=rows, chunk=CHUNK),
        grid=(batch, nblk),
        in_specs=in_specs,
        out_specs=out_specs,
        out_shape=_intra_out(m, CHUNK, batch * nblk),
        scratch_shapes=[pltpu.VMEM((rows + 8, cwid), F32),
                        pltpu.VMEM((_gdn_n_masks(CHUNK), rows, rows), F32)],
        compiler_params=pltpu.CompilerParams(
            dimension_semantics=("parallel", "arbitrary"), vmem_limit_bytes=V7X_VMEM_LIMIT_BYTES),
        name="gdn_intra_prompt",
    )(proj, gates, conv_w, alog_row, dtb_row)


def _gdn_intra_sample(proj, gates, conv_w, alog_row, dtb_row, conv_state, *, n_seq, tn):
    m = proj.shape[0]
    width = N_HEADS * HEAD_DIM
    cwid = 3 * width
    in_specs = [
        pl.BlockSpec((m, cwid), lambda i: (0, 1)),
        pl.BlockSpec((m, HEAD_DIM), lambda i: (0, 0)),
        pl.BlockSpec((CONV_W, cwid), lambda i: (0, 0)),
        pl.BlockSpec((1, HEAD_DIM), lambda i: (0, 0)),
        pl.BlockSpec((1, HEAD_DIM), lambda i: (0, 0)),
        pl.BlockSpec((n_seq, CONV_W - 1, cwid), lambda i: (0, 0, 0)),
    ]
    out_specs = [
        pl.BlockSpec((m, width), lambda i: (0, 0)),
        pl.BlockSpec((m, width), lambda i: (0, 0)),
        pl.BlockSpec((m, width), lambda i: (0, 0)),
        pl.BlockSpec((N_HEADS, HEAD_DIM, m), lambda i: (0, 0, 0)),
        pl.BlockSpec((N_HEADS, m, tn), lambda i: (0, 0, 0)),
        pl.BlockSpec((1, 8, width), lambda i: (0, 0, 0)),
    ]
    return pl.pallas_call(
        functools.partial(_gdn_intra_sample_kernel, n_seq=n_seq, tn=tn),
        grid=(1,),
        in_specs=in_specs,
        out_specs=out_specs,
        out_shape=_intra_out(m, tn, 1),
        scratch_shapes=[pltpu.VMEM((n_seq * (8 + tn), cwid), F32),
                        pltpu.VMEM((_gdn_n_masks(tn), m, m), F32)],
        compiler_params=pltpu.CompilerParams(
            dimension_semantics=("arbitrary",), vmem_limit_bytes=V7X_VMEM_LIMIT_BYTES),
        name="gdn_intra_sample",
    )(proj, gates, conv_w, alog_row, dtb_row, conv_state)


def _gdn_scan_kernel(u_ref, w_ref, qg_ref, kdt_ref, qkc_ref, egl_ref, z_ref, s0_ref, dnw_ref,
                     o_ref, s_ref, *, chunk, n_chunks, seqs_per_step):
    t = pl.program_id(1)

    @pl.when(t == 0)
    def _():
        s_ref[...] = s0_ref[...]

    dnw = dnw_ref[...]
    for c in range(n_chunks):
        si = c if seqs_per_step > 1 else 0
        rows = slice(c * chunk, (c + 1) * chunk)
        for h in range(N_HEADS):
            cols = slice(h * HEAD_DIM, (h + 1) * HEAD_DIM)
            s_old = s_ref[si, h]
            sb = s_old.astype(BF16)
            both = _dot(jnp.concatenate([w_ref[rows, cols], qg_ref[rows, cols]], axis=0), sb)
            vn = (u_ref[rows, cols] - both[:chunk]).astype(BF16)
            o = both[chunk:] + _dot(qkc_ref[h, rows, :], vn)
            s_ref[si, h] = s_old * egl_ref[0, c:c + 1, cols] + _dot(kdt_ref[h, :, rows], vn)
            o_ref[rows, cols] = (_rms(o, dnw) * _silu(z_ref[rows, cols])).astype(o_ref.dtype)


def _gdn_scan(intra, z_src, z_col_block, s0, dnw, *, n_seq, t, chunk, chunks_per_step, seqs_per_step=1):
    u, w, qg, kdt, qkc, egl = intra
    m = u.shape[0]
    width = N_HEADS * HEAD_DIM
    rows = chunk * chunks_per_step
    if seqs_per_step > 1:
        assert seqs_per_step == n_seq == chunks_per_step and t == chunk
        nsteps, n_outer = 1, 1
    else:
        nsteps, n_outer = t // rows, n_seq

    def rb(b, i):
        return b * nsteps + i

    in_specs = [
        pl.BlockSpec((rows, width), lambda b, i: (rb(b, i), 0)),
        pl.BlockSpec((rows, width), lambda b, i: (rb(b, i), 0)),
        pl.BlockSpec((rows, width), lambda b, i: (rb(b, i), 0)),
        pl.BlockSpec((N_HEADS, HEAD_DIM, rows), lambda b, i: (0, 0, rb(b, i))),
        pl.BlockSpec((N_HEADS, rows, chunk), lambda b, i: (0, rb(b, i), 0)),
        pl.BlockSpec((1, 8, width), lambda b, i: (rb(b, i), 0, 0)),
        pl.BlockSpec((rows, width), lambda b, i: (rb(b, i), z_col_block)),
        pl.BlockSpec((seqs_per_step, N_HEADS, HEAD_DIM, HEAD_DIM), lambda b, i: (b, 0, 0, 0)),
        pl.BlockSpec((1, HEAD_DIM), lambda b, i: (0, 0)),
    ]
    out_specs = [
        pl.BlockSpec((rows, width), lambda b, i: (rb(b, i), 0)),
        pl.BlockSpec((seqs_per_step, N_HEADS, HEAD_DIM, HEAD_DIM), lambda b, i: (b, 0, 0, 0)),
    ]
    return pl.pallas_call(
        functools.partial(_gdn_scan_kernel, chunk=chunk, n_chunks=chunks_per_step,
                          seqs_per_step=seqs_per_step),
        grid=(n_outer, nsteps),
        in_specs=in_specs,
        out_specs=out_specs,
        out_shape=[jax.ShapeDtypeStruct((m, width), BF16),
                   jax.ShapeDtypeStruct((n_seq, N_HEADS, HEAD_DIM, HEAD_DIM), F32)],
        compiler_params=pltpu.CompilerParams(
            dimension_semantics=("parallel", "arbitrary"), vmem_limit_bytes=V7X_VMEM_LIMIT_BYTES),
        name="gdn_scan",
    )(u, w, qg, kdt, qkc, egl, z_src, s0, dnw)


def _pad_lanes(v, offset):
    out = jnp.zeros((1, HEAD_DIM), F32)
    return lax.dynamic_update_slice(out, v.reshape(1, -1).astype(F32), (0, offset))


def kernel(x_prompt, x_sample, cache_k, cache_v, state_ssm, state_conv, page_table, ffn1_pre_w, ffn1_post_w, ffn1_gate, ffn1_up, ffn1_down, mix_pre_w, mix_post_w, w_in, conv_w, a_log, dt_bias, delta_norm_w, lambda_q1, lambda_k1, lambda_q2, lambda_k2, subln_w, w_out, ffn2_pre_w, ffn2_post_w, ffn2_gate, ffn2_up, ffn2_down):
    depth = w_in.shape[0]
    batch, t, d = x_prompt.shape
    n_seq, tn, _ = x_sample.shape
    a_w = N_HEADS * HEAD_DIM
    n_main = 3 * a_w + 3 * a_w + a_w
    yp = x_prompt.reshape(batch * t, d)
    ys = x_sample.reshape(n_seq * tn, d)
    outs = [[] for _ in range(8)]
    for l in range(depth):
        lam_init = 0.8 - 0.6 * math.exp(-0.3 * l)
        lam_rows = jnp.stack([lambda_q1[l], lambda_k1[l], lambda_q2[l], lambda_k2[l]]).astype(F32)
        row = lambda v: v[l].reshape(1, -1).astype(F32)
        w_gates = jnp.pad(w_in[l][:, n_main:], ((0, 0), (0, HEAD_DIM - 2 * N_HEADS))).astype(BF16)
        alog_row = _pad_lanes(a_log[l], N_HEADS)
        dtb_row = _pad_lanes(dt_bias[l], N_HEADS)
        dnw = row(delta_norm_w)
        subw = row(subln_w)
        cw = conv_w[l].astype(F32)
        ms = n_seq * tn

        ys, wg1, wu1, wd1 = _ffn(ys, row(ffn1_pre_w), row(ffn1_post_w), ffn1_gate[l], ffn1_up[l],
                                 ffn1_down[l], tm=ms, tf=512, emit_bf16=True)
        yp = _ffn(yp, row(ffn1_pre_w), row(ffn1_post_w), wg1, wu1, wd1, tm=512, tf=512)

        ps, gs, ks, vs, w_main = _inproj(ys, row(mix_pre_w), w_in[l], w_gates, tm=ms, tn=a_w, n=n_main,
                                         emit_bf16=True)
        qs = ps[:, :a_w].reshape(n_seq, tn * N_HEADS, HEAD_DIM)
        ks = ks.reshape(n_seq, tn * N_HEADS, HEAD_DIM)
        vs = vs.reshape(n_seq, tn * N_HEADS, HEAD_DIM)
        oa_s = _attn_sample(page_table, lam_rows, subw, qs, ks, vs, cache_k[l], cache_v[l],
                            lam_init=lam_init, group=8)
        oa_s = oa_s.reshape(ms, a_w)
        intra_s = _gdn_intra_sample(ps, gs, cw, alog_row, dtb_row, state_conv[l], n_seq=n_seq, tn=tn)
        ob_s, s_s = _gdn_scan(intra_s, ps, 6, state_ssm[l].astype(F32), dnw, n_seq=n_seq, t=tn,
                              chunk=tn, chunks_per_step=n_seq, seqs_per_step=n_seq)
        ys, wo = _outproj(ys, oa_s, ob_s, w_out[l], row(mix_post_w), tm=ms, emit_bf16=True)

        pp, gp, kp, vp = _inproj(yp, row(mix_pre_w), w_main, w_gates, tm=1024, tn=a_w, n=n_main)
        oa_p = _attn_prompt(lam_rows, subw.reshape(HEAD_DIM, 1), pp, kp, vp, batch=batch, t=t,
                            lam_init=lam_init, tq=256, hpp=2)
        intra_p = _gdn_intra_prompt(pp, gp, cw, alog_row, dtb_row, batch=batch, t=t)
        zero_s = jnp.zeros((batch, N_HEADS, HEAD_DIM, HEAD_DIM), F32)
        ob_p, s_p = _gdn_scan(intra_p, pp, 6, zero_s, dnw, n_seq=batch, t=t, chunk=CHUNK,
                              chunks_per_step=GDN_ROWS // CHUNK)
        yp = _outproj(yp, oa_p, ob_p, wo, row(mix_post_w), tm=512)

        ys, wg2, wu2, wd2 = _ffn(ys, row(ffn2_pre_w), row(ffn2_post_w), ffn2_gate[l], ffn2_up[l],
                                 ffn2_down[l], tm=ms, tf=512, emit_bf16=True)
        yp = _ffn(yp, row(ffn2_pre_w), row(ffn2_post_w), wg2, wu2, wd2, tm=512, tf=512)

        ppb = pp.reshape(batch, t, -1)
        psb = ps.reshape(n_seq, tn, -1)
        outs[0].append(kp.reshape(batch, t, N_HEADS, HEAD_DIM))
        outs[1].append(vp.reshape(batch, t, N_HEADS, HEAD_DIM))
        outs[2].append(s_p.astype(state_ssm.dtype))
        outs[3].append(ppb[:, t - (CONV_W - 1):, 3 * a_w:6 * a_w])
        outs[4].append(ks.reshape(n_seq, tn, N_HEADS, HEAD_DIM))
        outs[5].append(vs.reshape(n_seq, tn, N_HEADS, HEAD_DIM))
        outs[6].append(s_s.astype(state_ssm.dtype))
        outs[7].append(jnp.concatenate([state_conv[l].astype(psb.dtype), psb[:, :, 3 * a_w:6 * a_w]],
                                       axis=1)[:, tn:])
    return (yp.reshape(batch, t, d), ys.reshape(n_seq, tn, d)) + tuple(jnp.stack(o) for o in outs)
```

```python
import functools
import math

import jax
import jax.numpy as jnp
from jax import lax
from jax.experimental import pallas as pl
from jax.experimental.pallas import tpu as pltpu

F32 = jnp.float32
BF16 = jnp.bfloat16
EPS = 1e-6
NEG_INF = float("-inf")
LOG2E = 1.4426950408889634

V7X_VMEM_LIMIT_BYTES = 60 * 1024 * 1024

QK_DIM = 64
HEAD_DIM = 128
N_HEADS = 8
CONV_W = 4
CHUNK = 64
GDN_ROWS = 256

def _dot(a, b):
    return jnp.dot(a, b, preferred_element_type=F32)


def _dot_nt(a, b):
    return lax.dot_general(a, b, (((1,), (1,)), ((), ())), preferred_element_type=F32)


def _split3(x):
    hi = x.astype(BF16)
    r = x - hi.astype(F32)
    mid = r.astype(BF16)
    lo = (r - mid.astype(F32)).astype(BF16)
    return hi, mid, lo


def _dot_sel_l(sel, x):
    hi, mid, lo = _split3(x)
    return _dot(sel, hi) + _dot(sel, mid) + _dot(sel, lo)


def _dot_sel_r(x, sel):
    hi, mid, lo = _split3(x)
    return _dot(hi, sel) + _dot(mid, sel) + _dot(lo, sel)


def _dot_hp(a, b):
    ah = a.astype(BF16)
    al = (a - ah.astype(F32)).astype(BF16)
    bh = b.astype(BF16)
    bl = (b - bh.astype(F32)).astype(BF16)
    return _dot(ah, bh) + _dot(ah, bl) + _dot(al, bh)


def _rms(x, w):
    return x * lax.rsqrt(jnp.mean(x * x, axis=-1, keepdims=True) + EPS) * w


def _silu(x):
    return x * jax.nn.sigmoid(x)


def _softplus(x):
    return jnp.maximum(x, 0.0) + jnp.log1p(jnp.exp(-jnp.abs(x)))


def _lambda_full(lam_ref, lam_init):
    l = lam_ref[...]
    a = jnp.sum(l[0:1] * l[1:2], axis=-1, keepdims=True)
    b = jnp.sum(l[2:3] * l[3:4], axis=-1, keepdims=True)
    return jnp.exp(a) - jnp.exp(b) + lam_init


def _idiv(x, n):
    assert n & (n - 1) == 0
    return x >> (n.bit_length() - 1)


def _imod(x, n):
    assert n & (n - 1) == 0
    return x & (n - 1)


def _pow2_neg(k):
    return lax.bitcast_convert_type((127 - k) << 23, F32)


def _ffn_kernel(x_ref, prew_ref, postw_ref, wg_ref, wu_ref, wd_ref, o_ref, *rest, nf, emit_bf16):
    h_ref = rest[-1]
    f = pl.program_id(1)

    @pl.when(f == 0)
    def _():
        h_ref[...] = _rms(x_ref[...], prew_ref[...]).astype(BF16)
        o_ref[...] = jnp.zeros_like(o_ref)

    wg, wu, wd = wg_ref[...], wu_ref[...], wd_ref[...]
    if emit_bf16:
        wg, wu, wd = wg.astype(BF16), wu.astype(BF16), wd.astype(BF16)
        rest[0][...], rest[1][...], rest[2][...] = wg, wu, wd
    h = h_ref[...]
    g = _dot(h, wg)
    u = _dot(h, wu)
    o_ref[...] += _dot((_silu(g) * u).astype(BF16), wd)

    @pl.when(f == nf - 1)
    def _():
        o_ref[...] = x_ref[...] + 0.5 * _rms(o_ref[...], postw_ref[...])


def _ffn(x, pre_w, post_w, wg, wu, wd, *, tm, tf, emit_bf16=False):
    m, d = x.shape
    ff = wg.shape[1]
    nf = ff // tf
    assert not emit_bf16 or m == tm
    gu_spec = pl.BlockSpec((d, tf), lambda i, f: (0, f))
    dn_spec = pl.BlockSpec((tf, d), lambda i, f: (f, 0))
    out_specs = [pl.BlockSpec((tm, d), lambda i, f: (i, 0))]
    out_shape = [jax.ShapeDtypeStruct((m, d), F32)]
    if emit_bf16:
        out_specs += [gu_spec, gu_spec, dn_spec]
        out_shape += [jax.ShapeDtypeStruct(w.shape, BF16) for w in (wg, wu, wd)]
    outs = pl.pallas_call(
        functools.partial(_ffn_kernel, nf=nf, emit_bf16=emit_bf16),
        grid=(m // tm, nf),
        in_specs=[
            pl.BlockSpec((tm, d), lambda i, f: (i, 0)),
            pl.BlockSpec((1, d), lambda i, f: (0, 0)),
            pl.BlockSpec((1, d), lambda i, f: (0, 0)),
            gu_spec, gu_spec, dn_spec,
        ],
        out_specs=out_specs,
        out_shape=out_shape,
        scratch_shapes=[pltpu.VMEM((tm, d), BF16)],
        compiler_params=pltpu.CompilerParams(
            dimension_semantics=("parallel", "arbitrary"), vmem_limit_bytes=V7X_VMEM_LIMIT_BYTES),
        name="ffn",
    )(x, pre_w, post_w, wg, wu, wd)
    return outs if emit_bf16 else outs[0]


def _inproj_kernel(x_ref, prew_ref, w_ref, wg_ref, o_ref, og_ref, ko_ref, vo_ref, *rest, emit_bf16):
    h_ref = rest[-1]
    j = pl.program_id(1)

    @pl.when(j == 0)
    def _():
        h = _rms(x_ref[...], prew_ref[...]).astype(BF16)
        h_ref[...] = h
        og_ref[...] = _dot(h, wg_ref[...])

    w = w_ref[...]
    if emit_bf16:
        w = w.astype(BF16)
        rest[0][...] = w
    y = _dot(h_ref[...], w)
    o_ref[...] = y

    @pl.when(j == 1)
    def _():
        ko_ref[...] = y

    @pl.when(j == 2)
    def _():
        vo_ref[...] = y


def _inproj(x, pre_w, w_main, w_gates, *, tm, tn, n, layer=None, emit_bf16=False):
    m, d = x.shape
    ng = w_gates.shape[1]
    assert not emit_bf16 or m == tm
    w_spec = pl.BlockSpec((d, tn), lambda i, j: (0, j))
    w_in_spec = w_spec if layer is None else pl.BlockSpec((None, d, tn), lambda i, j: (layer, 0, j))
    row_spec = pl.BlockSpec((tm, tn), lambda i, j: (i, 0))
    out_specs = [pl.BlockSpec((tm, tn), lambda i, j: (i, j)), pl.BlockSpec((tm, ng), lambda i, j: (i, 0)),
                 row_spec, row_spec]
    out_shape = [jax.ShapeDtypeStruct((m, n), F32), jax.ShapeDtypeStruct((m, ng), F32),
                 jax.ShapeDtypeStruct((m, tn), F32), jax.ShapeDtypeStruct((m, tn), F32)]
    if emit_bf16:
        out_specs.append(w_spec)
        out_shape.append(jax.ShapeDtypeStruct((d, n), BF16))
    return pl.pallas_call(
        functools.partial(_inproj_kernel, emit_bf16=emit_bf16),
        grid=(m // tm, n // tn),
        in_specs=[
            pl.BlockSpec((tm, d), lambda i, j: (i, 0)),
            pl.BlockSpec((1, d), lambda i, j: (0, 0)),
            w_in_spec,
            pl.BlockSpec((d, ng), lambda i, j: (0, 0)),
        ],
        out_specs=out_specs,
        out_shape=out_shape,
        scratch_shapes=[pltpu.VMEM((tm, d), BF16)],
        compiler_params=pltpu.CompilerParams(
            dimension_semantics=("parallel", "arbitrary"), vmem_limit_bytes=V7X_VMEM_LIMIT_BYTES),
        name="inproj",
    )(x, pre_w, w_main, w_gates)


def _outproj_kernel(x_ref, oa_ref, ob_ref, wa_ref, wb_ref, postw_ref, o_ref, *rest, emit_bf16):
    wa, wb = wa_ref[...], wb_ref[...]
    if emit_bf16:
        wa, wb = wa.astype(BF16), wb.astype(BF16)
        ka = wa.shape[0]
        rest[0][:ka, :] = wa
        rest[0][ka:, :] = wb
    y = _dot(oa_ref[...], wa) + _dot(ob_ref[...], wb)
    o_ref[...] = x_ref[...] + _rms(y, postw_ref[...])


def _outproj(x, oa, ob, w_out, post_w, *, tm, emit_bf16=False):
    m, d = x.shape
    ka = oa.shape[1]
    assert not emit_bf16 or m == tm
    out_specs = [pl.BlockSpec((tm, d), lambda i: (i, 0))]
    out_shape = [jax.ShapeDtypeStruct((m, d), F32)]
    if emit_bf16:
        out_specs.append(pl.BlockSpec(w_out.shape, lambda i: (0, 0)))
        out_shape.append(jax.ShapeDtypeStruct(w_out.shape, BF16))
    outs = pl.pallas_call(
        functools.partial(_outproj_kernel, emit_bf16=emit_bf16),
        grid=(m // tm,),
        in_specs=[
            pl.BlockSpec((tm, d), lambda i: (i, 0)),
            pl.BlockSpec((tm, ka), lambda i: (i, 0)),
            pl.BlockSpec((tm, ka), lambda i: (i, 0)),
            pl.BlockSpec((ka, d), lambda i: (0, 0)),
            pl.BlockSpec((ka, d), lambda i: (1, 0)),
            pl.BlockSpec((1, d), lambda i: (0, 0)),
        ],
        out_specs=out_specs,
        out_shape=out_shape,
        compiler_params=pltpu.CompilerParams(
            dimension_semantics=("parallel",), vmem_limit_bytes=V7X_VMEM_LIMIT_BYTES),
        name="outproj",
    )(x, oa, ob, w_out, w_out, post_w)
    return outs if emit_bf16 else outs[0]


def _attn_prompt_kernel(lam_ref, subw_ref, q_ref, k_ref, v_ref, o_ref,
                        kb_ref, vt_ref, bias_ref, *, t, tq, hpp, lam_init):
    hg = pl.program_id(1)
    nq = t // tq
    lam = _lambda_full(lam_ref, lam_init)
    lane = lax.broadcasted_iota(jnp.int32, (1, HEAD_DIM), 1)
    first_map = lane < QK_DIM
    kk = lax.broadcasted_iota(jnp.int32, (tq, tq), 0)
    qq = lax.broadcasted_iota(jnp.int32, (tq, tq), 1)
    slopes = [_pow2_neg(jnp.full((1, 1), hg * hpp + hh + 1, jnp.int32)) * LOG2E for hh in range(hpp)]
    er = lax.broadcasted_iota(jnp.int32, (HEAD_DIM, HEAD_DIM), 0)
    eye_b = (er == lax.broadcasted_iota(jnp.int32, (HEAD_DIM, HEAD_DIM), 1)).astype(BF16)
    for hh in range(hpp):
        cols = slice(hh * HEAD_DIM, (hh + 1) * HEAD_DIM)
        b = slopes[hh] * (kk - qq).astype(F32)
        bias_ref[hh, 0] = b
        bias_ref[hh, 1] = jnp.where(qq >= kk, b, NEG_INF)
        vt_ref[hh] = _dot_nt(eye_b, v_ref[:, cols].astype(BF16)).astype(BF16)
    kb_ref[...] = k_ref[...].astype(BF16)
    scale = QK_DIM ** -0.5 * LOG2E
    subw_col = subw_ref[...]

    def q_block(i, _):
        rows_i = pl.ds(pl.multiple_of(i * tq, tq), tq)
        qs = []
        for hh in range(hpp):
            q = q_ref[rows_i, hh * HEAD_DIM:(hh + 1) * HEAD_DIM] * scale
            qs.append(jnp.concatenate(
                [jnp.where(first_map, q, 0.0), jnp.where(first_map, 0.0, q)], axis=0).astype(BF16))

        def scores(j):
            rows_j = pl.ds(pl.multiple_of(j * tq, tq), tq)
            diag = jnp.asarray(j == i, jnp.int32)
            out = []
            for hh in range(hpp):
                b = bias_ref[hh, diag]
                out.append(_dot_nt(kb_ref[rows_j, hh * HEAD_DIM:(hh + 1) * HEAD_DIM], qs[hh])
                           + jnp.concatenate([b, b], axis=1))
            return tuple(out)

        def consume(j, sts, state):
            rows_j = pl.ds(pl.multiple_of(j * tq, tq), tq)
            blocks_away = jnp.asarray((i - j) * tq, F32)
            hs = range(hpp)
            far = [slopes[hh] * blocks_away for hh in hs]
            m_new = [jnp.maximum(state[hh][0], jnp.max(sts[hh], axis=0, keepdims=True) - far[hh])
                     for hh in hs]
            p = [jnp.exp2(sts[hh] - (far[hh] + m_new[hh])) for hh in hs]
            alpha = [jnp.exp2(state[hh][0] - m_new[hh]) for hh in hs]
            l_new = [alpha[hh] * state[hh][1] + jnp.sum(p[hh], axis=0, keepdims=True) for hh in hs]
            pv = [_dot(vt_ref[hh, :, rows_j], p[hh].astype(BF16)) for hh in hs]
            return tuple((m_new[hh], l_new[hh], alpha[hh] * state[hh][2] + pv[hh]) for hh in hs)

        def step(j, carry):
            sts, state = carry
            nxt = scores(j + 1)
            return nxt, consume(j, sts, state)

        init = tuple((jnp.full((1, 2 * tq), NEG_INF, F32), jnp.zeros((1, 2 * tq), F32),
                      jnp.zeros((HEAD_DIM, 2 * tq), F32)) for _ in range(hpp))
        sts, state = lax.fori_loop(0, i, step, (scores(0), init))
        state = consume(i, sts, state)
        for hh in range(hpp):
            _, l_fin, acc_fin = state[hh]
            ot = acc_fin / l_fin
            dt = ot[:, :tq] - lam * ot[:, tq:]
            dt = dt * lax.rsqrt(jnp.mean(dt * dt, axis=0, keepdims=True) + EPS) * subw_col
            o_ref[rows_i, hh * HEAD_DIM:(hh + 1) * HEAD_DIM] = (
                (dt * (1.0 - lam_init)).T.astype(o_ref.dtype))
        return 0

    lax.fori_loop(0, nq, q_block, 0)


def _attn_prompt(lam_rows, subln_col, proj, k, v, *, batch, t, lam_init, tq, hpp):
    m = proj.shape[0]
    ng = N_HEADS // hpp
    w = hpp * HEAD_DIM
    return pl.pallas_call(
        functools.partial(_attn_prompt_kernel, t=t, tq=tq, hpp=hpp, lam_init=lam_init),
        grid=(batch, ng),
        in_specs=[
            pl.BlockSpec((4, QK_DIM), lambda b, g: (0, 0)),
            pl.BlockSpec((HEAD_DIM, 1), lambda b, g: (0, 0)),
            pl.BlockSpec((t, w), lambda b, g: (b, g)),
            pl.BlockSpec((t, w), lambda b, g: (b, g)),
            pl.BlockSpec((t, w), lambda b, g: (b, g)),
        ],
        out_specs=pl.BlockSpec((t, w), lambda b, g: (b, g)),
        out_shape=jax.ShapeDtypeStruct((m, N_HEADS * HEAD_DIM), BF16),
        scratch_shapes=[
            pltpu.VMEM((t, w), BF16),
            pltpu.VMEM((hpp, HEAD_DIM, t), BF16),
            pltpu.VMEM((hpp, 2, tq, tq), F32),
        ],
        compiler_params=pltpu.CompilerParams(
            dimension_semantics=("parallel", "parallel"), vmem_limit_bytes=V7X_VMEM_LIMIT_BYTES),
        name="attn_prompt",
    )(lam_rows, subln_col, proj, k, v)


def _attn_sample_kernel(pt_ref, lam_ref, subw_ref, q_ref, kn_ref, vn_ref, *rest,
                        n_pages, page, tn, group, lam_init):
    del pt_ref
    kc_refs = rest[:group]
    vc_refs = rest[group:2 * group]
    o_ref, qb_ref, bias_ref, m_ref, l_ref, acc_ref = rest[2 * group:]
    p = pl.program_id(1)
    nrow = 2 * tn * N_HEADS
    ncol = page * N_HEADS
    past = n_pages * page
    row = lax.broadcasted_iota(jnp.int32, (nrow, 1), 0)
    slope = _pow2_neg(_imod(row, N_HEADS) + 1)

    @pl.when(p == 0)
    def _():
        q = q_ref[0] * (QK_DIM ** -0.5)
        lane = lax.broadcasted_iota(jnp.int32, (1, HEAD_DIM), 1)
        first_map = lane < QK_DIM
        qb_ref[...] = jnp.concatenate(
            [jnp.where(first_map, q, 0.0), jnp.where(first_map, 0.0, q)], axis=0).astype(BF16)
        rr = lax.broadcasted_iota(jnp.int32, (nrow, ncol), 0)
        cc = lax.broadcasted_iota(jnp.int32, (nrow, ncol), 1)
        tq_ = _idiv(_imod(rr, tn * N_HEADS), N_HEADS)
        same_head = _imod(rr, N_HEADS) == _imod(cc, N_HEADS)
        bias_ref[...] = jnp.where(same_head, slope * (_idiv(cc, N_HEADS) - tq_).astype(F32), NEG_INF)
        m_ref[...] = jnp.full(m_ref.shape, NEG_INF, F32)
        l_ref[...] = jnp.zeros(l_ref.shape, F32)
        acc_ref[...] = jnp.zeros(acc_ref.shape, F32)

    def lane_fold(x, op):
        out = x[:, :HEAD_DIM]
        for i in range(1, x.shape[1] // HEAD_DIM):
            out = op(out, x[:, i * HEAD_DIM:(i + 1) * HEAD_DIM])
        return out

    def update(scores, offs, vals):
        m_old = m_ref[...]
        m_loc = lane_fold(scores[0], jnp.maximum) - offs[0]
        for s, off in zip(scores[1:], offs[1:]):
            m_loc = jnp.maximum(m_loc, lane_fold(s, jnp.maximum) - off)
        m_new = jnp.maximum(m_old, jnp.max(m_loc, axis=-1, keepdims=True))
        alpha = jnp.exp(m_old - m_new)
        pes = [jnp.exp(s - (off + m_new)) for s, off in zip(scores, offs)]
        l_part = lane_fold(pes[0], jnp.add)
        for pe in pes[1:]:
            l_part = l_part + lane_fold(pe, jnp.add)
        pv = _dot(jnp.concatenate([pe.astype(BF16) for pe in pes], axis=1),
                  jnp.concatenate(vals, axis=0))
        l_ref[...] = alpha * l_ref[...] + l_part
        acc_ref[...] = alpha * acc_ref[...] + pv
        m_ref[...] = m_new

    qb = qb_ref[...]
    bias = bias_ref[...]
    scores, offs, vals = [], [], []
    for g in range(group):
        kf = kc_refs[g][0].reshape(ncol, HEAD_DIM).astype(BF16)
        vals.append(vc_refs[g][0].reshape(ncol, HEAD_DIM).astype(BF16))
        scores.append(_dot_nt(qb, kf) + bias)
        offs.append(slope * jnp.asarray(past - (p * group + g) * page, F32))
    update(scores, offs, vals)

    @pl.when(p == n_pages // group - 1)
    def _():
        ncn = tn * N_HEADS
        rr = lax.broadcasted_iota(jnp.int32, (nrow, HEAD_DIM), 0)
        cc = lax.broadcasted_iota(jnp.int32, (nrow, HEAD_DIM), 1)
        tq_ = _idiv(_imod(rr, ncn), N_HEADS)
        tk_ = _idiv(cc, N_HEADS)
        ok = (cc < ncn) & (_imod(rr, N_HEADS) == _imod(cc, N_HEADS)) & (tk_ <= tq_)
        bias_new = jnp.where(ok, slope * (tk_ - tq_).astype(F32), NEG_INF)
        pad = jnp.zeros((HEAD_DIM - ncn, HEAD_DIM), BF16)
        kn = jnp.concatenate([kn_ref[0].astype(BF16), pad], axis=0)
        vn = jnp.concatenate([vn_ref[0].astype(BF16), pad], axis=0)
        update([_dot_nt(qb, kn) + bias_new], [jnp.zeros((nrow, 1), F32)], [vn])
        lam = _lambda_full(lam_ref, lam_init)
        o = acc_ref[...] / jnp.sum(l_ref[...], axis=-1, keepdims=True)
        d = o[:ncn] - lam * o[ncn:]
        o_ref[0] = (_rms(d, subw_ref[...]) * (1.0 - lam_init)).astype(o_ref.dtype)


def _attn_sample(page_table, lam_rows, subln_w, q, k_new, v_new, cache_k, cache_v, *, lam_init, group):
    n_seq, n_pages = page_table.shape
    page = cache_k.shape[1]
    tn = q.shape[1] // N_HEADS
    nrow = 2 * tn * N_HEADS
    ncol = page * N_HEADS
    assert n_pages % group == 0

    def page_spec(g):
        return pl.BlockSpec((1, page, N_HEADS, HEAD_DIM),
                            lambda b, p, pt: (pt[b, p * group + g], 0, 0, 0))

    new_spec = pl.BlockSpec((1, tn * N_HEADS, HEAD_DIM), lambda b, p, pt: (b, 0, 0))
    grid_spec = pltpu.PrefetchScalarGridSpec(
        num_scalar_prefetch=1,
        grid=(n_seq, n_pages // group),
        in_specs=[
            pl.BlockSpec((4, QK_DIM), lambda b, p, pt: (0, 0)),
            pl.BlockSpec((1, HEAD_DIM), lambda b, p, pt: (0, 0)),
            new_spec, new_spec, new_spec,
        ] + [page_spec(g) for g in range(group)] * 2,
        out_specs=new_spec,
        scratch_shapes=[
            pltpu.VMEM((nrow, HEAD_DIM), BF16),
            pltpu.VMEM((nrow, ncol), F32),
            pltpu.VMEM((nrow, 1), F32),
            pltpu.VMEM((nrow, HEAD_DIM), F32),
            pltpu.VMEM((nrow, HEAD_DIM), F32),
        ],
    )
    return pl.pallas_call(
        functools.partial(_attn_sample_kernel, n_pages=n_pages, page=page, tn=tn, group=group,
                          lam_init=lam_init),
        grid_spec=grid_spec,
        out_shape=jax.ShapeDtypeStruct((n_seq, tn * N_HEADS, HEAD_DIM), BF16),
        compiler_params=pltpu.CompilerParams(
            dimension_semantics=("parallel", "arbitrary"), vmem_limit_bytes=V7X_VMEM_LIMIT_BYTES),
        name="attn_sample",
    )(page_table, lam_rows, subln_w, q, k_new, v_new, *([cache_k] * group), *([cache_v] * group))


def _gdn_n_masks(chunk):
    return 3 + (chunk.bit_length() - 1)


def _gdn_fill_masks(mask_ref, *, rows, chunk):
    ri = lax.broadcasted_iota(jnp.int32, (rows, rows), 0)
    ci = lax.broadcasted_iota(jnp.int32, (rows, rows), 1)
    same = _idiv(ri, chunk) == _idiv(ci, chunk)
    mask_ref[0] = (same & (ri >= ci)).astype(F32)
    mask_ref[1] = (same & (ri > ci)).astype(F32)
    mask_ref[2] = (ri == ci).astype(F32)
    s, k = 1, 3
    while s < chunk:
        mask_ref[k] = ((_idiv(ri, 2 * s) == _idiv(ci, 2 * s)) & (_imod(_idiv(ri, s), 2) == 1)
                       & (_imod(_idiv(ci, s), 2) == 0)).astype(F32)
        s, k = 2 * s, k + 1


def _gdn_intra_all(act, gates, alog_row, dtb_row, mask_ref, u_ref, w_ref, qg_ref, kdt_ref, qkc_ref,
                   egl_ref, *, rows, chunk):
    n_chunks = rows // chunk
    n_levels = _gdn_n_masks(chunk) - 3
    beta_all = jax.nn.sigmoid(gates)
    g_all = -jnp.exp(alog_row) * _softplus(gates + dtb_row)
    gcum_all = _dot_sel_l(mask_ref[0].astype(BF16), g_all)
    gtot_all = jnp.concatenate(
        [jnp.broadcast_to(gcum_all[(c + 1) * chunk - 1:(c + 1) * chunk], (chunk, HEAD_DIM))
         for c in range(n_chunks)], axis=0)
    gcum_t = gcum_all.T
    fr = lax.broadcasted_iota(jnp.int32, (rows, chunk), 0)
    fc = lax.broadcasted_iota(jnp.int32, (rows, chunk), 1)
    fold = (_imod(fr, chunk) == fc).astype(BF16)
    er = lax.broadcasted_iota(jnp.int32, (HEAD_DIM, HEAD_DIM), 0)
    eye_b = (er == lax.broadcasted_iota(jnp.int32, (HEAD_DIM, HEAD_DIM), 1)).astype(BF16)

    def lanes(col):
        return jnp.broadcast_to(col, (rows, HEAD_DIM))

    def square(x):
        if rows >= HEAD_DIM:
            return jnp.concatenate([x] * (rows // HEAD_DIM), axis=1)
        return x[:, :rows]

    heads = range(N_HEADS)
    egl_rows, lmats, lbs, tinvs, rhss = [], [], [], [], []
    for h in heads:
        cols = slice(h * HEAD_DIM, (h + 1) * HEAD_DIM)
        aq = act[:, h * HEAD_DIM:(h + 1) * HEAD_DIM]
        ak = act[:, (N_HEADS + h) * HEAD_DIM:(N_HEADS + h + 1) * HEAD_DIM]
        av = act[:, (2 * N_HEADS + h) * HEAD_DIM:(2 * N_HEADS + h + 1) * HEAD_DIM]
        qn = aq * lax.rsqrt(jnp.sum(aq * aq, axis=-1, keepdims=True) + 1e-6) * (HEAD_DIM ** -0.5)
        kn = ak * lax.rsqrt(jnp.sum(ak * ak, axis=-1, keepdims=True) + 1e-6)
        beta = lanes(beta_all[:, h:h + 1])
        g_cum = lanes(gcum_all[:, N_HEADS + h:N_HEADS + h + 1])
        g_tot = lanes(gtot_all[:, N_HEADS + h:N_HEADS + h + 1])
        g_row = gcum_t[N_HEADS + h:N_HEADS + h + 1, :]
        decay = jnp.exp(jnp.minimum(square(g_cum) - g_row, 0.0)) * mask_ref[0]

        kbeta = kn * beta
        knb = kn.astype(BF16)
        lmat = mask_ref[1] * (_dot_nt(kbeta.astype(BF16), knb) * decay)
        lmats.append(lmat)
        lbs.append(lmat.astype(BF16))
        tinvs.append(mask_ref[2] - lmat * mask_ref[3])
        e_cum = jnp.exp(g_cum)
        rhss.append(jnp.concatenate([av * beta, kbeta * e_cum], axis=1).astype(BF16))
        qk = (_dot_nt(qn.astype(BF16), knb) * decay).astype(BF16)
        qkc_ref[h] = _dot(qk, fold).astype(BF16)
        qg_ref[:, cols] = (qn * e_cum).astype(BF16)
        kd = (kn * jnp.exp(g_tot - g_cum)).astype(BF16)
        kdt_ref[h] = _dot_nt(eye_b, kd).astype(BF16)
        egl = jnp.exp(g_tot)
        egl_rows.append(jnp.concatenate(
            [egl[c * chunk:c * chunk + 1] for c in range(n_chunks)]
            + [jnp.zeros((8 - n_chunks, HEAD_DIM), F32)] * (n_chunks < 8), axis=0))
    egl_ref[0] = jnp.concatenate(egl_rows, axis=1)

    for k in range(1, n_levels):
        tbs = [tinvs[h].astype(BF16) for h in heads]
        mids = [_dot(tbs[h], lbs[h]).astype(BF16) for h in heads]
        tinvs = [tinvs[h] - mask_ref[3 + k] * _dot(mids[h], tbs[h]) for h in heads]
    resids = [mask_ref[2] - tinvs[h] - _dot_hp(lmats[h], tinvs[h]) for h in heads]
    tinvs = [tinvs[h] + _dot(tinvs[h].astype(BF16), resids[h].astype(BF16)) for h in heads]
    for h in heads:
        cols = slice(h * HEAD_DIM, (h + 1) * HEAD_DIM)
        uw = _dot(tinvs[h].astype(BF16), rhss[h])
        u_ref[:, cols] = uw[:, :HEAD_DIM]
        w_ref[:, cols] = uw[:, HEAD_DIM:].astype(BF16)


def _gdn_intra_prompt_kernel(x_ref, gates_ref, cw_ref, alog_ref, dtb_ref,
                             u_ref, w_ref, qg_ref, kdt_ref, qkc_ref, egl_ref, pad_ref, mask_ref,
                             *, rows, chunk):
    t = pl.program_id(1)

    @pl.when(t == 0)
    def _():
        pad_ref[0:8, :] = jnp.zeros((8, pad_ref.shape[1]), F32)
        _gdn_fill_masks(mask_ref, rows=rows, chunk=chunk)

    pad_ref[8:, :] = x_ref[...]
    cw = cw_ref[...]
    acc = pad_ref[pl.ds(8 - (CONV_W - 1), rows), :] * cw[0:1]
    for j in range(1, CONV_W):
        acc = acc + pad_ref[pl.ds(8 - (CONV_W - 1) + j, rows), :] * cw[j:j + 1]
    pad_ref[0:8, :] = x_ref[rows - 8:, :]
    _gdn_intra_all(_silu(acc), gates_ref[...], alog_ref[...], dtb_ref[...], mask_ref,
                   u_ref, w_ref, qg_ref, kdt_ref, qkc_ref, egl_ref, rows=rows, chunk=chunk)


def _gdn_intra_sample_kernel(x_ref, gates_ref, cw_ref, alog_ref, dtb_ref, buf_ref,
                             u_ref, w_ref, qg_ref, kdt_ref, qkc_ref, egl_ref, pad_ref, mask_ref,
                             *, n_seq, tn):
    seg = 8 + tn
    _gdn_fill_masks(mask_ref, rows=n_seq * tn, chunk=tn)
    cw = cw_ref[...]
    for s in range(n_seq):
        pad_ref[s * seg + 8 - (CONV_W - 1):s * seg + 8, :] = buf_ref[s]
        pad_ref[s * seg + 8:(s + 1) * seg, :] = x_ref[s * tn:(s + 1) * tn, :]
    pieces = []
    for s in range(n_seq):
        base = s * seg + 8 - (CONV_W - 1)
        acc = pad_ref[pl.ds(base, tn), :] * cw[0:1]
        for j in range(1, CONV_W):
            acc = acc + pad_ref[pl.ds(base + j, tn), :] * cw[j:j + 1]
        pieces.append(acc)
    _gdn_intra_all(_silu(jnp.concatenate(pieces, axis=0)), gates_ref[...], alog_ref[...], dtb_ref[...],
                   mask_ref, u_ref, w_ref, qg_ref, kdt_ref, qkc_ref, egl_ref, rows=n_seq * tn, chunk=tn)


def _intra_out(m, chunk, n_steps):
    width = N_HEADS * HEAD_DIM
    return [
        jax.ShapeDtypeStruct((m, width), F32),
        jax.ShapeDtypeStruct((m, width), BF16),
        jax.ShapeDtypeStruct((m, width), BF16),
        jax.ShapeDtypeStruct((N_HEADS, HEAD_DIM, m), BF16),
        jax.ShapeDtypeStruct((N_HEADS, m, chunk), BF16),
        jax.ShapeDtypeStruct((n_steps, 8, width), F32),
    ]


def _gdn_intra_prompt(proj, gates, conv_w, alog_row, dtb_row, *, batch, t):
    m = proj.shape[0]
    rows = GDN_ROWS
    nblk = t // rows
    width = N_HEADS * HEAD_DIM
    cwid = 3 * width

    def rb(b, i):
        return b * nblk + i

    in_specs = [
        pl.BlockSpec((rows, cwid), lambda b, i: (rb(b, i), 1)),
        pl.BlockSpec((rows, HEAD_DIM), lambda b, i: (rb(b, i), 0)),
        pl.BlockSpec((CONV_W, cwid), lambda b, i: (0, 0)),
        pl.BlockSpec((1, HEAD_DIM), lambda b, i: (0, 0)),
        pl.BlockSpec((1, HEAD_DIM), lambda b, i: (0, 0)),
    ]
    out_specs = [
        pl.BlockSpec((rows, width), lambda b, i: (rb(b, i), 0)),
        pl.BlockSpec((rows, width), lambda b, i: (rb(b, i), 0)),
        pl.BlockSpec((rows, width), lambda b, i: (rb(b, i), 0)),
        pl.BlockSpec((N_HEADS, HEAD_DIM, rows), lambda b, i: (0, 0, rb(b, i))),
        pl.BlockSpec((N_HEADS, rows, CHUNK), lambda b, i: (0, rb(b, i), 0)),
        pl.BlockSpec((1, 8, width), lambda b, i: (rb(b, i), 0, 0)),
    ]
    return pl.pallas_call(
        functools.partial(_gdn_intra_prompt_kernel, rows=rows, chunk=CHUNK),
        grid=(batch, nblk),
        in_specs=in_specs,
        out_specs=out_specs,
        out_shape=_intra_out(m, CHUNK, batch * nblk),
        scratch_shapes=[pltpu.VMEM((rows + 8, cwid), F32),
                        pltpu.VMEM((_gdn_n_masks(CHUNK), rows, rows), F32)],
        compiler_params=pltpu.CompilerParams(
            dimension_semantics=("parallel", "arbitrary"), vmem_limit_bytes=V7X_VMEM_LIMIT_BYTES),
        name="gdn_intra_prompt",
    )(proj, gates, conv_w, alog_row, dtb_row)


def _gdn_intra_sample(proj, gates, conv_w, alog_row, dtb_row, conv_state, *, n_seq, tn):
    m = proj.shape[0]
    width = N_HEADS * HEAD_DIM
    cwid = 3 * width
    in_specs = [
        pl.BlockSpec((m, cwid), lambda i: (0, 1)),
        pl.BlockSpec((m, HEAD_DIM), lambda i: (0, 0)),
        pl.BlockSpec((CONV_W, cwid), lambda i: (0, 0)),
        pl.BlockSpec((1, HEAD_DIM), lambda i: (0, 0)),
        pl.BlockSpec((1, HEAD_DIM), lambda i: (0, 0)),
        pl.BlockSpec((n_seq, CONV_W - 1, cwid), lambda i: (0, 0, 0)),
    ]
    out_specs = [
        pl.BlockSpec((m, width), lambda i: (0, 0)),
        pl.BlockSpec((m, width), lambda i: (0, 0)),
        pl.BlockSpec((m, width), lambda i: (0, 0)),
        pl.BlockSpec((N_HEADS, HEAD_DIM, m), lambda i: (0, 0, 0)),
        pl.BlockSpec((N_HEADS, m, tn), lambda i: (0, 0, 0)),
        pl.BlockSpec((1, 8, width), lambda i: (0, 0, 0)),
    ]
    return pl.pallas_call(
        functools.partial(_gdn_intra_sample_kernel, n_seq=n_seq, tn=tn),
        grid=(1,),
        in_specs=in_specs,
        out_specs=out_specs,
        out_shape=_intra_out(m, tn, 1),
        scratch_shapes=[pltpu.VMEM((n_seq * (8 + tn), cwid), F32),
                        pltpu.VMEM((_gdn_n_masks(tn), m, m), F32)],
        compiler_params=pltpu.CompilerParams(
            dimension_semantics=("arbitrary",), vmem_limit_bytes=V7X_VMEM_LIMIT_BYTES),
        name="gdn_intra_sample",
    )(proj, gates, conv_w, alog_row, dtb_row, conv_state)


def _gdn_scan_kernel(u_ref, w_ref, qg_ref, kdt_ref, qkc_ref, egl_ref, z_ref, s0_ref, dnw_ref,
                     o_ref, s_ref, *, chunk, n_chunks, seqs_per_step):
    t = pl.program_id(1)

    @pl.when(t == 0)
    def _():
        s_ref[...] = s0_ref[...]

    dnw = dnw_ref[...]
    for c in range(n_chunks):
        si = c if seqs_per_step > 1 else 0
        rows = slice(c * chunk, (c + 1) * chunk)
        heads = range(N_HEADS)
        hcols = [slice(h * HEAD_DIM, (h + 1) * HEAD_DIM) for h in heads]
        s_old = [s_ref[si, h] for h in heads]
        both = [_dot(jnp.concatenate([w_ref[rows, hcols[h]], qg_ref[rows, hcols[h]]], axis=0),
                     s_old[h].astype(BF16)) for h in heads]
        vn = [(u_ref[rows, hcols[h]] - both[h][:chunk]).astype(BF16) for h in heads]
        o = [both[h][chunk:] + _dot(qkc_ref[h, rows, :], vn[h]) for h in heads]
        for h in heads:
            s_ref[si, h] = s_old[h] * egl_ref[0, c:c + 1, hcols[h]] + _dot(kdt_ref[h, :, rows], vn[h])
        for h in heads:
            o_ref[rows, hcols[h]] = (_rms(o[h], dnw) * _silu(z_ref[rows, hcols[h]])).astype(o_ref.dtype)


def _gdn_scan(intra, z_src, z_col_block, s0, dnw, *, n_seq, t, chunk, chunks_per_step, seqs_per_step=1):
    u, w, qg, kdt, qkc, egl = intra
    m = u.shape[0]
    width = N_HEADS * HEAD_DIM
    rows = chunk * chunks_per_step
    if seqs_per_step > 1:
        assert seqs_per_step == n_seq == chunks_per_step and t == chunk
        nsteps, n_outer = 1, 1
    else:
        nsteps, n_outer = t // rows, n_seq

    def rb(b, i):
        return b * nsteps + i

    in_specs = [
        pl.BlockSpec((rows, width), lambda b, i: (rb(b, i), 0)),
        pl.BlockSpec((rows, width), lambda b, i: (rb(b, i), 0)),
        pl.BlockSpec((rows, width), lambda b, i: (rb(b, i), 0)),
        pl.BlockSpec((N_HEADS, HEAD_DIM, rows), lambda b, i: (0, 0, rb(b, i))),
        pl.BlockSpec((N_HEADS, rows, chunk), lambda b, i: (0, rb(b, i), 0)),
        pl.BlockSpec((1, 8, width), lambda b, i: (rb(b, i), 0, 0)),
        pl.BlockSpec((rows, width), lambda b, i: (rb(b, i), z_col_block)),
        pl.BlockSpec((seqs_per_step, N_HEADS, HEAD_DIM, HEAD_DIM), lambda b, i: (b, 0, 0, 0)),
        pl.BlockSpec((1, HEAD_DIM), lambda b, i: (0, 0)),
    ]
    out_specs = [
        pl.BlockSpec((rows, width), lambda b, i: (rb(b, i), 0)),
        pl.BlockSpec((seqs_per_step, N_HEADS, HEAD_DIM, HEAD_DIM), lambda b, i: (b, 0, 0, 0)),
    ]
    return pl.pallas_call(
        functools.partial(_gdn_scan_kernel, chunk=chunk, n_chunks=chunks_per_step,
                          seqs_per_step=seqs_per_step),
        grid=(n_outer, nsteps),
        in_specs=in_specs,
        out_specs=out_specs,
        out_shape=[jax.ShapeDtypeStruct((m, width), BF16),
                   jax.ShapeDtypeStruct((n_seq, N_HEADS, HEAD_DIM, HEAD_DIM), F32)],
        compiler_params=pltpu.CompilerParams(
            dimension_semantics=("parallel", "arbitrary"), vmem_limit_bytes=V7X_VMEM_LIMIT_BYTES),
        name="gdn_scan",
    )(u, w, qg, kdt, qkc, egl, z_src, s0, dnw)


def _pad_lanes(v, offset):
    out = jnp.zeros((1, HEAD_DIM), F32)
    return lax.dynamic_update_slice(out, v.reshape(1, -1).astype(F32), (0, offset))


def kernel(x_prompt, x_sample, cache_k, cache_v, state_ssm, state_conv, page_table, ffn1_pre_w, ffn1_post_w, ffn1_gate, ffn1_up, ffn1_down, mix_pre_w, mix_post_w, w_in, conv_w, a_log, dt_bias, delta_norm_w, lambda_q1, lambda_k1, lambda_q2, lambda_k2, subln_w, w_out, ffn2_pre_w, ffn2_post_w, ffn2_gate, ffn2_up, ffn2_down):
    depth = w_in.shape[0]
    batch, t, d = x_prompt.shape
    n_seq, tn, _ = x_sample.shape
    a_w = N_HEADS * HEAD_DIM
    n_main = 3 * a_w + 3 * a_w + a_w
    yp = x_prompt.reshape(batch * t, d)
    ys = x_sample.reshape(n_seq * tn, d)
    outs = [[] for _ in range(8)]
    for l in range(depth):
        lam_init = 0.8 - 0.6 * math.exp(-0.3 * l)
        lam_rows = jnp.stack([lambda_q1[l], lambda_k1[l], lambda_q2[l], lambda_k2[l]]).astype(F32)
        row = lambda v: v[l].reshape(1, -1).astype(F32)
        w_gates = jnp.pad(w_in[l][:, n_main:], ((0, 0), (0, HEAD_DIM - 2 * N_HEADS))).astype(BF16)
        alog_row = _pad_lanes(a_log[l], N_HEADS)
        dtb_row = _pad_lanes(dt_bias[l], N_HEADS)
        dnw = row(delta_norm_w)
        subw = row(subln_w)
        cw = conv_w[l].astype(F32)
        ms = n_seq * tn

        ys, wg1, wu1, wd1 = _ffn(ys, row(ffn1_pre_w), row(ffn1_post_w), ffn1_gate[l], ffn1_up[l],
                                 ffn1_down[l], tm=ms, tf=512, emit_bf16=True)
        yp = _ffn(yp, row(ffn1_pre_w), row(ffn1_post_w), wg1, wu1, wd1, tm=512, tf=512)

        ps, gs, ks, vs, w_main = _inproj(ys, row(mix_pre_w), w_in, w_gates, tm=ms, tn=a_w, n=n_main,
                                         layer=l, emit_bf16=True)
        qs = ps[:, :a_w].reshape(n_seq, tn * N_HEADS, HEAD_DIM)
        ks = ks.reshape(n_seq, tn * N_HEADS, HEAD_DIM)
        vs = vs.reshape(n_seq, tn * N_HEADS, HEAD_DIM)
        oa_s = _attn_sample(page_table, lam_rows, subw, qs, ks, vs, cache_k[l], cache_v[l],
                            lam_init=lam_init, group=8)
        oa_s = oa_s.reshape(ms, a_w)
        intra_s = _gdn_intra_sample(ps, gs, cw, alog_row, dtb_row, state_conv[l], n_seq=n_seq, tn=tn)
        ob_s, s_s = _gdn_scan(intra_s, ps, 6, state_ssm[l].astype(F32), dnw, n_seq=n_seq, t=tn,
                              chunk=tn, chunks_per_step=n_seq, seqs_per_step=n_seq)
        ys, wo = _outproj(ys, oa_s, ob_s, w_out[l], row(mix_post_w), tm=ms, emit_bf16=True)

        pp, gp, kp, vp = _inproj(yp, row(mix_pre_w), w_main, w_gates, tm=512, tn=a_w, n=n_main)
        oa_p = _attn_prompt(lam_rows, subw.reshape(HEAD_DIM, 1), pp, kp, vp, batch=batch, t=t,
                            lam_init=lam_init, tq=256, hpp=2)
        intra_p = _gdn_intra_prompt(pp, gp, cw, alog_row, dtb_row, batch=batch, t=t)
        zero_s = jnp.zeros((batch, N_HEADS, HEAD_DIM, HEAD_DIM), F32)
        ob_p, s_p = _gdn_scan(intra_p, pp, 6, zero_s, dnw, n_seq=batch, t=t, chunk=CHUNK,
                              chunks_per_step=GDN_ROWS // CHUNK)
        yp = _outproj(yp, oa_p, ob_p, wo, row(mix_post_w), tm=512)

        ys, wg2, wu2, wd2 = _ffn(ys, row(ffn2_pre_w), row(ffn2_post_w), ffn2_gate[l], ffn2_up[l],
                                 ffn2_down[l], tm=ms, tf=512, emit_bf16=True)
        yp = _ffn(yp, row(ffn2_pre_w), row(ffn2_post_w), wg2, wu2, wd2, tm=512, tf=512)

        ppb = pp.reshape(batch, t, -1)
        psb = ps.reshape(n_seq, tn, -1)
        outs[0].append(kp.reshape(batch, t, N_HEADS, HEAD_DIM))
        outs[1].append(vp.reshape(batch, t, N_HEADS, HEAD_DIM))
        outs[2].append(s_p.astype(state_ssm.dtype))
        outs[3].append(ppb[:, t - (CONV_W - 1):, 3 * a_w:6 * a_w])
        outs[4].append(ks.reshape(n_seq, tn, N_HEADS, HEAD_DIM))
        outs[5].append(vs.reshape(n_seq, tn, N_HEADS, HEAD_DIM))
        outs[6].append(s_s.astype(state_ssm.dtype))
        outs[7].append(jnp.concatenate([state_conv[l].astype(psb.dtype), psb[:, :, 3 * a_w:6 * a_w]],
                                       axis=1)[:, tn:])
    return (yp.reshape(batch, t, d), ys.reshape(n_seq, tn, d)) + tuple(jnp.stack(o) for o in outs)
```

```python
import functools
import math

import jax
import jax.numpy as jnp
from jax import lax
from jax.experimental import pallas as pl
from jax.experimental.pallas import tpu as pltpu

F32 = jnp.float32
BF16 = jnp.bfloat16
EPS = 1e-6
NEG_INF = float("-inf")
LOG2E = 1.4426950408889634

V7X_VMEM_LIMIT_BYTES = 60 * 1024 * 1024

QK_DIM = 64
HEAD_DIM = 128
N_HEADS = 8
CONV_W = 4
CHUNK = 64
GDN_ROWS = 256

def _dot(a, b):
    return jnp.dot(a, b, preferred_element_type=F32)


def _dot_nt(a, b):
    return lax.dot_general(a, b, (((1,), (1,)), ((), ())), preferred_element_type=F32)


def _split3(x):
    hi = x.astype(BF16)
    r = x - hi.astype(F32)
    mid = r.astype(BF16)
    lo = (r - mid.astype(F32)).astype(BF16)
    return hi, mid, lo


def _dot_sel_l(sel, x):
    hi, mid, lo = _split3(x)
    return _dot(sel, hi) + _dot(sel, mid) + _dot(sel, lo)


def _dot_sel_r(x, sel):
    hi, mid, lo = _split3(x)
    return _dot(hi, sel) + _dot(mid, sel) + _dot(lo, sel)


def _dot_hp(a, b):
    ah = a.astype(BF16)
    al = (a - ah.astype(F32)).astype(BF16)
    bh = b.astype(BF16)
    bl = (b - bh.astype(F32)).astype(BF16)
    return _dot(ah, bh) + _dot(ah, bl) + _dot(al, bh)


def _rms(x, w):
    return x * lax.rsqrt(jnp.mean(x * x, axis=-1, keepdims=True) + EPS) * w


def _silu(x):
    return x * jax.nn.sigmoid(x)


def _softplus(x):
    return jnp.maximum(x, 0.0) + jnp.log1p(jnp.exp(-jnp.abs(x)))


def _lambda_full(lam_ref, lam_init):
    l = lam_ref[...]
    a = jnp.sum(l[0:1] * l[1:2], axis=-1, keepdims=True)
    b = jnp.sum(l[2:3] * l[3:4], axis=-1, keepdims=True)
    return jnp.exp(a) - jnp.exp(b) + lam_init


def _idiv(x, n):
    assert n & (n - 1) == 0
    return x >> (n.bit_length() - 1)


def _imod(x, n):
    assert n & (n - 1) == 0
    return x & (n - 1)


def _pow2_neg(k):
    return lax.bitcast_convert_type((127 - k) << 23, F32)


def _ffn_kernel(x_ref, prew_ref, postw_ref, wg_ref, wu_ref, wd_ref, o_ref, *rest, nf, emit_bf16):
    h_ref = rest[-1]
    f = pl.program_id(1)

    @pl.when(f == 0)
    def _():
        h_ref[...] = _rms(x_ref[...], prew_ref[...]).astype(BF16)
        o_ref[...] = jnp.zeros_like(o_ref)

    wg, wu, wd = wg_ref[...], wu_ref[...], wd_ref[...]
    if emit_bf16:
        wg, wu, wd = wg.astype(BF16), wu.astype(BF16), wd.astype(BF16)
        rest[0][...], rest[1][...], rest[2][...] = wg, wu, wd
    h = h_ref[...]
    g = _dot(h, wg)
    u = _dot(h, wu)
    o_ref[...] += _dot((_silu(g) * u).astype(BF16), wd)

    @pl.when(f == nf - 1)
    def _():
        o_ref[...] = x_ref[...] + 0.5 * _rms(o_ref[...], postw_ref[...])


def _ffn(x, pre_w, post_w, wg, wu, wd, *, tm, tf, emit_bf16=False):
    m, d = x.shape
    ff = wg.shape[1]
    nf = ff // tf
    assert not emit_bf16 or m == tm
    gu_spec = pl.BlockSpec((d, tf), lambda i, f: (0, f))
    dn_spec = pl.BlockSpec((tf, d), lambda i, f: (f, 0))
    out_specs = [pl.BlockSpec((tm, d), lambda i, f: (i, 0))]
    out_shape = [jax.ShapeDtypeStruct((m, d), F32)]
    if emit_bf16:
        out_specs += [gu_spec, gu_spec, dn_spec]
        out_shape += [jax.ShapeDtypeStruct(w.shape, BF16) for w in (wg, wu, wd)]
    outs = pl.pallas_call(
        functools.partial(_ffn_kernel, nf=nf, emit_bf16=emit_bf16),
        grid=(m // tm, nf),
        in_specs=[
            pl.BlockSpec((tm, d), lambda i, f: (i, 0)),
            pl.BlockSpec((1, d), lambda i, f: (0, 0)),
            pl.BlockSpec((1, d), lambda i, f: (0, 0)),
            gu_spec, gu_spec, dn_spec,
        ],
        out_specs=out_specs,
        out_shape=out_shape,
        scratch_shapes=[pltpu.VMEM((tm, d), BF16)],
        compiler_params=pltpu.CompilerParams(
            dimension_semantics=("parallel", "arbitrary"), vmem_limit_bytes=V7X_VMEM_LIMIT_BYTES),
        name="ffn",
    )(x, pre_w, post_w, wg, wu, wd)
    return outs if emit_bf16 else outs[0]


def _inproj_kernel(x_ref, prew_ref, wt_ref, wgt_ref, o_ref, og_ref, ko_ref, vo_ref, *rest, emit_bf16):
    h_ref = rest[-1]
    j = pl.program_id(1)

    @pl.when(j == 0)
    def _():
        h = _rms(x_ref[...], prew_ref[...]).astype(BF16)
        h_ref[...] = h
        og_ref[...] = _dot_nt(h, wgt_ref[...])

    wt = wt_ref[...]
    if emit_bf16:
        wt = wt.astype(BF16)
        rest[0][...] = wt
    y = _dot_nt(h_ref[...], wt)
    o_ref[...] = y

    @pl.when(j == 1)
    def _():
        ko_ref[...] = y

    @pl.when(j == 2)
    def _():
        vo_ref[...] = y


def _inproj(x, pre_w, w_main_t, w_gates_t, *, tm, tn, n, layer=None, emit_bf16=False):
    m, d = x.shape
    ng = w_gates_t.shape[0]
    assert not emit_bf16 or m == tm
    w_spec = pl.BlockSpec((tn, d), lambda i, j: (j, 0))
    w_in_spec = w_spec if layer is None else pl.BlockSpec((None, tn, d), lambda i, j: (layer, j, 0))
    row_spec = pl.BlockSpec((tm, tn), lambda i, j: (i, 0))
    out_specs = [pl.BlockSpec((tm, tn), lambda i, j: (i, j)), pl.BlockSpec((tm, ng), lambda i, j: (i, 0)),
                 row_spec, row_spec]
    out_shape = [jax.ShapeDtypeStruct((m, n), F32), jax.ShapeDtypeStruct((m, ng), F32),
                 jax.ShapeDtypeStruct((m, tn), F32), jax.ShapeDtypeStruct((m, tn), F32)]
    if emit_bf16:
        out_specs.append(w_spec)
        out_shape.append(jax.ShapeDtypeStruct((n, d), BF16))
    return pl.pallas_call(
        functools.partial(_inproj_kernel, emit_bf16=emit_bf16),
        grid=(m // tm, n // tn),
        in_specs=[
            pl.BlockSpec((tm, d), lambda i, j: (i, 0)),
            pl.BlockSpec((1, d), lambda i, j: (0, 0)),
            w_in_spec,
            pl.BlockSpec((ng, d), lambda i, j: (0, 0)),
        ],
        out_specs=out_specs,
        out_shape=out_shape,
        scratch_shapes=[pltpu.VMEM((tm, d), BF16)],
        compiler_params=pltpu.CompilerParams(
            dimension_semantics=("parallel", "arbitrary"), vmem_limit_bytes=V7X_VMEM_LIMIT_BYTES),
        name="inproj",
    )(x, pre_w, w_main_t, w_gates_t)


def _outproj_kernel(x_ref, oa_ref, ob_ref, wa_ref, wb_ref, postw_ref, o_ref, *rest, emit_bf16):
    wa, wb = wa_ref[...], wb_ref[...]
    if emit_bf16:
        wa, wb = wa.astype(BF16), wb.astype(BF16)
        ka = wa.shape[0]
        rest[0][:ka, :] = wa
        rest[0][ka:, :] = wb
    y = _dot(oa_ref[...], wa) + _dot(ob_ref[...], wb)
    o_ref[...] = x_ref[...] + _rms(y, postw_ref[...])


def _outproj(x, oa, ob, w_out, post_w, *, tm, emit_bf16=False):
    m, d = x.shape
    ka = oa.shape[1]
    assert not emit_bf16 or m == tm
    out_specs = [pl.BlockSpec((tm, d), lambda i: (i, 0))]
    out_shape = [jax.ShapeDtypeStruct((m, d), F32)]
    if emit_bf16:
        out_specs.append(pl.BlockSpec(w_out.shape, lambda i: (0, 0)))
        out_shape.append(jax.ShapeDtypeStruct(w_out.shape, BF16))
    outs = pl.pallas_call(
        functools.partial(_outproj_kernel, emit_bf16=emit_bf16),
        grid=(m // tm,),
        in_specs=[
            pl.BlockSpec((tm, d), lambda i: (i, 0)),
            pl.BlockSpec((tm, ka), lambda i: (i, 0)),
            pl.BlockSpec((tm, ka), lambda i: (i, 0)),
            pl.BlockSpec((ka, d), lambda i: (0, 0)),
            pl.BlockSpec((ka, d), lambda i: (1, 0)),
            pl.BlockSpec((1, d), lambda i: (0, 0)),
        ],
        out_specs=out_specs,
        out_shape=out_shape,
        compiler_params=pltpu.CompilerParams(
            dimension_semantics=("parallel",), vmem_limit_bytes=V7X_VMEM_LIMIT_BYTES),
        name="outproj",
    )(x, oa, ob, w_out, w_out, post_w)
    return outs if emit_bf16 else outs[0]


def _attn_prompt_kernel(lam_ref, subw_ref, q_ref, k_ref, v_ref, o_ref,
                        kb_ref, vt_ref, bias_ref, *, t, tq, hpp, lam_init):
    hg = pl.program_id(1)
    nq = t // tq
    lam = _lambda_full(lam_ref, lam_init)
    lane = lax.broadcasted_iota(jnp.int32, (1, HEAD_DIM), 1)
    first_map = lane < QK_DIM
    kk = lax.broadcasted_iota(jnp.int32, (tq, tq), 0)
    qq = lax.broadcasted_iota(jnp.int32, (tq, tq), 1)
    slopes = [_pow2_neg(jnp.full((1, 1), hg * hpp + hh + 1, jnp.int32)) * LOG2E for hh in range(hpp)]
    er = lax.broadcasted_iota(jnp.int32, (HEAD_DIM, HEAD_DIM), 0)
    eye_b = (er == lax.broadcasted_iota(jnp.int32, (HEAD_DIM, HEAD_DIM), 1)).astype(BF16)
    for hh in range(hpp):
        cols = slice(hh * HEAD_DIM, (hh + 1) * HEAD_DIM)
        b = slopes[hh] * (kk - qq).astype(F32)
        bias_ref[hh, 0] = b
        bias_ref[hh, 1] = jnp.where(qq >= kk, b, NEG_INF)
        vt_ref[hh] = _dot_nt(eye_b, v_ref[:, cols].astype(BF16)).astype(BF16)
    kb_ref[...] = k_ref[...].astype(BF16)
    scale = QK_DIM ** -0.5 * LOG2E
    subw_col = subw_ref[...]

    def q_block(i, _):
        rows_i = pl.ds(pl.multiple_of(i * tq, tq), tq)
        qs = []
        for hh in range(hpp):
            q = q_ref[rows_i, hh * HEAD_DIM:(hh + 1) * HEAD_DIM] * scale
            qs.append(jnp.concatenate(
                [jnp.where(first_map, q, 0.0), jnp.where(first_map, 0.0, q)], axis=0).astype(BF16))

        def scores(j):
            rows_j = pl.ds(pl.multiple_of(j * tq, tq), tq)
            diag = jnp.asarray(j == i, jnp.int32)
            out = []
            for hh in range(hpp):
                b = bias_ref[hh, diag]
                out.append(_dot_nt(kb_ref[rows_j, hh * HEAD_DIM:(hh + 1) * HEAD_DIM], qs[hh])
                           + jnp.concatenate([b, b], axis=1))
            return tuple(out)

        def consume(j, sts, state):
            rows_j = pl.ds(pl.multiple_of(j * tq, tq), tq)
            blocks_away = jnp.asarray((i - j) * tq, F32)
            hs = range(hpp)
            far = [slopes[hh] * blocks_away for hh in hs]
            m_new = [jnp.maximum(state[hh][0], jnp.max(sts[hh], axis=0, keepdims=True) - far[hh])
                     for hh in hs]
            p = [jnp.exp2(sts[hh] - (far[hh] + m_new[hh])) for hh in hs]
            alpha = [jnp.exp2(state[hh][0] - m_new[hh]) for hh in hs]
            l_new = [alpha[hh] * state[hh][1] + jnp.sum(p[hh], axis=0, keepdims=True) for hh in hs]
            pv = [_dot(vt_ref[hh, :, rows_j], p[hh].astype(BF16)) for hh in hs]
            return tuple((m_new[hh], l_new[hh], alpha[hh] * state[hh][2] + pv[hh]) for hh in hs)

        def step(j, carry):
            sts, state = carry
            nxt = scores(j + 1)
            return nxt, consume(j, sts, state)

        init = tuple((jnp.full((1, 2 * tq), NEG_INF, F32), jnp.zeros((1, 2 * tq), F32),
                      jnp.zeros((HEAD_DIM, 2 * tq), F32)) for _ in range(hpp))
        sts, state = lax.fori_loop(0, i, step, (scores(0), init))
        state = consume(i, sts, state)
        for hh in range(hpp):
            _, l_fin, acc_fin = state[hh]
            ot = acc_fin / l_fin
            dt = ot[:, :tq] - lam * ot[:, tq:]
            dt = dt * lax.rsqrt(jnp.mean(dt * dt, axis=0, keepdims=True) + EPS) * subw_col
            o_ref[rows_i, hh * HEAD_DIM:(hh + 1) * HEAD_DIM] = (
                (dt * (1.0 - lam_init)).T.astype(o_ref.dtype))
        return 0

    lax.fori_loop(0, nq, q_block, 0)


def _attn_prompt(lam_rows, subln_col, proj, k, v, *, batch, t, lam_init, tq, hpp):
    m = proj.shape[0]
    ng = N_HEADS // hpp
    w = hpp * HEAD_DIM
    return pl.pallas_call(
        functools.partial(_attn_prompt_kernel, t=t, tq=tq, hpp=hpp, lam_init=lam_init),
        grid=(batch, ng),
        in_specs=[
            pl.BlockSpec((4, QK_DIM), lambda b, g: (0, 0)),
            pl.BlockSpec((HEAD_DIM, 1), lambda b, g: (0, 0)),
            pl.BlockSpec((t, w), lambda b, g: (b, g)),
            pl.BlockSpec((t, w), lambda b, g: (b, g)),
            pl.BlockSpec((t, w), lambda b, g: (b, g)),
        ],
        out_specs=pl.BlockSpec((t, w), lambda b, g: (b, g)),
        out_shape=jax.ShapeDtypeStruct((m, N_HEADS * HEAD_DIM), BF16),
        scratch_shapes=[
            pltpu.VMEM((t, w), BF16),
            pltpu.VMEM((hpp, HEAD_DIM, t), BF16),
            pltpu.VMEM((hpp, 2, tq, tq), F32),
        ],
        compiler_params=pltpu.CompilerParams(
            dimension_semantics=("parallel", "parallel"), vmem_limit_bytes=V7X_VMEM_LIMIT_BYTES),
        name="attn_prompt",
    )(lam_rows, subln_col, proj, k, v)


def _attn_sample_kernel(pt_ref, lam_ref, subw_ref, q_ref, kn_ref, vn_ref, *rest,
                        n_pages, page, tn, group, lam_init):
    del pt_ref
    kc_refs = rest[:group]
    vc_refs = rest[group:2 * group]
    o_ref, qb_ref, bias_ref, m_ref, l_ref, acc_ref = rest[2 * group:]
    p = pl.program_id(1)
    nrow = 2 * tn * N_HEADS
    ncol = page * N_HEADS
    past = n_pages * page
    row = lax.broadcasted_iota(jnp.int32, (nrow, 1), 0)
    slope = _pow2_neg(_imod(row, N_HEADS) + 1)

    @pl.when(p == 0)
    def _():
        q = q_ref[0] * (QK_DIM ** -0.5)
        lane = lax.broadcasted_iota(jnp.int32, (1, HEAD_DIM), 1)
        first_map = lane < QK_DIM
        qb_ref[...] = jnp.concatenate(
            [jnp.where(first_map, q, 0.0), jnp.where(first_map, 0.0, q)], axis=0).astype(BF16)
        rr = lax.broadcasted_iota(jnp.int32, (nrow, ncol), 0)
        cc = lax.broadcasted_iota(jnp.int32, (nrow, ncol), 1)
        tq_ = _idiv(_imod(rr, tn * N_HEADS), N_HEADS)
        same_head = _imod(rr, N_HEADS) == _imod(cc, N_HEADS)
        bias_ref[...] = jnp.where(same_head, slope * (_idiv(cc, N_HEADS) - tq_).astype(F32), NEG_INF)
        m_ref[...] = jnp.full(m_ref.shape, NEG_INF, F32)
        l_ref[...] = jnp.zeros(l_ref.shape, F32)
        acc_ref[...] = jnp.zeros(acc_ref.shape, F32)

    def lane_fold(x, op):
        out = x[:, :HEAD_DIM]
        for i in range(1, x.shape[1] // HEAD_DIM):
            out = op(out, x[:, i * HEAD_DIM:(i + 1) * HEAD_DIM])
        return out

    def update(scores, offs, vals):
        m_old = m_ref[...]
        m_loc = lane_fold(scores[0], jnp.maximum) - offs[0]
        for s, off in zip(scores[1:], offs[1:]):
            m_loc = jnp.maximum(m_loc, lane_fold(s, jnp.maximum) - off)
        m_new = jnp.maximum(m_old, jnp.max(m_loc, axis=-1, keepdims=True))
        alpha = jnp.exp(m_old - m_new)
        pes = [jnp.exp(s - (off + m_new)) for s, off in zip(scores, offs)]
        l_part = lane_fold(pes[0], jnp.add)
        for pe in pes[1:]:
            l_part = l_part + lane_fold(pe, jnp.add)
        pv = _dot(jnp.concatenate([pe.astype(BF16) for pe in pes], axis=1),
                  jnp.concatenate(vals, axis=0))
        l_ref[...] = alpha * l_ref[...] + l_part
        acc_ref[...] = alpha * acc_ref[...] + pv
        m_ref[...] = m_new

    qb = qb_ref[...]
    bias = bias_ref[...]
    scores, offs, vals = [], [], []
    for g in range(group):
        kf = kc_refs[g][0].reshape(ncol, HEAD_DIM).astype(BF16)
        vals.append(vc_refs[g][0].reshape(ncol, HEAD_DIM).astype(BF16))
        scores.append(_dot_nt(qb, kf) + bias)
        offs.append(slope * jnp.asarray(past - (p * group + g) * page, F32))
    update(scores, offs, vals)

    @pl.when(p == n_pages // group - 1)
    def _():
        ncn = tn * N_HEADS
        rr = lax.broadcasted_iota(jnp.int32, (nrow, HEAD_DIM), 0)
        cc = lax.broadcasted_iota(jnp.int32, (nrow, HEAD_DIM), 1)
        tq_ = _idiv(_imod(rr, ncn), N_HEADS)
        tk_ = _idiv(cc, N_HEADS)
        ok = (cc < ncn) & (_imod(rr, N_HEADS) == _imod(cc, N_HEADS)) & (tk_ <= tq_)
        bias_new = jnp.where(ok, slope * (tk_ - tq_).astype(F32), NEG_INF)
        pad = jnp.zeros((HEAD_DIM - ncn, HEAD_DIM), BF16)
        kn = jnp.concatenate([kn_ref[0].astype(BF16), pad], axis=0)
        vn = jnp.concatenate([vn_ref[0].astype(BF16), pad], axis=0)
        update([_dot_nt(qb, kn) + bias_new], [jnp.zeros((nrow, 1), F32)], [vn])
        lam = _lambda_full(lam_ref, lam_init)
        o = acc_ref[...] / jnp.sum(l_ref[...], axis=-1, keepdims=True)
        d = o[:ncn] - lam * o[ncn:]
        o_ref[0] = (_rms(d, subw_ref[...]) * (1.0 - lam_init)).astype(o_ref.dtype)


def _attn_sample(page_table, lam_rows, subln_w, q, k_new, v_new, cache_k, cache_v, *, lam_init, group):
    n_seq, n_pages = page_table.shape
    page = cache_k.shape[1]
    tn = q.shape[1] // N_HEADS
    nrow = 2 * tn * N_HEADS
    ncol = page * N_HEADS
    assert n_pages % group == 0

    def page_spec(g):
        return pl.BlockSpec((1, page, N_HEADS, HEAD_DIM),
                            lambda b, p, pt: (pt[b, p * group + g], 0, 0, 0))

    new_spec = pl.BlockSpec((1, tn * N_HEADS, HEAD_DIM), lambda b, p, pt: (b, 0, 0))
    grid_spec = pltpu.PrefetchScalarGridSpec(
        num_scalar_prefetch=1,
        grid=(n_seq, n_pages // group),
        in_specs=[
            pl.BlockSpec((4, QK_DIM), lambda b, p, pt: (0, 0)),
            pl.BlockSpec((1, HEAD_DIM), lambda b, p, pt: (0, 0)),
            new_spec, new_spec, new_spec,
        ] + [page_spec(g) for g in range(group)] * 2,
        out_specs=new_spec,
        scratch_shapes=[
            pltpu.VMEM((nrow, HEAD_DIM), BF16),
            pltpu.VMEM((nrow, ncol), F32),
            pltpu.VMEM((nrow, 1), F32),
            pltpu.VMEM((nrow, HEAD_DIM), F32),
            pltpu.VMEM((nrow, HEAD_DIM), F32),
        ],
    )
    return pl.pallas_call(
        functools.partial(_attn_sample_kernel, n_pages=n_pages, page=page, tn=tn, group=group,
                          lam_init=lam_init),
        grid_spec=grid_spec,
        out_shape=jax.ShapeDtypeStruct((n_seq, tn * N_HEADS, HEAD_DIM), BF16),
        compiler_params=pltpu.CompilerParams(
            dimension_semantics=("parallel", "arbitrary"), vmem_limit_bytes=V7X_VMEM_LIMIT_BYTES),
        name="attn_sample",
    )(page_table, lam_rows, subln_w, q, k_new, v_new, *([cache_k] * group), *([cache_v] * group))


def _gdn_n_masks(chunk):
    return 3 + (chunk.bit_length() - 1)


def _gdn_fill_masks(mask_ref, *, rows, chunk):
    ri = lax.broadcasted_iota(jnp.int32, (rows, rows), 0)
    ci = lax.broadcasted_iota(jnp.int32, (rows, rows), 1)
    same = _idiv(ri, chunk) == _idiv(ci, chunk)
    mask_ref[0] = (same & (ri >= ci)).astype(F32)
    mask_ref[1] = (same & (ri > ci)).astype(F32)
    mask_ref[2] = (ri == ci).astype(F32)
    s, k = 1, 3
    while s < chunk:
        mask_ref[k] = ((_idiv(ri, 2 * s) == _idiv(ci, 2 * s)) & (_imod(_idiv(ri, s), 2) == 1)
                       & (_imod(_idiv(ci, s), 2) == 0)).astype(F32)
        s, k = 2 * s, k + 1


def _gdn_intra_all(act, gates, alog_row, dtb_row, mask_ref, u_ref, w_ref, qg_ref, kdt_ref, qkc_ref,
                   egl_ref, *, rows, chunk):
    n_chunks = rows // chunk
    n_levels = _gdn_n_masks(chunk) - 3
    beta_all = jax.nn.sigmoid(gates)
    g_all = -jnp.exp(alog_row) * _softplus(gates + dtb_row)
    gcum_all = _dot_sel_l(mask_ref[0].astype(BF16), g_all)
    gtot_all = jnp.concatenate(
        [jnp.broadcast_to(gcum_all[(c + 1) * chunk - 1:(c + 1) * chunk], (chunk, HEAD_DIM))
         for c in range(n_chunks)], axis=0)
    gcum_t = gcum_all.T
    fr = lax.broadcasted_iota(jnp.int32, (rows, chunk), 0)
    fc = lax.broadcasted_iota(jnp.int32, (rows, chunk), 1)
    fold = (_imod(fr, chunk) == fc).astype(BF16)
    er = lax.broadcasted_iota(jnp.int32, (HEAD_DIM, HEAD_DIM), 0)
    eye_b = (er == lax.broadcasted_iota(jnp.int32, (HEAD_DIM, HEAD_DIM), 1)).astype(BF16)

    def lanes(col):
        return jnp.broadcast_to(col, (rows, HEAD_DIM))

    def square(x):
        if rows >= HEAD_DIM:
            return jnp.concatenate([x] * (rows // HEAD_DIM), axis=1)
        return x[:, :rows]

    heads = range(N_HEADS)
    egl_rows, lmats, lbs, tinvs, rhss = [], [], [], [], []
    for h in heads:
        cols = slice(h * HEAD_DIM, (h + 1) * HEAD_DIM)
        aq = act[:, h * HEAD_DIM:(h + 1) * HEAD_DIM]
        ak = act[:, (N_HEADS + h) * HEAD_DIM:(N_HEADS + h + 1) * HEAD_DIM]
        av = act[:, (2 * N_HEADS + h) * HEAD_DIM:(2 * N_HEADS + h + 1) * HEAD_DIM]
        qn = aq * lax.rsqrt(jnp.sum(aq * aq, axis=-1, keepdims=True) + 1e-6) * (HEAD_DIM ** -0.5)
        kn = ak * lax.rsqrt(jnp.sum(ak * ak, axis=-1, keepdims=True) + 1e-6)
        beta = lanes(beta_all[:, h:h + 1])
        g_cum = lanes(gcum_all[:, N_HEADS + h:N_HEADS + h + 1])
        g_tot = lanes(gtot_all[:, N_HEADS + h:N_HEADS + h + 1])
        g_row = gcum_t[N_HEADS + h:N_HEADS + h + 1, :]
        decay = jnp.exp(jnp.minimum(square(g_cum) - g_row, 0.0)) * mask_ref[0]

        kbeta = kn * beta
        knb = kn.astype(BF16)
        lmat = mask_ref[1] * (_dot_nt(kbeta.astype(BF16), knb) * decay)
        lmats.append(lmat)
        lbs.append(lmat.astype(BF16))
        tinvs.append(mask_ref[2] - lmat * mask_ref[3])
        e_cum = jnp.exp(g_cum)
        rhss.append(jnp.concatenate([av * beta, kbeta * e_cum], axis=1).astype(BF16))
        qk = (_dot_nt(qn.astype(BF16), knb) * decay).astype(BF16)
        qkc_ref[h] = _dot(qk, fold).astype(BF16)
        qg_ref[:, cols] = (qn * e_cum).astype(BF16)
        kd = (kn * jnp.exp(g_tot - g_cum)).astype(BF16)
        kdt_ref[h] = _dot_nt(eye_b, kd).astype(BF16)
        egl = jnp.exp(g_tot)
        egl_rows.append(jnp.concatenate(
            [egl[c * chunk:c * chunk + 1] for c in range(n_chunks)]
            + [jnp.zeros((8 - n_chunks, HEAD_DIM), F32)] * (n_chunks < 8), axis=0))
    egl_ref[0] = jnp.concatenate(egl_rows, axis=1)

    for k in range(1, n_levels):
        tbs = [tinvs[h].astype(BF16) for h in heads]
        mids = [_dot(tbs[h], lbs[h]).astype(BF16) for h in heads]
        tinvs = [tinvs[h] - mask_ref[3 + k] * _dot(mids[h], tbs[h]) for h in heads]
    resids = [mask_ref[2] - tinvs[h] - _dot_hp(lmats[h], tinvs[h]) for h in heads]
    tinvs = [tinvs[h] + _dot(tinvs[h].astype(BF16), resids[h].astype(BF16)) for h in heads]
    for h in heads:
        cols = slice(h * HEAD_DIM, (h + 1) * HEAD_DIM)
        uw = _dot(tinvs[h].astype(BF16), rhss[h])
        u_ref[:, cols] = uw[:, :HEAD_DIM]
        w_ref[:, cols] = uw[:, HEAD_DIM:].astype(BF16)


def _gdn_intra_prompt_kernel(x_ref, gates_ref, cw_ref, alog_ref, dtb_ref,
                             u_ref, w_ref, qg_ref, kdt_ref, qkc_ref, egl_ref, pad_ref, mask_ref,
                             *, rows, chunk):
    t = pl.program_id(1)

    @pl.when(t == 0)
    def _():
        pad_ref[0:8, :] = jnp.zeros((8, pad_ref.shape[1]), F32)
        _gdn_fill_masks(mask_ref, rows=rows, chunk=chunk)

    pad_ref[8:, :] = x_ref[...]
    cw = cw_ref[...]
    acc = pad_ref[pl.ds(8 - (CONV_W - 1), rows), :] * cw[0:1]
    for j in range(1, CONV_W):
        acc = acc + pad_ref[pl.ds(8 - (CONV_W - 1) + j, rows), :] * cw[j:j + 1]
    pad_ref[0:8, :] = x_ref[rows - 8:, :]
    _gdn_intra_all(_silu(acc), gates_ref[...], alog_ref[...], dtb_ref[...], mask_ref,
                   u_ref, w_ref, qg_ref, kdt_ref, qkc_ref, egl_ref, rows=rows, chunk=chunk)


def _gdn_intra_sample_kernel(x_ref, gates_ref, cw_ref, alog_ref, dtb_ref, buf_ref,
                             u_ref, w_ref, qg_ref, kdt_ref, qkc_ref, egl_ref, pad_ref, mask_ref,
                             *, n_seq, tn):
    seg = 8 + tn
    _gdn_fill_masks(mask_ref, rows=n_seq * tn, chunk=tn)
    cw = cw_ref[...]
    for s in range(n_seq):
        pad_ref[s * seg + 8 - (CONV_W - 1):s * seg + 8, :] = buf_ref[s]
        pad_ref[s * seg + 8:(s + 1) * seg, :] = x_ref[s * tn:(s + 1) * tn, :]
    pieces = []
    for s in range(n_seq):
        base = s * seg + 8 - (CONV_W - 1)
        acc = pad_ref[pl.ds(base, tn), :] * cw[0:1]
        for j in range(1, CONV_W):
            acc = acc + pad_ref[pl.ds(base + j, tn), :] * cw[j:j + 1]
        pieces.append(acc)
    _gdn_intra_all(_silu(jnp.concatenate(pieces, axis=0)), gates_ref[...], alog_ref[...], dtb_ref[...],
                   mask_ref, u_ref, w_ref, qg_ref, kdt_ref, qkc_ref, egl_ref, rows=n_seq * tn, chunk=tn)


def _intra_out(m, chunk, n_steps):
    width = N_HEADS * HEAD_DIM
    return [
        jax.ShapeDtypeStruct((m, width), F32),
        jax.ShapeDtypeStruct((m, width), BF16),
        jax.ShapeDtypeStruct((m, width), BF16),
        jax.ShapeDtypeStruct((N_HEADS, HEAD_DIM, m), BF16),
        jax.ShapeDtypeStruct((N_HEADS, m, chunk), BF16),
        jax.ShapeDtypeStruct((n_steps, 8, width), F32),
    ]


def _gdn_intra_prompt(proj, gates, conv_w, alog_row, dtb_row, *, batch, t):
    m = proj.shape[0]
    rows = GDN_ROWS
    nblk = t // rows
    width = N_HEADS * HEAD_DIM
    cwid = 3 * width

    def rb(b, i):
        return b * nblk + i

    in_specs = [
        pl.BlockSpec((rows, cwid), lambda b, i: (rb(b, i), 1)),
        pl.BlockSpec((rows, HEAD_DIM), lambda b, i: (rb(b, i), 0)),
        pl.BlockSpec((CONV_W, cwid), lambda b, i: (0, 0)),
        pl.BlockSpec((1, HEAD_DIM), lambda b, i: (0, 0)),
        pl.BlockSpec((1, HEAD_DIM), lambda b, i: (0, 0)),
    ]
    out_specs = [
        pl.BlockSpec((rows, width), lambda b, i: (rb(b, i), 0)),
        pl.BlockSpec((rows, width), lambda b, i: (rb(b, i), 0)),
        pl.BlockSpec((rows, width), lambda b, i: (rb(b, i), 0)),
        pl.BlockSpec((N_HEADS, HEAD_DIM, rows), lambda b, i: (0, 0, rb(b, i))),
        pl.BlockSpec((N_HEADS, rows, CHUNK), lambda b, i: (0, rb(b, i), 0)),
        pl.BlockSpec((1, 8, width), lambda b, i: (rb(b, i), 0, 0)),
    ]
    return pl.pallas_call(
        functools.partial(_gdn_intra_prompt_kernel, rows=rows, chunk=CHUNK),
        grid=(batch, nblk),
        in_specs=in_specs,
        out_specs=out_specs,
        out_shape=_intra_out(m, CHUNK, batch * nblk),
        scratch_shapes=[pltpu.VMEM((rows + 8, cwid), F32),
                        pltpu.VMEM((_gdn_n_masks(CHUNK), rows, rows), F32)],
        compiler_params=pltpu.CompilerParams(
            dimension_semantics=("parallel", "arbitrary"), vmem_limit_bytes=V7X_VMEM_LIMIT_BYTES),
        name="gdn_intra_prompt",
    )(proj, gates, conv_w, alog_row, dtb_row)


def _gdn_intra_sample(proj, gates, conv_w, alog_row, dtb_row, conv_state, *, n_seq, tn):
    m = proj.shape[0]
    width = N_HEADS * HEAD_DIM
    cwid = 3 * width
    in_specs = [
        pl.BlockSpec((m, cwid), lambda i: (0, 1)),
        pl.BlockSpec((m, HEAD_DIM), lambda i: (0, 0)),
        pl.BlockSpec((CONV_W, cwid), lambda i: (0, 0)),
        pl.BlockSpec((1, HEAD_DIM), lambda i: (0, 0)),
        pl.BlockSpec((1, HEAD_DIM), lambda i: (0, 0)),
        pl.BlockSpec((n_seq, CONV_W - 1, cwid), lambda i: (0, 0, 0)),
    ]
    out_specs = [
        pl.BlockSpec((m, width), lambda i: (0, 0)),
        pl.BlockSpec((m, width), lambda i: (0, 0)),
        pl.BlockSpec((m, width), lambda i: (0, 0)),
        pl.BlockSpec((N_HEADS, HEAD_DIM, m), lambda i: (0, 0, 0)),
        pl.BlockSpec((N_HEADS, m, tn), lambda i: (0, 0, 0)),
        pl.BlockSpec((1, 8, width), lambda i: (0, 0, 0)),
    ]
    return pl.pallas_call(
        functools.partial(_gdn_intra_sample_kernel, n_seq=n_seq, tn=tn),
        grid=(1,),
        in_specs=in_specs,
        out_specs=out_specs,
        out_shape=_intra_out(m, tn, 1),
        scratch_shapes=[pltpu.VMEM((n_seq * (8 + tn), cwid), F32),
                        pltpu.VMEM((_gdn_n_masks(tn), m, m), F32)],
        compiler_params=pltpu.CompilerParams(
            dimension_semantics=("arbitrary",), vmem_limit_bytes=V7X_VMEM_LIMIT_BYTES),
        name="gdn_intra_sample",
    )(proj, gates, conv_w, alog_row, dtb_row, conv_state)


def _gdn_scan_kernel(u_ref, w_ref, qg_ref, kdt_ref, qkc_ref, egl_ref, z_ref, s0_ref, dnw_ref,
                     o_ref, s_ref, *, chunk, n_chunks, seqs_per_step):
    t = pl.program_id(1)

    @pl.when(t == 0)
    def _():
        s_ref[...] = s0_ref[...]

    dnw = dnw_ref[...]
    for c in range(n_chunks):
        si = c if seqs_per_step > 1 else 0
        rows = slice(c * chunk, (c + 1) * chunk)
        heads = range(N_HEADS)
        hcols = [slice(h * HEAD_DIM, (h + 1) * HEAD_DIM) for h in heads]
        s_old = [s_ref[si, h] for h in heads]
        both = [_dot(jnp.concatenate([w_ref[rows, hcols[h]], qg_ref[rows, hcols[h]]], axis=0),
                     s_old[h].astype(BF16)) for h in heads]
        vn = [(u_ref[rows, hcols[h]] - both[h][:chunk]).astype(BF16) for h in heads]
        o = [both[h][chunk:] + _dot(qkc_ref[h, rows, :], vn[h]) for h in heads]
        for h in heads:
            s_ref[si, h] = s_old[h] * egl_ref[0, c:c + 1, hcols[h]] + _dot(kdt_ref[h, :, rows], vn[h])
        for h in heads:
            o_ref[rows, hcols[h]] = (_rms(o[h], dnw) * _silu(z_ref[rows, hcols[h]])).astype(o_ref.dtype)


def _gdn_scan(intra, z_src, z_col_block, s0, dnw, *, n_seq, t, chunk, chunks_per_step, seqs_per_step=1):
    u, w, qg, kdt, qkc, egl = intra
    m = u.shape[0]
    width = N_HEADS * HEAD_DIM
    rows = chunk * chunks_per_step
    if seqs_per_step > 1:
        assert seqs_per_step == n_seq == chunks_per_step and t == chunk
        nsteps, n_outer = 1, 1
    else:
        nsteps, n_outer = t // rows, n_seq

    def rb(b, i):
        return b * nsteps + i

    in_specs = [
        pl.BlockSpec((rows, width), lambda b, i: (rb(b, i), 0)),
        pl.BlockSpec((rows, width), lambda b, i: (rb(b, i), 0)),
        pl.BlockSpec((rows, width), lambda b, i: (rb(b, i), 0)),
        pl.BlockSpec((N_HEADS, HEAD_DIM, rows), lambda b, i: (0, 0, rb(b, i))),
        pl.BlockSpec((N_HEADS, rows, chunk), lambda b, i: (0, rb(b, i), 0)),
        pl.BlockSpec((1, 8, width), lambda b, i: (rb(b, i), 0, 0)),
        pl.BlockSpec((rows, width), lambda b, i: (rb(b, i), z_col_block)),
        pl.BlockSpec((seqs_per_step, N_HEADS, HEAD_DIM, HEAD_DIM), lambda b, i: (b, 0, 0, 0)),
        pl.BlockSpec((1, HEAD_DIM), lambda b, i: (0, 0)),
    ]
    out_specs = [
        pl.BlockSpec((rows, width), lambda b, i: (rb(b, i), 0)),
        pl.BlockSpec((seqs_per_step, N_HEADS, HEAD_DIM, HEAD_DIM), lambda b, i: (b, 0, 0, 0)),
    ]
    return pl.pallas_call(
        functools.partial(_gdn_scan_kernel, chunk=chunk, n_chunks=chunks_per_step,
                          seqs_per_step=seqs_per_step),
        grid=(n_outer, nsteps),
        in_specs=in_specs,
        out_specs=out_specs,
        out_shape=[jax.ShapeDtypeStruct((m, width), BF16),
                   jax.ShapeDtypeStruct((n_seq, N_HEADS, HEAD_DIM, HEAD_DIM), F32)],
        compiler_params=pltpu.CompilerParams(
            dimension_semantics=("parallel", "arbitrary"), vmem_limit_bytes=V7X_VMEM_LIMIT_BYTES),
        name="gdn_scan",
    )(u, w, qg, kdt, qkc, egl, z_src, s0, dnw)


def _pad_lanes(v, offset):
    out = jnp.zeros((1, HEAD_DIM), F32)
    return lax.dynamic_update_slice(out, v.reshape(1, -1).astype(F32), (0, offset))


def kernel(x_prompt, x_sample, cache_k, cache_v, state_ssm, state_conv, page_table, ffn1_pre_w, ffn1_post_w, ffn1_gate, ffn1_up, ffn1_down, mix_pre_w, mix_post_w, w_in, conv_w, a_log, dt_bias, delta_norm_w, lambda_q1, lambda_k1, lambda_q2, lambda_k2, subln_w, w_out, ffn2_pre_w, ffn2_post_w, ffn2_gate, ffn2_up, ffn2_down):
    depth = w_in.shape[0]
    batch, t, d = x_prompt.shape
    n_seq, tn, _ = x_sample.shape
    a_w = N_HEADS * HEAD_DIM
    n_main = 3 * a_w + 3 * a_w + a_w
    yp = x_prompt.reshape(batch * t, d)
    ys = x_sample.reshape(n_seq * tn, d)
    outs = [[] for _ in range(8)]
    w_in_t = jnp.swapaxes(w_in, 1, 2)
    for l in range(depth):
        lam_init = 0.8 - 0.6 * math.exp(-0.3 * l)
        lam_rows = jnp.stack([lambda_q1[l], lambda_k1[l], lambda_q2[l], lambda_k2[l]]).astype(F32)
        row = lambda v: v[l].reshape(1, -1).astype(F32)
        w_gates = jnp.pad(w_in_t[l, n_main:], ((0, HEAD_DIM - 2 * N_HEADS), (0, 0))).astype(BF16)
        alog_row = _pad_lanes(a_log[l], N_HEADS)
        dtb_row = _pad_lanes(dt_bias[l], N_HEADS)
        dnw = row(delta_norm_w)
        subw = row(subln_w)
        cw = conv_w[l].astype(F32)
        ms = n_seq * tn

        ys, wg1, wu1, wd1 = _ffn(ys, row(ffn1_pre_w), row(ffn1_post_w), ffn1_gate[l], ffn1_up[l],
                                 ffn1_down[l], tm=ms, tf=512, emit_bf16=True)
        yp = _ffn(yp, row(ffn1_pre_w), row(ffn1_post_w), wg1, wu1, wd1, tm=512, tf=512)

        ps, gs, ks, vs, w_main = _inproj(ys, row(mix_pre_w), w_in_t, w_gates, tm=ms, tn=a_w, n=n_main,
                                         layer=l, emit_bf16=True)
        qs = ps[:, :a_w].reshape(n_seq, tn * N_HEADS, HEAD_DIM)
        ks = ks.reshape(n_seq, tn * N_HEADS, HEAD_DIM)
        vs = vs.reshape(n_seq, tn * N_HEADS, HEAD_DIM)
        oa_s = _attn_sample(page_table, lam_rows, subw, qs, ks, vs, cache_k[l], cache_v[l],
                            lam_init=lam_init, group=8)
        oa_s = oa_s.reshape(ms, a_w)
        intra_s = _gdn_intra_sample(ps, gs, cw, alog_row, dtb_row, state_conv[l], n_seq=n_seq, tn=tn)
        ob_s, s_s = _gdn_scan(intra_s, ps, 6, state_ssm[l].astype(F32), dnw, n_seq=n_seq, t=tn,
                              chunk=tn, chunks_per_step=n_seq, seqs_per_step=n_seq)
        ys, wo = _outproj(ys, oa_s, ob_s, w_out[l], row(mix_post_w), tm=ms, emit_bf16=True)

        pp, gp, kp, vp = _inproj(yp, row(mix_pre_w), w_main, w_gates, tm=512, tn=a_w, n=n_main)
        oa_p = _attn_prompt(lam_rows, subw.reshape(HEAD_DIM, 1), pp, kp, vp, batch=batch, t=t,
                            lam_init=lam_init, tq=256, hpp=2)
        intra_p = _gdn_intra_prompt(pp, gp, cw, alog_row, dtb_row, batch=batch, t=t)
        zero_s = jnp.zeros((batch, N_HEADS, HEAD_DIM, HEAD_DIM), F32)
        ob_p, s_p = _gdn_scan(intra_p, pp, 6, zero_s, dnw, n_seq=batch, t=t, chunk=CHUNK,
                              chunks_per_step=GDN_ROWS // CHUNK)
        yp = _outproj(yp, oa_p, ob_p, wo, row(mix_post_w), tm=512)

        ys, wg2, wu2, wd2 = _ffn(ys, row(ffn2_pre_w), row(ffn2_post_w), ffn2_gate[l], ffn2_up[l],
                                 ffn2_down[l], tm=ms, tf=512, emit_bf16=True)
        yp = _ffn(yp, row(ffn2_pre_w), row(ffn2_post_w), wg2, wu2, wd2, tm=512, tf=512)

        ppb = pp.reshape(batch, t, -1)
        psb = ps.reshape(n_seq, tn, -1)
        outs[0].append(kp.reshape(batch, t, N_HEADS, HEAD_DIM))
        outs[1].append(vp.reshape(batch, t, N_HEADS, HEAD_DIM))
        outs[2].append(s_p.astype(state_ssm.dtype))
        outs[3].append(ppb[:, t - (CONV_W - 1):, 3 * a_w:6 * a_w])
        outs[4].append(ks.reshape(n_seq, tn, N_HEADS, HEAD_DIM))
        outs[5].append(vs.reshape(n_seq, tn, N_HEADS, HEAD_DIM))
        outs[6].append(s_s.astype(state_ssm.dtype))
        outs[7].append(jnp.concatenate([state_conv[l].astype(psb.dtype), psb[:, :, 3 * a_w:6 * a_w]],
                                       axis=1)[:, tn:])
    return (yp.reshape(batch, t, d), ys.reshape(n_seq, tn, d)) + tuple(jnp.stack(o) for o in outs)
```

```python
import functools
import math

import jax
import jax.numpy as jnp
from jax import lax
from jax.experimental import pallas as pl
from jax.experimental.pallas import tpu as pltpu

F32 = jnp.float32
BF16 = jnp.bfloat16
EPS = 1e-6
NEG_INF = float("-inf")
LOG2E = 1.4426950408889634

V7X_VMEM_LIMIT_BYTES = 60 * 1024 * 1024

QK_DIM = 64
HEAD_DIM = 128
N_HEADS = 8
CONV_W = 4
CHUNK = 64
GDN_ROWS = 256

def _dot(a, b):
    return jnp.dot(a, b, preferred_element_type=F32)


def _dot_nt(a, b):
    return lax.dot_general(a, b, (((1,), (1,)), ((), ())), preferred_element_type=F32)


def _split3(x):
    hi = x.astype(BF16)
    r = x - hi.astype(F32)
    mid = r.astype(BF16)
    lo = (r - mid.astype(F32)).astype(BF16)
    return hi, mid, lo


def _dot_sel_l(sel, x):
    hi, mid, lo = _split3(x)
    return _dot(sel, hi) + _dot(sel, mid) + _dot(sel, lo)


def _dot_sel_r(x, sel):
    hi, mid, lo = _split3(x)
    return _dot(hi, sel) + _dot(mid, sel) + _dot(lo, sel)


def _dot_hp(a, b):
    ah = a.astype(BF16)
    al = (a - ah.astype(F32)).astype(BF16)
    bh = b.astype(BF16)
    bl = (b - bh.astype(F32)).astype(BF16)
    return _dot(ah, bh) + _dot(ah, bl) + _dot(al, bh)


def _rms(x, w):
    return x * lax.rsqrt(jnp.mean(x * x, axis=-1, keepdims=True) + EPS) * w


def _silu(x):
    return x * jax.nn.sigmoid(x)


def _softplus(x):
    return jnp.maximum(x, 0.0) + jnp.log1p(jnp.exp(-jnp.abs(x)))


def _lambda_full(lam_ref, lam_init):
    l = lam_ref[...]
    a = jnp.sum(l[0:1] * l[1:2], axis=-1, keepdims=True)
    b = jnp.sum(l[2:3] * l[3:4], axis=-1, keepdims=True)
    return jnp.exp(a) - jnp.exp(b) + lam_init


def _idiv(x, n):
    assert n & (n - 1) == 0
    return x >> (n.bit_length() - 1)


def _imod(x, n):
    assert n & (n - 1) == 0
    return x & (n - 1)


def _pow2_neg(k):
    return lax.bitcast_convert_type((127 - k) << 23, F32)


def _ffn_kernel(x_ref, prew_ref, postw_ref, wg_ref, wu_ref, wd_ref, o_ref, *rest, nf, emit_bf16):
    h_ref = rest[-1]
    f = pl.program_id(1)

    @pl.when(f == 0)
    def _():
        h_ref[...] = _rms(x_ref[...], prew_ref[...]).astype(BF16)
        o_ref[...] = jnp.zeros_like(o_ref)

    wg, wu, wd = wg_ref[...], wu_ref[...], wd_ref[...]
    if emit_bf16:
        wg, wu, wd = wg.astype(BF16), wu.astype(BF16), wd.astype(BF16)
        rest[0][...], rest[1][...], rest[2][...] = wg, wu, wd
    h = h_ref[...]
    g = _dot(h, wg)
    u = _dot(h, wu)
    o_ref[...] += _dot((_silu(g) * u).astype(BF16), wd)

    @pl.when(f == nf - 1)
    def _():
        o_ref[...] = x_ref[...] + 0.5 * _rms(o_ref[...], postw_ref[...])


def _ffn(x, pre_w, post_w, wg, wu, wd, *, tm, tf, emit_bf16=False):
    m, d = x.shape
    ff = wg.shape[1]
    nf = ff // tf
    assert not emit_bf16 or m == tm
    gu_spec = pl.BlockSpec((d, tf), lambda i, f: (0, f))
    dn_spec = pl.BlockSpec((tf, d), lambda i, f: (f, 0))
    out_specs = [pl.BlockSpec((tm, d), lambda i, f: (i, 0))]
    out_shape = [jax.ShapeDtypeStruct((m, d), F32)]
    if emit_bf16:
        out_specs += [gu_spec, gu_spec, dn_spec]
        out_shape += [jax.ShapeDtypeStruct(w.shape, BF16) for w in (wg, wu, wd)]
    outs = pl.pallas_call(
        functools.partial(_ffn_kernel, nf=nf, emit_bf16=emit_bf16),
        grid=(m // tm, nf),
        in_specs=[
            pl.BlockSpec((tm, d), lambda i, f: (i, 0)),
            pl.BlockSpec((1, d), lambda i, f: (0, 0)),
            pl.BlockSpec((1, d), lambda i, f: (0, 0)),
            gu_spec, gu_spec, dn_spec,
        ],
        out_specs=out_specs,
        out_shape=out_shape,
        scratch_shapes=[pltpu.VMEM((tm, d), BF16)],
        compiler_params=pltpu.CompilerParams(
            dimension_semantics=("parallel", "arbitrary"), vmem_limit_bytes=V7X_VMEM_LIMIT_BYTES),
        name="ffn",
    )(x, pre_w, post_w, wg, wu, wd)
    return outs if emit_bf16 else outs[0]


def _inproj_kernel(x_ref, prew_ref, wt_ref, wgt_ref, o_ref, og_ref, ko_ref, vo_ref, *rest, emit_bf16):
    h_ref = rest[-1]
    j = pl.program_id(1)

    @pl.when(j == 0)
    def _():
        h = _rms(x_ref[...], prew_ref[...]).astype(BF16)
        h_ref[...] = h
        og_ref[...] = _dot_nt(h, wgt_ref[...])

    wt = wt_ref[...]
    if emit_bf16:
        wt = wt.astype(BF16)
        rest[0][...] = wt
    y = _dot_nt(h_ref[...], wt)
    o_ref[...] = y

    @pl.when(j == 1)
    def _():
        ko_ref[...] = y

    @pl.when(j == 2)
    def _():
        vo_ref[...] = y


def _inproj(x, pre_w, w_main_t, w_gates_t, *, tm, tn, n, layer=None, emit_bf16=False):
    m, d = x.shape
    ng = w_gates_t.shape[0]
    assert not emit_bf16 or m == tm
    w_spec = pl.BlockSpec((tn, d), lambda i, j: (j, 0))
    w_in_spec = w_spec if layer is None else pl.BlockSpec((None, tn, d), lambda i, j: (layer, j, 0))
    row_spec = pl.BlockSpec((tm, tn), lambda i, j: (i, 0))
    out_specs = [pl.BlockSpec((tm, tn), lambda i, j: (i, j)), pl.BlockSpec((tm, ng), lambda i, j: (i, 0)),
                 row_spec, row_spec]
    out_shape = [jax.ShapeDtypeStruct((m, n), F32), jax.ShapeDtypeStruct((m, ng), F32),
                 jax.ShapeDtypeStruct((m, tn), F32), jax.ShapeDtypeStruct((m, tn), F32)]
    if emit_bf16:
        out_specs.append(w_spec)
        out_shape.append(jax.ShapeDtypeStruct((n, d), BF16))
    return pl.pallas_call(
        functools.partial(_inproj_kernel, emit_bf16=emit_bf16),
        grid=(m // tm, n // tn),
        in_specs=[
            pl.BlockSpec((tm, d), lambda i, j: (i, 0),
                         pipeline_mode=pl.Buffered(1 if tm * d * 4 >= 8 * 1024 * 1024 else 2)),
            pl.BlockSpec((1, d), lambda i, j: (0, 0)),
            w_in_spec,
            pl.BlockSpec((ng, d), lambda i, j: (0, 0)),
        ],
        out_specs=out_specs,
        out_shape=out_shape,
        scratch_shapes=[pltpu.VMEM((tm, d), BF16)],
        compiler_params=pltpu.CompilerParams(
            dimension_semantics=("parallel", "arbitrary"), vmem_limit_bytes=V7X_VMEM_LIMIT_BYTES),
        name="inproj",
    )(x, pre_w, w_main_t, w_gates_t)


def _outproj_kernel(x_ref, oa_ref, ob_ref, wa_ref, wb_ref, postw_ref, o_ref, *rest, emit_bf16):
    wa, wb = wa_ref[...], wb_ref[...]
    if emit_bf16:
        wa, wb = wa.astype(BF16), wb.astype(BF16)
        ka = wa.shape[0]
        rest[0][:ka, :] = wa
        rest[0][ka:, :] = wb
    y = _dot(oa_ref[...], wa) + _dot(ob_ref[...], wb)
    o_ref[...] = x_ref[...] + _rms(y, postw_ref[...])


def _outproj(x, oa, ob, w_out, post_w, *, tm, emit_bf16=False):
    m, d = x.shape
    ka = oa.shape[1]
    assert not emit_bf16 or m == tm
    out_specs = [pl.BlockSpec((tm, d), lambda i: (i, 0))]
    out_shape = [jax.ShapeDtypeStruct((m, d), F32)]
    if emit_bf16:
        out_specs.append(pl.BlockSpec(w_out.shape, lambda i: (0, 0)))
        out_shape.append(jax.ShapeDtypeStruct(w_out.shape, BF16))
    outs = pl.pallas_call(
        functools.partial(_outproj_kernel, emit_bf16=emit_bf16),
        grid=(m // tm,),
        in_specs=[
            pl.BlockSpec((tm, d), lambda i: (i, 0)),
            pl.BlockSpec((tm, ka), lambda i: (i, 0)),
            pl.BlockSpec((tm, ka), lambda i: (i, 0)),
            pl.BlockSpec((ka, d), lambda i: (0, 0)),
            pl.BlockSpec((ka, d), lambda i: (1, 0)),
            pl.BlockSpec((1, d), lambda i: (0, 0)),
        ],
        out_specs=out_specs,
        out_shape=out_shape,
        compiler_params=pltpu.CompilerParams(
            dimension_semantics=("parallel",), vmem_limit_bytes=V7X_VMEM_LIMIT_BYTES),
        name="outproj",
    )(x, oa, ob, w_out, w_out, post_w)
    return outs if emit_bf16 else outs[0]


def _attn_prompt_kernel(lam_ref, subw_ref, q_ref, k_ref, v_ref, o_ref,
                        kb_ref, vt_ref, bias_ref, *, t, tq, hpp, lam_init):
    hg = pl.program_id(1)
    nq = t // tq
    lam = _lambda_full(lam_ref, lam_init)
    lane = lax.broadcasted_iota(jnp.int32, (1, HEAD_DIM), 1)
    first_map = lane < QK_DIM
    kk = lax.broadcasted_iota(jnp.int32, (tq, tq), 0)
    qq = lax.broadcasted_iota(jnp.int32, (tq, tq), 1)
    slopes = [_pow2_neg(jnp.full((1, 1), hg * hpp + hh + 1, jnp.int32)) * LOG2E for hh in range(hpp)]
    er = lax.broadcasted_iota(jnp.int32, (HEAD_DIM, HEAD_DIM), 0)
    eye_b = (er == lax.broadcasted_iota(jnp.int32, (HEAD_DIM, HEAD_DIM), 1)).astype(BF16)
    for hh in range(hpp):
        cols = slice(hh * HEAD_DIM, (hh + 1) * HEAD_DIM)
        b = slopes[hh] * (kk - qq).astype(F32)
        bias_ref[hh, 0] = b
        bias_ref[hh, 1] = jnp.where(qq >= kk, b, NEG_INF)
        vt_ref[hh] = _dot_nt(eye_b, v_ref[:, cols].astype(BF16)).astype(BF16)
    kb_ref[...] = k_ref[...].astype(BF16)
    scale = QK_DIM ** -0.5 * LOG2E
    subw_col = subw_ref[...]

    def q_block(i, _):
        rows_i = pl.ds(pl.multiple_of(i * tq, tq), tq)
        qs = []
        for hh in range(hpp):
            q = q_ref[rows_i, hh * HEAD_DIM:(hh + 1) * HEAD_DIM] * scale
            qs.append(jnp.concatenate(
                [jnp.where(first_map, q, 0.0), jnp.where(first_map, 0.0, q)], axis=0).astype(BF16))

        def scores(j):
            rows_j = pl.ds(pl.multiple_of(j * tq, tq), tq)
            diag = jnp.asarray(j == i, jnp.int32)
            out = []
            for hh in range(hpp):
                b = bias_ref[hh, diag]
                out.append(_dot_nt(kb_ref[rows_j, hh * HEAD_DIM:(hh + 1) * HEAD_DIM], qs[hh])
                           + jnp.concatenate([b, b], axis=1))
            return tuple(out)

        def consume(j, sts, state):
            rows_j = pl.ds(pl.multiple_of(j * tq, tq), tq)
            blocks_away = jnp.asarray((i - j) * tq, F32)
            hs = range(hpp)
            far = [slopes[hh] * blocks_away for hh in hs]
            m_new = [jnp.maximum(state[hh][0], jnp.max(sts[hh], axis=0, keepdims=True) - far[hh])
                     for hh in hs]
            p = [jnp.exp2(sts[hh] - (far[hh] + m_new[hh])) for hh in hs]
            alpha = [jnp.exp2(state[hh][0] - m_new[hh]) for hh in hs]
            l_new = [alpha[hh] * state[hh][1] + jnp.sum(p[hh], axis=0, keepdims=True) for hh in hs]
            pv = [_dot(vt_ref[hh, :, rows_j], p[hh].astype(BF16)) for hh in hs]
            return tuple((m_new[hh], l_new[hh], alpha[hh] * state[hh][2] + pv[hh]) for hh in hs)

        def step(j, carry):
            sts, state = carry
            nxt = scores(j + 1)
            return nxt, consume(j, sts, state)

        init = tuple((jnp.full((1, 2 * tq), NEG_INF, F32), jnp.zeros((1, 2 * tq), F32),
                      jnp.zeros((HEAD_DIM, 2 * tq), F32)) for _ in range(hpp))
        sts, state = lax.fori_loop(0, i, step, (scores(0), init))
        state = consume(i, sts, state)
        for hh in range(hpp):
            _, l_fin, acc_fin = state[hh]
            ot = acc_fin / l_fin
            dt = ot[:, :tq] - lam * ot[:, tq:]
            dt = dt * lax.rsqrt(jnp.mean(dt * dt, axis=0, keepdims=True) + EPS) * subw_col
            o_ref[rows_i, hh * HEAD_DIM:(hh + 1) * HEAD_DIM] = (
                (dt * (1.0 - lam_init)).T.astype(o_ref.dtype))
        return 0

    lax.fori_loop(0, nq, q_block, 0)


def _attn_prompt(lam_rows, subln_col, proj, k, v, *, batch, t, lam_init, tq, hpp):
    m = proj.shape[0]
    ng = N_HEADS // hpp
    w = hpp * HEAD_DIM
    return pl.pallas_call(
        functools.partial(_attn_prompt_kernel, t=t, tq=tq, hpp=hpp, lam_init=lam_init),
        grid=(batch, ng),
        in_specs=[
            pl.BlockSpec((4, QK_DIM), lambda b, g: (0, 0)),
            pl.BlockSpec((HEAD_DIM, 1), lambda b, g: (0, 0)),
            pl.BlockSpec((t, w), lambda b, g: (b, g)),
            pl.BlockSpec((t, w), lambda b, g: (b, g)),
            pl.BlockSpec((t, w), lambda b, g: (b, g)),
        ],
        out_specs=pl.BlockSpec((t, w), lambda b, g: (b, g)),
        out_shape=jax.ShapeDtypeStruct((m, N_HEADS * HEAD_DIM), BF16),
        scratch_shapes=[
            pltpu.VMEM((t, w), BF16),
            pltpu.VMEM((hpp, HEAD_DIM, t), BF16),
            pltpu.VMEM((hpp, 2, tq, tq), F32),
        ],
        compiler_params=pltpu.CompilerParams(
            dimension_semantics=("parallel", "parallel"), vmem_limit_bytes=V7X_VMEM_LIMIT_BYTES),
        name="attn_prompt",
    )(lam_rows, subln_col, proj, k, v)


def _attn_sample_kernel(pt_ref, lam_ref, subw_ref, q_ref, kn_ref, vn_ref, *rest,
                        n_pages, page, tn, group, lam_init):
    del pt_ref
    kc_refs = rest[:group]
    vc_refs = rest[group:2 * group]
    o_ref, qb_ref, bias_ref, m_ref, l_ref, acc_ref = rest[2 * group:]
    p = pl.program_id(1)
    nrow = 2 * tn * N_HEADS
    ncol = page * N_HEADS
    past = n_pages * page
    row = lax.broadcasted_iota(jnp.int32, (nrow, 1), 0)
    slope = _pow2_neg(_imod(row, N_HEADS) + 1)

    @pl.when(p == 0)
    def _():
        q = q_ref[0] * (QK_DIM ** -0.5)
        lane = lax.broadcasted_iota(jnp.int32, (1, HEAD_DIM), 1)
        first_map = lane < QK_DIM
        qb_ref[...] = jnp.concatenate(
            [jnp.where(first_map, q, 0.0), jnp.where(first_map, 0.0, q)], axis=0).astype(BF16)
        rr = lax.broadcasted_iota(jnp.int32, (nrow, ncol), 0)
        cc = lax.broadcasted_iota(jnp.int32, (nrow, ncol), 1)
        tq_ = _idiv(_imod(rr, tn * N_HEADS), N_HEADS)
        same_head = _imod(rr, N_HEADS) == _imod(cc, N_HEADS)
        bias_ref[...] = jnp.where(same_head, slope * (_idiv(cc, N_HEADS) - tq_).astype(F32), NEG_INF)
        m_ref[...] = jnp.full(m_ref.shape, NEG_INF, F32)
        l_ref[...] = jnp.zeros(l_ref.shape, F32)
        acc_ref[...] = jnp.zeros(acc_ref.shape, F32)

    def lane_fold(x, op):
        out = x[:, :HEAD_DIM]
        for i in range(1, x.shape[1] // HEAD_DIM):
            out = op(out, x[:, i * HEAD_DIM:(i + 1) * HEAD_DIM])
        return out

    def update(scores, offs, vals):
        m_old = m_ref[...]
        m_loc = lane_fold(scores[0], jnp.maximum) - offs[0]
        for s, off in zip(scores[1:], offs[1:]):
            m_loc = jnp.maximum(m_loc, lane_fold(s, jnp.maximum) - off)
        m_new = jnp.maximum(m_old, jnp.max(m_loc, axis=-1, keepdims=True))
        alpha = jnp.exp(m_old - m_new)
        pes = [jnp.exp(s - (off + m_new)) for s, off in zip(scores, offs)]
        l_part = lane_fold(pes[0], jnp.add)
        for pe in pes[1:]:
            l_part = l_part + lane_fold(pe, jnp.add)
        pv = _dot(jnp.concatenate([pe.astype(BF16) for pe in pes], axis=1),
                  jnp.concatenate(vals, axis=0))
        l_ref[...] = alpha * l_ref[...] + l_part
        acc_ref[...] = alpha * acc_ref[...] + pv
        m_ref[...] = m_new

    qb = qb_ref[...]
    bias = bias_ref[...]
    scores, offs, vals = [], [], []
    for g in range(group):
        kf = kc_refs[g][0].reshape(ncol, HEAD_DIM).astype(BF16)
        vals.append(vc_refs[g][0].reshape(ncol, HEAD_DIM).astype(BF16))
        scores.append(_dot_nt(qb, kf) + bias)
        offs.append(slope * jnp.asarray(past - (p * group + g) * page, F32))
    update(scores, offs, vals)

    @pl.when(p == n_pages // group - 1)
    def _():
        ncn = tn * N_HEADS
        rr = lax.broadcasted_iota(jnp.int32, (nrow, HEAD_DIM), 0)
        cc = lax.broadcasted_iota(jnp.int32, (nrow, HEAD_DIM), 1)
        tq_ = _idiv(_imod(rr, ncn), N_HEADS)
        tk_ = _idiv(cc, N_HEADS)
        ok = (cc < ncn) & (_imod(rr, N_HEADS) == _imod(cc, N_HEADS)) & (tk_ <= tq_)
        bias_new = jnp.where(ok, slope * (tk_ - tq_).astype(F32), NEG_INF)
        pad = jnp.zeros((HEAD_DIM - ncn, HEAD_DIM), BF16)
        kn = jnp.concatenate([kn_ref[0].astype(BF16), pad], axis=0)
        vn = jnp.concatenate([vn_ref[0].astype(BF16), pad], axis=0)
        update([_dot_nt(qb, kn) + bias_new], [jnp.zeros((nrow, 1), F32)], [vn])
        lam = _lambda_full(lam_ref, lam_init)
        o = acc_ref[...] / jnp.sum(l_ref[...], axis=-1, keepdims=True)
        d = o[:ncn] - lam * o[ncn:]
        o_ref[0] = (_rms(d, subw_ref[...]) * (1.0 - lam_init)).astype(o_ref.dtype)


def _attn_sample(page_table, lam_rows, subln_w, q, k_new, v_new, cache_k, cache_v, *, lam_init, group):
    n_seq, n_pages = page_table.shape
    page = cache_k.shape[1]
    tn = q.shape[1] // N_HEADS
    nrow = 2 * tn * N_HEADS
    ncol = page * N_HEADS
    assert n_pages % group == 0

    def page_spec(g):
        return pl.BlockSpec((1, page, N_HEADS, HEAD_DIM),
                            lambda b, p, pt: (pt[b, p * group + g], 0, 0, 0))

    new_spec = pl.BlockSpec((1, tn * N_HEADS, HEAD_DIM), lambda b, p, pt: (b, 0, 0))
    grid_spec = pltpu.PrefetchScalarGridSpec(
        num_scalar_prefetch=1,
        grid=(n_seq, n_pages // group),
        in_specs=[
            pl.BlockSpec((4, QK_DIM), lambda b, p, pt: (0, 0)),
            pl.BlockSpec((1, HEAD_DIM), lambda b, p, pt: (0, 0)),
            new_spec, new_spec, new_spec,
        ] + [page_spec(g) for g in range(group)] * 2,
        out_specs=new_spec,
        scratch_shapes=[
            pltpu.VMEM((nrow, HEAD_DIM), BF16),
            pltpu.VMEM((nrow, ncol), F32),
            pltpu.VMEM((nrow, 1), F32),
            pltpu.VMEM((nrow, HEAD_DIM), F32),
            pltpu.VMEM((nrow, HEAD_DIM), F32),
        ],
    )
    return pl.pallas_call(
        functools.partial(_attn_sample_kernel, n_pages=n_pages, page=page, tn=tn, group=group,
                          lam_init=lam_init),
        grid_spec=grid_spec,
        out_shape=jax.ShapeDtypeStruct((n_seq, tn * N_HEADS, HEAD_DIM), BF16),
        compiler_params=pltpu.CompilerParams(
            dimension_semantics=("parallel", "arbitrary"), vmem_limit_bytes=V7X_VMEM_LIMIT_BYTES),
        name="attn_sample",
    )(page_table, lam_rows, subln_w, q, k_new, v_new, *([cache_k] * group), *([cache_v] * group))


def _gdn_n_masks(chunk):
    return 3 + (chunk.bit_length() - 1)


def _gdn_fill_masks(mask_ref, *, rows, chunk):
    ri = lax.broadcasted_iota(jnp.int32, (rows, rows), 0)
    ci = lax.broadcasted_iota(jnp.int32, (rows, rows), 1)
    same = _idiv(ri, chunk) == _idiv(ci, chunk)
    mask_ref[0] = (same & (ri >= ci)).astype(F32)
    mask_ref[1] = (same & (ri > ci)).astype(F32)
    mask_ref[2] = (ri == ci).astype(F32)
    s, k = 1, 3
    while s < chunk:
        mask_ref[k] = ((_idiv(ri, 2 * s) == _idiv(ci, 2 * s)) & (_imod(_idiv(ri, s), 2) == 1)
                       & (_imod(_idiv(ci, s), 2) == 0)).astype(F32)
        s, k = 2 * s, k + 1


def _gdn_intra_all(act, gates, alog_row, dtb_row, mask_ref, u_ref, w_ref, qg_ref, kdt_ref, qkc_ref,
                   egl_ref, *, rows, chunk):
    n_chunks = rows // chunk
    n_levels = _gdn_n_masks(chunk) - 3
    beta_all = jax.nn.sigmoid(gates)
    g_all = -jnp.exp(alog_row) * _softplus(gates + dtb_row)
    gcum_all = _dot_sel_l(mask_ref[0].astype(BF16), g_all)
    gtot_all = jnp.concatenate(
        [jnp.broadcast_to(gcum_all[(c + 1) * chunk - 1:(c + 1) * chunk], (chunk, HEAD_DIM))
         for c in range(n_chunks)], axis=0)
    gcum_t = gcum_all.T
    fr = lax.broadcasted_iota(jnp.int32, (rows, chunk), 0)
    fc = lax.broadcasted_iota(jnp.int32, (rows, chunk), 1)
    fold = (_imod(fr, chunk) == fc).astype(BF16)
    er = lax.broadcasted_iota(jnp.int32, (HEAD_DIM, HEAD_DIM), 0)
    eye_b = (er == lax.broadcasted_iota(jnp.int32, (HEAD_DIM, HEAD_DIM), 1)).astype(BF16)

    def lanes(col):
        return jnp.broadcast_to(col, (rows, HEAD_DIM))

    def square(x):
        if rows >= HEAD_DIM:
            return jnp.concatenate([x] * (rows // HEAD_DIM), axis=1)
        return x[:, :rows]

    heads = range(N_HEADS)
    egl_rows, lmats, lbs, tinvs, rhss = [], [], [], [], []
    for h in heads:
        cols = slice(h * HEAD_DIM, (h + 1) * HEAD_DIM)
        aq = act[:, h * HEAD_DIM:(h + 1) * HEAD_DIM]
        ak = act[:, (N_HEADS + h) * HEAD_DIM:(N_HEADS + h + 1) * HEAD_DIM]
        av = act[:, (2 * N_HEADS + h) * HEAD_DIM:(2 * N_HEADS + h + 1) * HEAD_DIM]
        qn = aq * lax.rsqrt(jnp.sum(aq * aq, axis=-1, keepdims=True) + 1e-6) * (HEAD_DIM ** -0.5)
        kn = ak * lax.rsqrt(jnp.sum(ak * ak, axis=-1, keepdims=True) + 1e-6)
        beta = lanes(beta_all[:, h:h + 1])
        g_cum = lanes(gcum_all[:, N_HEADS + h:N_HEADS + h + 1])
        g_tot = lanes(gtot_all[:, N_HEADS + h:N_HEADS + h + 1])
        g_row = gcum_t[N_HEADS + h:N_HEADS + h + 1, :]
        decay = jnp.exp(jnp.minimum(square(g_cum) - g_row, 0.0)) * mask_ref[0]

        kbeta = kn * beta
        knb = kn.astype(BF16)
        lmat = mask_ref[1] * (_dot_nt(kbeta.astype(BF16), knb) * decay)
        lmats.append(lmat)
        lbs.append(lmat.astype(BF16))
        tinvs.append(mask_ref[2] - lmat * mask_ref[3])
        e_cum = jnp.exp(g_cum)
        rhss.append(jnp.concatenate([av * beta, kbeta * e_cum], axis=1).astype(BF16))
        qk = (_dot_nt(qn.astype(BF16), knb) * decay).astype(BF16)
        qkc_ref[h] = _dot(qk, fold).astype(BF16)
        qg_ref[:, cols] = (qn * e_cum).astype(BF16)
        kd = (kn * jnp.exp(g_tot - g_cum)).astype(BF16)
        kdt_ref[h] = _dot_nt(eye_b, kd).astype(BF16)
        egl = jnp.exp(g_tot)
        egl_rows.append(jnp.concatenate(
            [egl[c * chunk:c * chunk + 1] for c in range(n_chunks)]
            + [jnp.zeros((8 - n_chunks, HEAD_DIM), F32)] * (n_chunks < 8), axis=0))
    egl_ref[0] = jnp.concatenate(egl_rows, axis=1)

    for k in range(1, n_levels):
        tbs = [tinvs[h].astype(BF16) for h in heads]
        mids = [_dot(tbs[h], lbs[h]).astype(BF16) for h in heads]
        tinvs = [tinvs[h] - mask_ref[3 + k] * _dot(mids[h], tbs[h]) for h in heads]
    resids = [mask_ref[2] - tinvs[h] - _dot_hp(lmats[h], tinvs[h]) for h in heads]
    tinvs = [tinvs[h] + _dot(tinvs[h].astype(BF16), resids[h].astype(BF16)) for h in heads]
    for h in heads:
        cols = slice(h * HEAD_DIM, (h + 1) * HEAD_DIM)
        uw = _dot(tinvs[h].astype(BF16), rhss[h])
        u_ref[:, cols] = uw[:, :HEAD_DIM]
        w_ref[:, cols] = uw[:, HEAD_DIM:].astype(BF16)


def _gdn_intra_prompt_kernel(x_ref, gates_ref, cw_ref, alog_ref, dtb_ref,
                             u_ref, w_ref, qg_ref, kdt_ref, qkc_ref, egl_ref, pad_ref, mask_ref,
                             *, rows, chunk):
    t = pl.program_id(1)

    @pl.when(t == 0)
    def _():
        pad_ref[0:8, :] = jnp.zeros((8, pad_ref.shape[1]), F32)
        _gdn_fill_masks(mask_ref, rows=rows, chunk=chunk)

    pad_ref[8:, :] = x_ref[...]
    cw = cw_ref[...]
    acc = pad_ref[pl.ds(8 - (CONV_W - 1), rows), :] * cw[0:1]
    for j in range(1, CONV_W):
        acc = acc + pad_ref[pl.ds(8 - (CONV_W - 1) + j, rows), :] * cw[j:j + 1]
    pad_ref[0:8, :] = x_ref[rows - 8:, :]
    _gdn_intra_all(_silu(acc), gates_ref[...], alog_ref[...], dtb_ref[...], mask_ref,
                   u_ref, w_ref, qg_ref, kdt_ref, qkc_ref, egl_ref, rows=rows, chunk=chunk)


def _gdn_intra_sample_kernel(x_ref, gates_ref, cw_ref, alog_ref, dtb_ref, buf_ref,
                             u_ref, w_ref, qg_ref, kdt_ref, qkc_ref, egl_ref, pad_ref, mask_ref,
                             *, n_seq, tn):
    seg = 8 + tn
    _gdn_fill_masks(mask_ref, rows=n_seq * tn, chunk=tn)
    cw = cw_ref[...]
    for s in range(n_seq):
        pad_ref[s * seg + 8 - (CONV_W - 1):s * seg + 8, :] = buf_ref[s]
        pad_ref[s * seg + 8:(s + 1) * seg, :] = x_ref[s * tn:(s + 1) * tn, :]
    pieces = []
    for s in range(n_seq):
        base = s * seg + 8 - (CONV_W - 1)
        acc = pad_ref[pl.ds(base, tn), :] * cw[0:1]
        for j in range(1, CONV_W):
            acc = acc + pad_ref[pl.ds(base + j, tn), :] * cw[j:j + 1]
        pieces.append(acc)
    _gdn_intra_all(_silu(jnp.concatenate(pieces, axis=0)), gates_ref[...], alog_ref[...], dtb_ref[...],
                   mask_ref, u_ref, w_ref, qg_ref, kdt_ref, qkc_ref, egl_ref, rows=n_seq * tn, chunk=tn)


def _intra_out(m, chunk, n_steps):
    width = N_HEADS * HEAD_DIM
    return [
        jax.ShapeDtypeStruct((m, width), F32),
        jax.ShapeDtypeStruct((m, width), BF16),
        jax.ShapeDtypeStruct((m, width), BF16),
        jax.ShapeDtypeStruct((N_HEADS, HEAD_DIM, m), BF16),
        jax.ShapeDtypeStruct((N_HEADS, m, chunk), BF16),
        jax.ShapeDtypeStruct((n_steps, 8, width), F32),
    ]


def _gdn_intra_prompt(proj, gates, conv_w, alog_row, dtb_row, *, batch, t):
    m = proj.shape[0]
    rows = GDN_ROWS
    nblk = t // rows
    width = N_HEADS * HEAD_DIM
    cwid = 3 * width

    def rb(b, i):
        return b * nblk + i

    in_specs = [
        pl.BlockSpec((rows, cwid), lambda b, i: (rb(b, i), 1)),
        pl.BlockSpec((rows, HEAD_DIM), lambda b, i: (rb(b, i), 0)),
        pl.BlockSpec((CONV_W, cwid), lambda b, i: (0, 0)),
        pl.BlockSpec((1, HEAD_DIM), lambda b, i: (0, 0)),
        pl.BlockSpec((1, HEAD_DIM), lambda b, i: (0, 0)),
    ]
    out_specs = [
        pl.BlockSpec((rows, width), lambda b, i: (rb(b, i), 0)),
        pl.BlockSpec((rows, width), lambda b, i: (rb(b, i), 0)),
        pl.BlockSpec((rows, width), lambda b, i: (rb(b, i), 0)),
        pl.BlockSpec((N_HEADS, HEAD_DIM, rows), lambda b, i: (0, 0, rb(b, i))),
        pl.BlockSpec((N_HEADS, rows, CHUNK), lambda b, i: (0, rb(b, i), 0)),
        pl.BlockSpec((1, 8, width), lambda b, i: (rb(b, i), 0, 0)),
    ]
    return pl.pallas_call(
        functools.partial(_gdn_intra_prompt_kernel, rows=rows, chunk=CHUNK),
        grid=(batch, nblk),
        in_specs=in_specs,
        out_specs=out_specs,
        out_shape=_intra_out(m, CHUNK, batch * nblk),
        scratch_shapes=[pltpu.VMEM((rows + 8, cwid), F32),
                        pltpu.VMEM((_gdn_n_masks(CHUNK), rows, rows), F32)],
        compiler_params=pltpu.CompilerParams(
            dimension_semantics=("parallel", "arbitrary"), vmem_limit_bytes=V7X_VMEM_LIMIT_BYTES),
        name="gdn_intra_prompt",
    )(proj, gates, conv_w, alog_row, dtb_row)


def _gdn_intra_sample(proj, gates, conv_w, alog_row, dtb_row, conv_state, *, n_seq, tn):
    m = proj.shape[0]
    width = N_HEADS * HEAD_DIM
    cwid = 3 * width
    in_specs = [
        pl.BlockSpec((m, cwid), lambda i: (0, 1)),
        pl.BlockSpec((m, HEAD_DIM), lambda i: (0, 0)),
        pl.BlockSpec((CONV_W, cwid), lambda i: (0, 0)),
        pl.BlockSpec((1, HEAD_DIM), lambda i: (0, 0)),
        pl.BlockSpec((1, HEAD_DIM), lambda i: (0, 0)),
        pl.BlockSpec((n_seq, CONV_W - 1, cwid), lambda i: (0, 0, 0)),
    ]
    out_specs = [
        pl.BlockSpec((m, width), lambda i: (0, 0)),
        pl.BlockSpec((m, width), lambda i: (0, 0)),
        pl.BlockSpec((m, width), lambda i: (0, 0)),
        pl.BlockSpec((N_HEADS, HEAD_DIM, m), lambda i: (0, 0, 0)),
        pl.BlockSpec((N_HEADS, m, tn), lambda i: (0, 0, 0)),
        pl.BlockSpec((1, 8, width), lambda i: (0, 0, 0)),
    ]
    return pl.pallas_call(
        functools.partial(_gdn_intra_sample_kernel, n_seq=n_seq, tn=tn),
        grid=(1,),
        in_specs=in_specs,
        out_specs=out_specs,
        out_shape=_intra_out(m, tn, 1),
        scratch_shapes=[pltpu.VMEM((n_seq * (8 + tn), cwid), F32),
                        pltpu.VMEM((_gdn_n_masks(tn), m, m), F32)],
        compiler_params=pltpu.CompilerParams(
            dimension_semantics=("arbitrary",), vmem_limit_bytes=V7X_VMEM_LIMIT_BYTES),
        name="gdn_intra_sample",
    )(proj, gates, conv_w, alog_row, dtb_row, conv_state)


def _gdn_scan_kernel(u_ref, w_ref, qg_ref, kdt_ref, qkc_ref, egl_ref, z_ref, s0_ref, dnw_ref,
                     o_ref, s_ref, *, chunk, n_chunks, seqs_per_step):
    t = pl.program_id(1)

    @pl.when(t == 0)
    def _():
        s_ref[...] = s0_ref[...]

    dnw = dnw_ref[...]
    for c in range(n_chunks):
        si = c if seqs_per_step > 1 else 0
        rows = slice(c * chunk, (c + 1) * chunk)
        heads = range(N_HEADS)
        hcols = [slice(h * HEAD_DIM, (h + 1) * HEAD_DIM) for h in heads]
        s_old = [s_ref[si, h] for h in heads]
        both = [_dot(jnp.concatenate([w_ref[rows, hcols[h]], qg_ref[rows, hcols[h]]], axis=0),
                     s_old[h].astype(BF16)) for h in heads]
        vn = [(u_ref[rows, hcols[h]] - both[h][:chunk]).astype(BF16) for h in heads]
        o = [both[h][chunk:] + _dot(qkc_ref[h, rows, :], vn[h]) for h in heads]
        for h in heads:
            s_ref[si, h] = s_old[h] * egl_ref[0, c:c + 1, hcols[h]] + _dot(kdt_ref[h, :, rows], vn[h])
        for h in heads:
            o_ref[rows, hcols[h]] = (_rms(o[h], dnw) * _silu(z_ref[rows, hcols[h]])).astype(o_ref.dtype)


def _gdn_scan(intra, z_src, z_col_block, s0, dnw, *, n_seq, t, chunk, chunks_per_step, seqs_per_step=1):
    u, w, qg, kdt, qkc, egl = intra
    m = u.shape[0]
    width = N_HEADS * HEAD_DIM
    rows = chunk * chunks_per_step
    if seqs_per_step > 1:
        assert seqs_per_step == n_seq == chunks_per_step and t == chunk
        nsteps, n_outer = 1, 1
    else:
        nsteps, n_outer = t // rows, n_seq

    def rb(b, i):
        return b * nsteps + i

    in_specs = [
        pl.BlockSpec((rows, width), lambda b, i: (rb(b, i), 0)),
        pl.BlockSpec((rows, width), lambda b, i: (rb(b, i), 0)),
        pl.BlockSpec((rows, width), lambda b, i: (rb(b, i), 0)),
        pl.BlockSpec((N_HEADS, HEAD_DIM, rows), lambda b, i: (0, 0, rb(b, i))),
        pl.BlockSpec((N_HEADS, rows, chunk), lambda b, i: (0, rb(b, i), 0)),
        pl.BlockSpec((1, 8, width), lambda b, i: (rb(b, i), 0, 0)),
        pl.BlockSpec((rows, width), lambda b, i: (rb(b, i), z_col_block)),
        pl.BlockSpec((seqs_per_step, N_HEADS, HEAD_DIM, HEAD_DIM), lambda b, i: (b, 0, 0, 0)),
        pl.BlockSpec((1, HEAD_DIM), lambda b, i: (0, 0)),
    ]
    out_specs = [
        pl.BlockSpec((rows, width), lambda b, i: (rb(b, i), 0)),
        pl.BlockSpec((seqs_per_step, N_HEADS, HEAD_DIM, HEAD_DIM), lambda b, i: (b, 0, 0, 0)),
    ]
    return pl.pallas_call(
        functools.partial(_gdn_scan_kernel, chunk=chunk, n_chunks=chunks_per_step,
                          seqs_per_step=seqs_per_step),
        grid=(n_outer, nsteps),
        in_specs=in_specs,
        out_specs=out_specs,
        out_shape=[jax.ShapeDtypeStruct((m, width), BF16),
                   jax.ShapeDtypeStruct((n_seq, N_HEADS, HEAD_DIM, HEAD_DIM), F32)],
        compiler_params=pltpu.CompilerParams(
            dimension_semantics=("parallel", "arbitrary"), vmem_limit_bytes=V7X_VMEM_LIMIT_BYTES),
        name="gdn_scan",
    )(u, w, qg, kdt, qkc, egl, z_src, s0, dnw)


def _pad_lanes(v, offset):
    out = jnp.zeros((1, HEAD_DIM), F32)
    return lax.dynamic_update_slice(out, v.reshape(1, -1).astype(F32), (0, offset))


def kernel(x_prompt, x_sample, cache_k, cache_v, state_ssm, state_conv, page_table, ffn1_pre_w, ffn1_post_w, ffn1_gate, ffn1_up, ffn1_down, mix_pre_w, mix_post_w, w_in, conv_w, a_log, dt_bias, delta_norm_w, lambda_q1, lambda_k1, lambda_q2, lambda_k2, subln_w, w_out, ffn2_pre_w, ffn2_post_w, ffn2_gate, ffn2_up, ffn2_down):
    depth = w_in.shape[0]
    batch, t, d = x_prompt.shape
    n_seq, tn, _ = x_sample.shape
    a_w = N_HEADS * HEAD_DIM
    n_main = 3 * a_w + 3 * a_w + a_w
    yp = x_prompt.reshape(batch * t, d)
    ys = x_sample.reshape(n_seq * tn, d)
    outs = [[] for _ in range(8)]
    w_in_t = jnp.swapaxes(w_in, 1, 2)
    for l in range(depth):
        lam_init = 0.8 - 0.6 * math.exp(-0.3 * l)
        lam_rows = jnp.stack([lambda_q1[l], lambda_k1[l], lambda_q2[l], lambda_k2[l]]).astype(F32)
        row = lambda v: v[l].reshape(1, -1).astype(F32)
        w_gates = jnp.pad(w_in_t[l, n_main:], ((0, HEAD_DIM - 2 * N_HEADS), (0, 0))).astype(BF16)
        alog_row = _pad_lanes(a_log[l], N_HEADS)
        dtb_row = _pad_lanes(dt_bias[l], N_HEADS)
        dnw = row(delta_norm_w)
        subw = row(subln_w)
        cw = conv_w[l].astype(F32)
        ms = n_seq * tn

        ys, wg1, wu1, wd1 = _ffn(ys, row(ffn1_pre_w), row(ffn1_post_w), ffn1_gate[l], ffn1_up[l],
                                 ffn1_down[l], tm=ms, tf=512, emit_bf16=True)
        yp = _ffn(yp, row(ffn1_pre_w), row(ffn1_post_w), wg1, wu1, wd1, tm=512, tf=512)

        ps, gs, ks, vs, w_main = _inproj(ys, row(mix_pre_w), w_in_t, w_gates, tm=ms, tn=a_w, n=n_main,
                                         layer=l, emit_bf16=True)
        qs = ps[:, :a_w].reshape(n_seq, tn * N_HEADS, HEAD_DIM)
        ks = ks.reshape(n_seq, tn * N_HEADS, HEAD_DIM)
        vs = vs.reshape(n_seq, tn * N_HEADS, HEAD_DIM)
        oa_s = _attn_sample(page_table, lam_rows, subw, qs, ks, vs, cache_k[l], cache_v[l],
                            lam_init=lam_init, group=16)
        oa_s = oa_s.reshape(ms, a_w)
        intra_s = _gdn_intra_sample(ps, gs, cw, alog_row, dtb_row, state_conv[l], n_seq=n_seq, tn=tn)
        ob_s, s_s = _gdn_scan(intra_s, ps, 6, state_ssm[l].astype(F32), dnw, n_seq=n_seq, t=tn,
                              chunk=tn, chunks_per_step=n_seq, seqs_per_step=n_seq)
        ys, wo = _outproj(ys, oa_s, ob_s, w_out[l], row(mix_post_w), tm=ms, emit_bf16=True)

        pp, gp, kp, vp = _inproj(yp, row(mix_pre_w), w_main, w_gates, tm=1024, tn=a_w, n=n_main)
        oa_p = _attn_prompt(lam_rows, subw.reshape(HEAD_DIM, 1), pp, kp, vp, batch=batch, t=t,
                            lam_init=lam_init, tq=256, hpp=2)
        intra_p = _gdn_intra_prompt(pp, gp, cw, alog_row, dtb_row, batch=batch, t=t)
        zero_s = jnp.zeros((batch, N_HEADS, HEAD_DIM, HEAD_DIM), F32)
        ob_p, s_p = _gdn_scan(intra_p, pp, 6, zero_s, dnw, n_seq=batch, t=t, chunk=CHUNK,
                              chunks_per_step=GDN_ROWS // CHUNK)
        yp = _outproj(yp, oa_p, ob_p, wo, row(mix_post_w), tm=512)

        ys, wg2, wu2, wd2 = _ffn(ys, row(ffn2_pre_w), row(ffn2_post_w), ffn2_gate[l], ffn2_up[l],
                                 ffn2_down[l], tm=ms, tf=512, emit_bf16=True)
        yp = _ffn(yp, row(ffn2_pre_w), row(ffn2_post_w), wg2, wu2, wd2, tm=512, tf=512)

        ppb = pp.reshape(batch, t, -1)
        psb = ps.reshape(n_seq, tn, -1)
        outs[0].append(kp.reshape(batch, t, N_HEADS, HEAD_DIM))
        outs[1].append(vp.reshape(batch, t, N_HEADS, HEAD_DIM))
        outs[2].append(s_p.astype(state_ssm.dtype))
        outs[3].append(ppb[:, t - (CONV_W - 1):, 3 * a_w:6 * a_w])
        outs[4].append(ks.reshape(n_seq, tn, N_HEADS, HEAD_DIM))
        outs[5].append(vs.reshape(n_seq, tn, N_HEADS, HEAD_DIM))
        outs[6].append(s_s.astype(state_ssm.dtype))
        outs[7].append(jnp.concatenate([state_conv[l].astype(psb.dtype), psb[:, :, 3 * a_w:6 * a_w]],
                                       axis=1)[:, tn:])
    return (yp.reshape(batch, t, d), ys.reshape(n_seq, tn, d)) + tuple(jnp.stack(o) for o in outs)
```

```python
import functools
import math

import jax
import jax.numpy as jnp
from jax import lax
from jax.experimental import pallas as pl
from jax.experimental.pallas import tpu as pltpu

F32 = jnp.float32
BF16 = jnp.bfloat16
EPS = 1e-6
NEG_INF = float("-inf")
LOG2E = 1.4426950408889634

V7X_VMEM_LIMIT_BYTES = 60 * 1024 * 1024

QK_DIM = 64
HEAD_DIM = 128
N_HEADS = 8
CONV_W = 4
CHUNK = 64
GDN_ROWS = 256

def _dot(a, b):
    return jnp.dot(a, b, preferred_element_type=F32)


def _dot_nt(a, b):
    return lax.dot_general(a, b, (((1,), (1,)), ((), ())), preferred_element_type=F32)


def _split3(x):
    hi = x.astype(BF16)
    r = x - hi.astype(F32)
    mid = r.astype(BF16)
    lo = (r - mid.astype(F32)).astype(BF16)
    return hi, mid, lo


def _dot_sel_l(sel, x):
    hi, mid, lo = _split3(x)
    return _dot(sel, hi) + _dot(sel, mid) + _dot(sel, lo)


def _dot_sel_r(x, sel):
    hi, mid, lo = _split3(x)
    return _dot(hi, sel) + _dot(mid, sel) + _dot(lo, sel)


def _dot_hp(a, b):
    ah = a.astype(BF16)
    al = (a - ah.astype(F32)).astype(BF16)
    bh = b.astype(BF16)
    bl = (b - bh.astype(F32)).astype(BF16)
    return _dot(ah, bh) + _dot(ah, bl) + _dot(al, bh)


def _rms(x, w):
    return x * lax.rsqrt(jnp.mean(x * x, axis=-1, keepdims=True) + EPS) * w


def _silu(x):
    return x * jax.nn.sigmoid(x)


def _softplus(x):
    return jnp.maximum(x, 0.0) + jnp.log1p(jnp.exp(-jnp.abs(x)))


def _lambda_full(lam_ref, lam_init):
    l = lam_ref[...]
    a = jnp.sum(l[0:1] * l[1:2], axis=-1, keepdims=True)
    b = jnp.sum(l[2:3] * l[3:4], axis=-1, keepdims=True)
    return jnp.exp(a) - jnp.exp(b) + lam_init


def _idiv(x, n):
    assert n & (n - 1) == 0
    return x >> (n.bit_length() - 1)


def _imod(x, n):
    assert n & (n - 1) == 0
    return x & (n - 1)


def _pow2_neg(k):
    return lax.bitcast_convert_type((127 - k) << 23, F32)


def _ffn_kernel(*refs, nf, emit_bf16, emit_next):
    x_ref, prew_ref, postw_ref, wg_ref, wu_ref, wd_ref = refs[:6]
    rest = list(refs[6:])
    nextw_ref = rest.pop(0) if emit_next else None
    o_ref = rest.pop(0)
    cast_refs = [rest.pop(0) for _ in range(3)] if emit_bf16 else None
    hn_ref = rest.pop(0) if emit_next else None
    h_ref, = rest
    f = pl.program_id(1)

    @pl.when(f == 0)
    def _():
        h_ref[...] = _rms(x_ref[...], prew_ref[...]).astype(BF16)
        o_ref[...] = jnp.zeros_like(o_ref)

    wg, wu, wd = wg_ref[...], wu_ref[...], wd_ref[...]
    if emit_bf16:
        wg, wu, wd = wg.astype(BF16), wu.astype(BF16), wd.astype(BF16)
        cast_refs[0][...], cast_refs[1][...], cast_refs[2][...] = wg, wu, wd
    h = h_ref[...]
    g = _dot(h, wg)
    u = _dot(h, wu)
    o_ref[...] += _dot((_silu(g) * u).astype(BF16), wd)

    @pl.when(f == nf - 1)
    def _():
        y = x_ref[...] + 0.5 * _rms(o_ref[...], postw_ref[...])
        o_ref[...] = y
        if emit_next:
            hn_ref[...] = _rms(y, nextw_ref[...]).astype(BF16)


def _ffn(x, pre_w, post_w, wg, wu, wd, *, tm, tf, emit_bf16=False, next_pre_w=None):
    m, d = x.shape
    ff = wg.shape[1]
    nf = ff // tf
    emit_next = next_pre_w is not None
    assert not emit_bf16 or m == tm
    gu_spec = pl.BlockSpec((d, tf), lambda i, f: (0, f))
    dn_spec = pl.BlockSpec((tf, d), lambda i, f: (f, 0))
    row_spec = pl.BlockSpec((tm, d), lambda i, f: (i, 0))
    vec_spec = pl.BlockSpec((1, d), lambda i, f: (0, 0))
    out_specs = [row_spec]
    out_shape = [jax.ShapeDtypeStruct((m, d), F32)]
    if emit_bf16:
        out_specs += [gu_spec, gu_spec, dn_spec]
        out_shape += [jax.ShapeDtypeStruct(w.shape, BF16) for w in (wg, wu, wd)]
    if emit_next:
        out_specs.append(row_spec)
        out_shape.append(jax.ShapeDtypeStruct((m, d), BF16))
    outs = pl.pallas_call(
        functools.partial(_ffn_kernel, nf=nf, emit_bf16=emit_bf16, emit_next=emit_next),
        grid=(m // tm, nf),
        in_specs=[row_spec, vec_spec, vec_spec, gu_spec, gu_spec, dn_spec] + [vec_spec] * emit_next,
        out_specs=out_specs,
        out_shape=out_shape,
        scratch_shapes=[pltpu.VMEM((tm, d), BF16)],
        compiler_params=pltpu.CompilerParams(
            dimension_semantics=("parallel", "arbitrary"), vmem_limit_bytes=V7X_VMEM_LIMIT_BYTES),
        name="ffn",
    )(x, pre_w, post_w, wg, wu, wd, *([next_pre_w] * emit_next))
    return outs if len(outs) > 1 else outs[0]


def _inproj_kernel(x_ref, prew_ref, wt_ref, wgt_ref, o_ref, og_ref, ko_ref, vo_ref, *rest, emit_bf16):
    normed = x_ref.dtype == BF16
    h_ref = x_ref if normed else rest[-1]
    j = pl.program_id(1)

    @pl.when(j == 0)
    def _():
        if not normed:
            h_ref[...] = _rms(x_ref[...], prew_ref[...]).astype(BF16)
        og_ref[...] = _dot_nt(h_ref[...], wgt_ref[...])

    wt = wt_ref[...]
    if emit_bf16:
        wt = wt.astype(BF16)
        rest[0][...] = wt
    y = _dot_nt(h_ref[...], wt)
    o_ref[...] = y

    @pl.when(j == 1)
    def _():
        ko_ref[...] = y

    @pl.when(j == 2)
    def _():
        vo_ref[...] = y


def _inproj(x, pre_w, w_main_t, w_gates_t, *, tm, tn, n, layer=None, emit_bf16=False):
    m, d = x.shape
    ng = w_gates_t.shape[0]
    assert not emit_bf16 or m == tm
    w_spec = pl.BlockSpec((tn, d), lambda i, j: (j, 0))
    w_in_spec = w_spec if layer is None else pl.BlockSpec((None, tn, d), lambda i, j: (layer, j, 0))
    row_spec = pl.BlockSpec((tm, tn), lambda i, j: (i, 0))
    out_specs = [pl.BlockSpec((tm, tn), lambda i, j: (i, j)), pl.BlockSpec((tm, ng), lambda i, j: (i, 0)),
                 row_spec, row_spec]
    out_shape = [jax.ShapeDtypeStruct((m, n), F32), jax.ShapeDtypeStruct((m, ng), F32),
                 jax.ShapeDtypeStruct((m, tn), F32), jax.ShapeDtypeStruct((m, tn), F32)]
    if emit_bf16:
        out_specs.append(w_spec)
        out_shape.append(jax.ShapeDtypeStruct((n, d), BF16))
    return pl.pallas_call(
        functools.partial(_inproj_kernel, emit_bf16=emit_bf16),
        grid=(m // tm, n // tn),
        in_specs=[
            pl.BlockSpec((tm, d), lambda i, j: (i, 0),
                         pipeline_mode=pl.Buffered(1 if tm * d * x.dtype.itemsize >= 8 * 1024 * 1024 else 2)),
            pl.BlockSpec((1, d), lambda i, j: (0, 0)),
            w_in_spec,
            pl.BlockSpec((ng, d), lambda i, j: (0, 0)),
        ],
        out_specs=out_specs,
        out_shape=out_shape,
        scratch_shapes=[pltpu.VMEM((tm, d), BF16)],
        compiler_params=pltpu.CompilerParams(
            dimension_semantics=("parallel", "arbitrary"), vmem_limit_bytes=V7X_VMEM_LIMIT_BYTES),
        name="inproj",
    )(x, pre_w, w_main_t, w_gates_t)


def _outproj_kernel(x_ref, oa_ref, ob_ref, wa_ref, wb_ref, postw_ref, o_ref, *rest, emit_bf16):
    wa, wb = wa_ref[...], wb_ref[...]
    if emit_bf16:
        wa, wb = wa.astype(BF16), wb.astype(BF16)
        ka = wa.shape[0]
        rest[0][:ka, :] = wa
        rest[0][ka:, :] = wb
    y = _dot(oa_ref[...], wa) + _dot(ob_ref[...], wb)
    o_ref[...] = x_ref[...] + _rms(y, postw_ref[...])


def _outproj(x, oa, ob, w_out, post_w, *, tm, emit_bf16=False):
    m, d = x.shape
    ka = oa.shape[1]
    assert not emit_bf16 or m == tm
    out_specs = [pl.BlockSpec((tm, d), lambda i: (i, 0))]
    out_shape = [jax.ShapeDtypeStruct((m, d), F32)]
    if emit_bf16:
        out_specs.append(pl.BlockSpec(w_out.shape, lambda i: (0, 0)))
        out_shape.append(jax.ShapeDtypeStruct(w_out.shape, BF16))
    outs = pl.pallas_call(
        functools.partial(_outproj_kernel, emit_bf16=emit_bf16),
        grid=(m // tm,),
        in_specs=[
            pl.BlockSpec((tm, d), lambda i: (i, 0)),
            pl.BlockSpec((tm, ka), lambda i: (i, 0)),
            pl.BlockSpec((tm, ka), lambda i: (i, 0)),
            pl.BlockSpec((ka, d), lambda i: (0, 0)),
            pl.BlockSpec((ka, d), lambda i: (1, 0)),
            pl.BlockSpec((1, d), lambda i: (0, 0)),
        ],
        out_specs=out_specs,
        out_shape=out_shape,
        compiler_params=pltpu.CompilerParams(
            dimension_semantics=("parallel",), vmem_limit_bytes=V7X_VMEM_LIMIT_BYTES),
        name="outproj",
    )(x, oa, ob, w_out, w_out, post_w)
    return outs if emit_bf16 else outs[0]


def _attn_prompt_kernel(lam_ref, subw_ref, q_ref, k_ref, v_ref, o_ref,
                        kb_ref, vt_ref, bias_ref, *, t, tq, hpp, lam_init):
    hg = pl.program_id(1)
    nq = t // tq
    lam = _lambda_full(lam_ref, lam_init)
    lane = lax.broadcasted_iota(jnp.int32, (1, HEAD_DIM), 1)
    first_map = lane < QK_DIM
    kk = lax.broadcasted_iota(jnp.int32, (tq, tq), 0)
    qq = lax.broadcasted_iota(jnp.int32, (tq, tq), 1)
    slopes = [_pow2_neg(jnp.full((1, 1), hg * hpp + hh + 1, jnp.int32)) * LOG2E for hh in range(hpp)]
    er = lax.broadcasted_iota(jnp.int32, (HEAD_DIM, HEAD_DIM), 0)
    eye_b = (er == lax.broadcasted_iota(jnp.int32, (HEAD_DIM, HEAD_DIM), 1)).astype(BF16)
    for hh in range(hpp):
        cols = slice(hh * HEAD_DIM, (hh + 1) * HEAD_DIM)
        b = slopes[hh] * (kk - qq).astype(F32)
        bias_ref[hh, 0] = b
        bias_ref[hh, 1] = jnp.where(qq >= kk, b, NEG_INF)
        vt_ref[hh] = _dot_nt(eye_b, v_ref[:, cols].astype(BF16)).astype(BF16)
    kb_ref[...] = k_ref[...].astype(BF16)
    scale = QK_DIM ** -0.5 * LOG2E
    subw_col = subw_ref[...]

    def q_block(i, _):
        rows_i = pl.ds(pl.multiple_of(i * tq, tq), tq)
        qs = []
        for hh in range(hpp):
            q = q_ref[rows_i, hh * HEAD_DIM:(hh + 1) * HEAD_DIM] * scale
            qs.append(jnp.concatenate(
                [jnp.where(first_map, q, 0.0), jnp.where(first_map, 0.0, q)], axis=0).astype(BF16))

        def scores(j):
            rows_j = pl.ds(pl.multiple_of(j * tq, tq), tq)
            diag = jnp.asarray(j == i, jnp.int32)
            out = []
            for hh in range(hpp):
                b = bias_ref[hh, diag]
                out.append(_dot_nt(kb_ref[rows_j, hh * HEAD_DIM:(hh + 1) * HEAD_DIM], qs[hh])
                           + jnp.concatenate([b, b], axis=1))
            return tuple(out)

        def consume(j, sts, state):
            rows_j = pl.ds(pl.multiple_of(j * tq, tq), tq)
            blocks_away = jnp.asarray((i - j) * tq, F32)
            hs = range(hpp)
            far = [slopes[hh] * blocks_away for hh in hs]
            m_new = [jnp.maximum(state[hh][0], jnp.max(sts[hh], axis=0, keepdims=True) - far[hh])
                     for hh in hs]
            p = [jnp.exp2(sts[hh] - (far[hh] + m_new[hh])) for hh in hs]
            alpha = [jnp.exp2(state[hh][0] - m_new[hh]) for hh in hs]
            l_new = [alpha[hh] * state[hh][1] + jnp.sum(p[hh], axis=0, keepdims=True) for hh in hs]
            pv = [_dot(vt_ref[hh, :, rows_j], p[hh].astype(BF16)) for hh in hs]
            return tuple((m_new[hh], l_new[hh], alpha[hh] * state[hh][2] + pv[hh]) for hh in hs)

        def step(j, carry):
            sts, state = carry
            nxt = scores(j + 1)
            return nxt, consume(j, sts, state)

        init = tuple((jnp.full((1, 2 * tq), NEG_INF, F32), jnp.zeros((1, 2 * tq), F32),
                      jnp.zeros((HEAD_DIM, 2 * tq), F32)) for _ in range(hpp))
        sts, state = lax.fori_loop(0, i, step, (scores(0), init))
        state = consume(i, sts, state)
        for hh in range(hpp):
            _, l_fin, acc_fin = state[hh]
            ot = acc_fin / l_fin
            dt = ot[:, :tq] - lam * ot[:, tq:]
            dt = dt * lax.rsqrt(jnp.mean(dt * dt, axis=0, keepdims=True) + EPS) * subw_col
            o_ref[rows_i, hh * HEAD_DIM:(hh + 1) * HEAD_DIM] = (
                (dt * (1.0 - lam_init)).T.astype(o_ref.dtype))
        return 0

    lax.fori_loop(0, nq, q_block, 0)


def _attn_prompt(lam_rows, subln_col, proj, k, v, *, batch, t, lam_init, tq, hpp):
    m = proj.shape[0]
    ng = N_HEADS // hpp
    w = hpp * HEAD_DIM
    return pl.pallas_call(
        functools.partial(_attn_prompt_kernel, t=t, tq=tq, hpp=hpp, lam_init=lam_init),
        grid=(batch, ng),
        in_specs=[
            pl.BlockSpec((4, QK_DIM), lambda b, g: (0, 0)),
            pl.BlockSpec((HEAD_DIM, 1), lambda b, g: (0, 0)),
            pl.BlockSpec((t, w), lambda b, g: (b, g)),
            pl.BlockSpec((t, w), lambda b, g: (b, g)),
            pl.BlockSpec((t, w), lambda b, g: (b, g)),
        ],
        out_specs=pl.BlockSpec((t, w), lambda b, g: (b, g)),
        out_shape=jax.ShapeDtypeStruct((m, N_HEADS * HEAD_DIM), BF16),
        scratch_shapes=[
            pltpu.VMEM((t, w), BF16),
            pltpu.VMEM((hpp, HEAD_DIM, t), BF16),
            pltpu.VMEM((hpp, 2, tq, tq), F32),
        ],
        compiler_params=pltpu.CompilerParams(
            dimension_semantics=("parallel", "parallel"), vmem_limit_bytes=V7X_VMEM_LIMIT_BYTES),
        name="attn_prompt",
    )(lam_rows, subln_col, proj, k, v)


def _attn_sample_kernel(pt_ref, lam_ref, subw_ref, q_ref, kn_ref, vn_ref, *rest,
                        n_pages, page, tn, group, lam_init):
    del pt_ref
    kc_refs = rest[:group]
    vc_refs = rest[group:2 * group]
    o_ref, qb_ref, bias_ref, m_ref, l_ref, acc_ref = rest[2 * group:]
    p = pl.program_id(1)
    nrow = 2 * tn * N_HEADS
    ncol = page * N_HEADS
    past = n_pages * page
    row = lax.broadcasted_iota(jnp.int32, (nrow, 1), 0)
    slope = _pow2_neg(_imod(row, N_HEADS) + 1) * LOG2E

    @pl.when(p == 0)
    def _():
        q = q_ref[0] * (QK_DIM ** -0.5 * LOG2E)
        lane = lax.broadcasted_iota(jnp.int32, (1, HEAD_DIM), 1)
        first_map = lane < QK_DIM
        qb_ref[...] = jnp.concatenate(
            [jnp.where(first_map, q, 0.0), jnp.where(first_map, 0.0, q)], axis=0).astype(BF16)
        rr = lax.broadcasted_iota(jnp.int32, (nrow, ncol), 0)
        cc = lax.broadcasted_iota(jnp.int32, (nrow, ncol), 1)
        tq_ = _idiv(_imod(rr, tn * N_HEADS), N_HEADS)
        same_head = _imod(rr, N_HEADS) == _imod(cc, N_HEADS)
        bias_ref[...] = jnp.where(same_head, slope * (_idiv(cc, N_HEADS) - tq_).astype(F32), NEG_INF)
        m_ref[...] = jnp.full(m_ref.shape, NEG_INF, F32)
        l_ref[...] = jnp.zeros(l_ref.shape, F32)
        acc_ref[...] = jnp.zeros(acc_ref.shape, F32)

    def lane_fold(x, op):
        out = x[:, :HEAD_DIM]
        for i in range(1, x.shape[1] // HEAD_DIM):
            out = op(out, x[:, i * HEAD_DIM:(i + 1) * HEAD_DIM])
        return out

    def update(scores, offs, vals):
        m_old = m_ref[...]
        m_loc = lane_fold(scores[0], jnp.maximum) - offs[0]
        for s, off in zip(scores[1:], offs[1:]):
            m_loc = jnp.maximum(m_loc, lane_fold(s, jnp.maximum) - off)
        m_new = jnp.maximum(m_old, jnp.max(m_loc, axis=-1, keepdims=True))
        alpha = jnp.exp2(m_old - m_new)
        pes = [jnp.exp2(s - (off + m_new)) for s, off in zip(scores, offs)]
        l_part = lane_fold(pes[0], jnp.add)
        for pe in pes[1:]:
            l_part = l_part + lane_fold(pe, jnp.add)
        pv = _dot(jnp.concatenate([pe.astype(BF16) for pe in pes], axis=1),
                  jnp.concatenate(vals, axis=0))
        l_ref[...] = alpha * l_ref[...] + l_part
        acc_ref[...] = alpha * acc_ref[...] + pv
        m_ref[...] = m_new

    qb = qb_ref[...]
    bias = bias_ref[...]
    scores, offs, vals = [], [], []
    for g in range(group):
        kf = kc_refs[g][0].reshape(ncol, HEAD_DIM).astype(BF16)
        vals.append(vc_refs[g][0].reshape(ncol, HEAD_DIM).astype(BF16))
        scores.append(_dot_nt(qb, kf) + bias)
        offs.append(slope * jnp.asarray(past - (p * group + g) * page, F32))
    update(scores, offs, vals)

    @pl.when(p == n_pages // group - 1)
    def _():
        ncn = tn * N_HEADS
        rr = lax.broadcasted_iota(jnp.int32, (nrow, HEAD_DIM), 0)
        cc = lax.broadcasted_iota(jnp.int32, (nrow, HEAD_DIM), 1)
        tq_ = _idiv(_imod(rr, ncn), N_HEADS)
        tk_ = _idiv(cc, N_HEADS)
        ok = (cc < ncn) & (_imod(rr, N_HEADS) == _imod(cc, N_HEADS)) & (tk_ <= tq_)
        bias_new = jnp.where(ok, slope * (tk_ - tq_).astype(F32), NEG_INF)
        pad = jnp.zeros((HEAD_DIM - ncn, HEAD_DIM), BF16)
        kn = jnp.concatenate([kn_ref[0].astype(BF16), pad], axis=0)
        vn = jnp.concatenate([vn_ref[0].astype(BF16), pad], axis=0)
        update([_dot_nt(qb, kn) + bias_new], [jnp.zeros((nrow, 1), F32)], [vn])
        lam = _lambda_full(lam_ref, lam_init)
        o = acc_ref[...] / jnp.sum(l_ref[...], axis=-1, keepdims=True)
        d = o[:ncn] - lam * o[ncn:]
        o_ref[0] = (_rms(d, subw_ref[...]) * (1.0 - lam_init)).astype(o_ref.dtype)


def _attn_sample(page_table, lam_rows, subln_w, q, k_new, v_new, cache_k, cache_v, *, lam_init, group):
    n_seq, n_pages = page_table.shape
    page = cache_k.shape[1]
    tn = q.shape[1] // N_HEADS
    nrow = 2 * tn * N_HEADS
    ncol = page * N_HEADS
    assert n_pages % group == 0

    def page_spec(g):
        return pl.BlockSpec((1, page, N_HEADS, HEAD_DIM),
                            lambda b, p, pt: (pt[b, p * group + g], 0, 0, 0))

    new_spec = pl.BlockSpec((1, tn * N_HEADS, HEAD_DIM), lambda b, p, pt: (b, 0, 0))
    grid_spec = pltpu.PrefetchScalarGridSpec(
        num_scalar_prefetch=1,
        grid=(n_seq, n_pages // group),
        in_specs=[
            pl.BlockSpec((4, QK_DIM), lambda b, p, pt: (0, 0)),
            pl.BlockSpec((1, HEAD_DIM), lambda b, p, pt: (0, 0)),
            new_spec, new_spec, new_spec,
        ] + [page_spec(g) for g in range(group)] * 2,
        out_specs=new_spec,
        scratch_shapes=[
            pltpu.VMEM((nrow, HEAD_DIM), BF16),
            pltpu.VMEM((nrow, ncol), F32),
            pltpu.VMEM((nrow, 1), F32),
            pltpu.VMEM((nrow, HEAD_DIM), F32),
            pltpu.VMEM((nrow, HEAD_DIM), F32),
        ],
    )
    return pl.pallas_call(
        functools.partial(_attn_sample_kernel, n_pages=n_pages, page=page, tn=tn, group=group,
                          lam_init=lam_init),
        grid_spec=grid_spec,
        out_shape=jax.ShapeDtypeStruct((n_seq, tn * N_HEADS, HEAD_DIM), BF16),
        compiler_params=pltpu.CompilerParams(
            dimension_semantics=("parallel", "arbitrary"), vmem_limit_bytes=V7X_VMEM_LIMIT_BYTES),
        name="attn_sample",
    )(page_table, lam_rows, subln_w, q, k_new, v_new, *([cache_k] * group), *([cache_v] * group))


def _gdn_n_masks(chunk):
    return 3 + (chunk.bit_length() - 1)


def _gdn_fill_masks(mask_ref, *, rows, chunk):
    ri = lax.broadcasted_iota(jnp.int32, (rows, rows), 0)
    ci = lax.broadcasted_iota(jnp.int32, (rows, rows), 1)
    same = _idiv(ri, chunk) == _idiv(ci, chunk)
    mask_ref[0] = (same & (ri >= ci)).astype(F32)
    mask_ref[1] = (same & (ri > ci)).astype(F32)
    mask_ref[2] = (ri == ci).astype(F32)
    s, k = 1, 3
    while s < chunk:
        mask_ref[k] = ((_idiv(ri, 2 * s) == _idiv(ci, 2 * s)) & (_imod(_idiv(ri, s), 2) == 1)
                       & (_imod(_idiv(ci, s), 2) == 0)).astype(F32)
        s, k = 2 * s, k + 1


def _gdn_intra_all(act, gates, alog_row, dtb_row, mask_ref, u_ref, w_ref, qg_ref, kdt_ref, qkc_ref,
                   egl_ref, *, rows, chunk):
    n_chunks = rows // chunk
    n_levels = _gdn_n_masks(chunk) - 3
    beta_all = jax.nn.sigmoid(gates)
    g_all = -jnp.exp(alog_row) * _softplus(gates + dtb_row)
    gcum_all = _dot_sel_l(mask_ref[0].astype(BF16), g_all)
    gtot_all = jnp.concatenate(
        [jnp.broadcast_to(gcum_all[(c + 1) * chunk - 1:(c + 1) * chunk], (chunk, HEAD_DIM))
         for c in range(n_chunks)], axis=0)
    gcum_t = gcum_all.T
    fr = lax.broadcasted_iota(jnp.int32, (rows, chunk), 0)
    fc = lax.broadcasted_iota(jnp.int32, (rows, chunk), 1)
    fold = (_imod(fr, chunk) == fc).astype(BF16)
    er = lax.broadcasted_iota(jnp.int32, (HEAD_DIM, HEAD_DIM), 0)
    eye_b = (er == lax.broadcasted_iota(jnp.int32, (HEAD_DIM, HEAD_DIM), 1)).astype(BF16)

    def lanes(col):
        return jnp.broadcast_to(col, (rows, HEAD_DIM))

    def square(x):
        if rows >= HEAD_DIM:
            return jnp.concatenate([x] * (rows // HEAD_DIM), axis=1)
        return x[:, :rows]

    heads = range(N_HEADS)
    egl_rows, lmats, lbs, tinvs, rhss = [], [], [], [], []
    for h in heads:
        cols = slice(h * HEAD_DIM, (h + 1) * HEAD_DIM)
        aq = act[:, h * HEAD_DIM:(h + 1) * HEAD_DIM]
        ak = act[:, (N_HEADS + h) * HEAD_DIM:(N_HEADS + h + 1) * HEAD_DIM]
        av = act[:, (2 * N_HEADS + h) * HEAD_DIM:(2 * N_HEADS + h + 1) * HEAD_DIM]
        qn = aq * lax.rsqrt(jnp.sum(aq * aq, axis=-1, keepdims=True) + 1e-6) * (HEAD_DIM ** -0.5)
        kn = ak * lax.rsqrt(jnp.sum(ak * ak, axis=-1, keepdims=True) + 1e-6)
        beta = lanes(beta_all[:, h:h + 1])
        g_cum = lanes(gcum_all[:, N_HEADS + h:N_HEADS + h + 1])
        g_tot = lanes(gtot_all[:, N_HEADS + h:N_HEADS + h + 1])
        g_row = gcum_t[N_HEADS + h:N_HEADS + h + 1, :]
        decay = jnp.exp(jnp.minimum(square(g_cum) - g_row, 0.0)) * mask_ref[0]

        kbeta = kn * beta
        knb = kn.astype(BF16)
        lmat = mask_ref[1] * (_dot_nt(kbeta.astype(BF16), knb) * decay)
        lmats.append(lmat)
        lbs.append(lmat.astype(BF16))
        tinvs.append(mask_ref[2] - lmat * mask_ref[3])
        e_cum = jnp.exp(g_cum)
        rhss.append(jnp.concatenate([av * beta, kbeta * e_cum], axis=1).astype(BF16))
        qk = (_dot_nt(qn.astype(BF16), knb) * decay).astype(BF16)
        qkc_ref[h] = _dot(qk, fold).astype(BF16)
        qg_ref[:, cols] = (qn * e_cum).astype(BF16)
        kd = (kn * jnp.exp(g_tot - g_cum)).astype(BF16)
        kdt_ref[h] = _dot_nt(eye_b, kd).astype(BF16)
        egl = jnp.exp(g_tot)
        egl_rows.append(jnp.concatenate(
            [egl[c * chunk:c * chunk + 1] for c in range(n_chunks)]
            + [jnp.zeros((8 - n_chunks, HEAD_DIM), F32)] * (n_chunks < 8), axis=0))
    egl_ref[0] = jnp.concatenate(egl_rows, axis=1)

    for k in range(1, n_levels):
        tbs = [tinvs[h].astype(BF16) for h in heads]
        mids = [_dot(tbs[h], lbs[h]).astype(BF16) for h in heads]
        tinvs = [tinvs[h] - mask_ref[3 + k] * _dot(mids[h], tbs[h]) for h in heads]
    resids = [mask_ref[2] - tinvs[h] - _dot_hp(lmats[h], tinvs[h]) for h in heads]
    tinvs = [tinvs[h] + _dot(tinvs[h].astype(BF16), resids[h].astype(BF16)) for h in heads]
    for h in heads:
        cols = slice(h * HEAD_DIM, (h + 1) * HEAD_DIM)
        uw = _dot(tinvs[h].astype(BF16), rhss[h])
        u_ref[:, cols] = uw[:, :HEAD_DIM]
        w_ref[:, cols] = uw[:, HEAD_DIM:].astype(BF16)


def _gdn_intra_prompt_kernel(x_ref, gates_ref, cw_ref, alog_ref, dtb_ref,
                             u_ref, w_ref, qg_ref, kdt_ref, qkc_ref, egl_ref, pad_ref, mask_ref,
                             *, rows, chunk):
    t = pl.program_id(1)

    @pl.when(t == 0)
    def _():
        pad_ref[0:8, :] = jnp.zeros((8, pad_ref.shape[1]), F32)
        _gdn_fill_masks(mask_ref, rows=rows, chunk=chunk)

    pad_ref[8:, :] = x_ref[...]
    cw = cw_ref[...]
    acc = pad_ref[pl.ds(8 - (CONV_W - 1), rows), :] * cw[0:1]
    for j in range(1, CONV_W):
        acc = acc + pad_ref[pl.ds(8 - (CONV_W - 1) + j, rows), :] * cw[j:j + 1]
    pad_ref[0:8, :] = x_ref[rows - 8:, :]
    _gdn_intra_all(_silu(acc), gates_ref[...], alog_ref[...], dtb_ref[...], mask_ref,
                   u_ref, w_ref, qg_ref, kdt_ref, qkc_ref, egl_ref, rows=rows, chunk=chunk)


def _gdn_intra_sample_kernel(x_ref, gates_ref, cw_ref, alog_ref, dtb_ref, buf_ref,
                             u_ref, w_ref, qg_ref, kdt_ref, qkc_ref, egl_ref, pad_ref, mask_ref,
                             *, n_seq, tn):
    seg = 8 + tn
    _gdn_fill_masks(mask_ref, rows=n_seq * tn, chunk=tn)
    cw = cw_ref[...]
    for s in range(n_seq):
        pad_ref[s * seg + 8 - (CONV_W - 1):s * seg + 8, :] = buf_ref[s]
        pad_ref[s * seg + 8:(s + 1) * seg, :] = x_ref[s * tn:(s + 1) * tn, :]
    pieces = []
    for s in range(n_seq):
        base = s * seg + 8 - (CONV_W - 1)
        acc = pad_ref[pl.ds(base, tn), :] * cw[0:1]
        for j in range(1, CONV_W):
            acc = acc + pad_ref[pl.ds(base + j, tn), :] * cw[j:j + 1]
        pieces.append(acc)
    _gdn_intra_all(_silu(jnp.concatenate(pieces, axis=0)), gates_ref[...], alog_ref[...], dtb_ref[...],
                   mask_ref, u_ref, w_ref, qg_ref, kdt_ref, qkc_ref, egl_ref, rows=n_seq * tn, chunk=tn)


def _intra_out(m, chunk, n_steps):
    width = N_HEADS * HEAD_DIM
    return [
        jax.ShapeDtypeStruct((m, width), F32),
        jax.ShapeDtypeStruct((m, width), BF16),
        jax.ShapeDtypeStruct((m, width), BF16),
        jax.ShapeDtypeStruct((N_HEADS, HEAD_DIM, m), BF16),
        jax.ShapeDtypeStruct((N_HEADS, m, chunk), BF16),
        jax.ShapeDtypeStruct((n_steps, 8, width), F32),
    ]


def _gdn_intra_prompt(proj, gates, conv_w, alog_row, dtb_row, *, batch, t):
    m = proj.shape[0]
    rows = GDN_ROWS
    nblk = t // rows
    width = N_HEADS * HEAD_DIM
    cwid = 3 * width

    def rb(b, i):
        return b * nblk + i

    in_specs = [
        pl.BlockSpec((rows, cwid), lambda b, i: (rb(b, i), 1)),
        pl.BlockSpec((rows, HEAD_DIM), lambda b, i: (rb(b, i), 0)),
        pl.BlockSpec((CONV_W, cwid), lambda b, i: (0, 0)),
        pl.BlockSpec((1, HEAD_DIM), lambda b, i: (0, 0)),
        pl.BlockSpec((1, HEAD_DIM), lambda b, i: (0, 0)),
    ]
    out_specs = [
        pl.BlockSpec((rows, width), lambda b, i: (rb(b, i), 0)),
        pl.BlockSpec((rows, width), lambda b, i: (rb(b, i), 0)),
        pl.BlockSpec((rows, width), lambda b, i: (rb(b, i), 0)),
        pl.BlockSpec((N_HEADS, HEAD_DIM, rows), lambda b, i: (0, 0, rb(b, i))),
        pl.BlockSpec((N_HEADS, rows, CHUNK), lambda b, i: (0, rb(b, i), 0)),
        pl.BlockSpec((1, 8, width), lambda b, i: (rb(b, i), 0, 0)),
    ]
    return pl.pallas_call(
        functools.partial(_gdn_intra_prompt_kernel, rows=rows, chunk=CHUNK),
        grid=(batch, nblk),
        in_specs=in_specs,
        out_specs=out_specs,
        out_shape=_intra_out(m, CHUNK, batch * nblk),
        scratch_shapes=[pltpu.VMEM((rows + 8, cwid), F32),
                        pltpu.VMEM((_gdn_n_masks(CHUNK), rows, rows), F32)],
        compiler_params=pltpu.CompilerParams(
            dimension_semantics=("parallel", "arbitrary"), vmem_limit_bytes=V7X_VMEM_LIMIT_BYTES),
        name="gdn_intra_prompt",
    )(proj, gates, conv_w, alog_row, dtb_row)


def _gdn_intra_sample(proj, gates, conv_w, alog_row, dtb_row, conv_state, *, n_seq, tn):
    m = proj.shape[0]
    width = N_HEADS * HEAD_DIM
    cwid = 3 * width
    in_specs = [
        pl.BlockSpec((m, cwid), lambda i: (0, 1)),
        pl.BlockSpec((m, HEAD_DIM), lambda i: (0, 0)),
        pl.BlockSpec((CONV_W, cwid), lambda i: (0, 0)),
        pl.BlockSpec((1, HEAD_DIM), lambda i: (0, 0)),
        pl.BlockSpec((1, HEAD_DIM), lambda i: (0, 0)),
        pl.BlockSpec((n_seq, CONV_W - 1, cwid), lambda i: (0, 0, 0)),
    ]
    out_specs = [
        pl.BlockSpec((m, width), lambda i: (0, 0)),
        pl.BlockSpec((m, width), lambda i: (0, 0)),
        pl.BlockSpec((m, width), lambda i: (0, 0)),
        pl.BlockSpec((N_HEADS, HEAD_DIM, m), lambda i: (0, 0, 0)),
        pl.BlockSpec((N_HEADS, m, tn), lambda i: (0, 0, 0)),
        pl.BlockSpec((1, 8, width), lambda i: (0, 0, 0)),
    ]
    return pl.pallas_call(
        functools.partial(_gdn_intra_sample_kernel, n_seq=n_seq, tn=tn),
        grid=(1,),
        in_specs=in_specs,
        out_specs=out_specs,
        out_shape=_intra_out(m, tn, 1),
        scratch_shapes=[pltpu.VMEM((n_seq * (8 + tn), cwid), F32),
                        pltpu.VMEM((_gdn_n_masks(tn), m, m), F32)],
        compiler_params=pltpu.CompilerParams(
            dimension_semantics=("arbitrary",), vmem_limit_bytes=V7X_VMEM_LIMIT_BYTES),
        name="gdn_intra_sample",
    )(proj, gates, conv_w, alog_row, dtb_row, conv_state)


def _gdn_scan_kernel(u_ref, w_ref, qg_ref, kdt_ref, qkc_ref, egl_ref, z_ref, s0_ref, dnw_ref,
                     o_ref, s_ref, *, chunk, n_chunks, seqs_per_step):
    t = pl.program_id(1)

    @pl.when(t == 0)
    def _():
        s_ref[...] = s0_ref[...]

    dnw = dnw_ref[...]
    for c in range(n_chunks):
        si = c if seqs_per_step > 1 else 0
        rows = slice(c * chunk, (c + 1) * chunk)
        heads = range(N_HEADS)
        hcols = [slice(h * HEAD_DIM, (h + 1) * HEAD_DIM) for h in heads]
        s_old = [s_ref[si, h] for h in heads]
        both = [_dot(jnp.concatenate([w_ref[rows, hcols[h]], qg_ref[rows, hcols[h]]], axis=0),
                     s_old[h].astype(BF16)) for h in heads]
        vn = [(u_ref[rows, hcols[h]] - both[h][:chunk]).astype(BF16) for h in heads]
        o = [both[h][chunk:] + _dot(qkc_ref[h, rows, :], vn[h]) for h in heads]
        for h in heads:
            s_ref[si, h] = s_old[h] * egl_ref[0, c:c + 1, hcols[h]] + _dot(kdt_ref[h, :, rows], vn[h])
        for h in heads:
            o_ref[rows, hcols[h]] = (_rms(o[h], dnw) * _silu(z_ref[rows, hcols[h]])).astype(o_ref.dtype)


def _gdn_scan(intra, z_src, z_col_block, s0, dnw, *, n_seq, t, chunk, chunks_per_step, seqs_per_step=1):
    u, w, qg, kdt, qkc, egl = intra
    m = u.shape[0]
    width = N_HEADS * HEAD_DIM
    rows = chunk * chunks_per_step
    if seqs_per_step > 1:
        assert seqs_per_step == n_seq == chunks_per_step and t == chunk
        nsteps, n_outer = 1, 1
    else:
        nsteps, n_outer = t // rows, n_seq

    def rb(b, i):
        return b * nsteps + i

    in_specs = [
        pl.BlockSpec((rows, width), lambda b, i: (rb(b, i), 0)),
        pl.BlockSpec((rows, width), lambda b, i: (rb(b, i), 0)),
        pl.BlockSpec((rows, width), lambda b, i: (rb(b, i), 0)),
        pl.BlockSpec((N_HEADS, HEAD_DIM, rows), lambda b, i: (0, 0, rb(b, i))),
        pl.BlockSpec((N_HEADS, rows, chunk), lambda b, i: (0, rb(b, i), 0)),
        pl.BlockSpec((1, 8, width), lambda b, i: (rb(b, i), 0, 0)),
        pl.BlockSpec((rows, width), lambda b, i: (rb(b, i), z_col_block)),
        pl.BlockSpec((seqs_per_step, N_HEADS, HEAD_DIM, HEAD_DIM), lambda b, i: (b, 0, 0, 0)),
        pl.BlockSpec((1, HEAD_DIM), lambda b, i: (0, 0)),
    ]
    out_specs = [
        pl.BlockSpec((rows, width), lambda b, i: (rb(b, i), 0)),
        pl.BlockSpec((seqs_per_step, N_HEADS, HEAD_DIM, HEAD_DIM), lambda b, i: (b, 0, 0, 0)),
    ]
    return pl.pallas_call(
        functools.partial(_gdn_scan_kernel, chunk=chunk, n_chunks=chunks_per_step,
                          seqs_per_step=seqs_per_step),
        grid=(n_outer, nsteps),
        in_specs=in_specs,
        out_specs=out_specs,
        out_shape=[jax.ShapeDtypeStruct((m, width), BF16),
                   jax.ShapeDtypeStruct((n_seq, N_HEADS, HEAD_DIM, HEAD_DIM), F32)],
        compiler_params=pltpu.CompilerParams(
            dimension_semantics=("parallel", "arbitrary"), vmem_limit_bytes=V7X_VMEM_LIMIT_BYTES),
        name="gdn_scan",
    )(u, w, qg, kdt, qkc, egl, z_src, s0, dnw)


def _pad_lanes(v, offset):
    out = jnp.zeros((1, HEAD_DIM), F32)
    return lax.dynamic_update_slice(out, v.reshape(1, -1).astype(F32), (0, offset))


def kernel(x_prompt, x_sample, cache_k, cache_v, state_ssm, state_conv, page_table, ffn1_pre_w, ffn1_post_w, ffn1_gate, ffn1_up, ffn1_down, mix_pre_w, mix_post_w, w_in, conv_w, a_log, dt_bias, delta_norm_w, lambda_q1, lambda_k1, lambda_q2, lambda_k2, subln_w, w_out, ffn2_pre_w, ffn2_post_w, ffn2_gate, ffn2_up, ffn2_down):
    depth = w_in.shape[0]
    batch, t, d = x_prompt.shape
    n_seq, tn, _ = x_sample.shape
    a_w = N_HEADS * HEAD_DIM
    n_main = 3 * a_w + 3 * a_w + a_w
    yp = x_prompt.reshape(batch * t, d)
    ys = x_sample.reshape(n_seq * tn, d)
    outs = [[] for _ in range(8)]
    w_in_t = jnp.swapaxes(w_in, 1, 2)
    for l in range(depth):
        lam_init = 0.8 - 0.6 * math.exp(-0.3 * l)
        lam_rows = jnp.stack([lambda_q1[l], lambda_k1[l], lambda_q2[l], lambda_k2[l]]).astype(F32)
        row = lambda v: v[l].reshape(1, -1).astype(F32)
        w_gates = jnp.pad(w_in_t[l, n_main:], ((0, HEAD_DIM - 2 * N_HEADS), (0, 0))).astype(BF16)
        alog_row = _pad_lanes(a_log[l], N_HEADS)
        dtb_row = _pad_lanes(dt_bias[l], N_HEADS)
        dnw = row(delta_norm_w)
        subw = row(subln_w)
        cw = conv_w[l].astype(F32)
        ms = n_seq * tn

        ys, wg1, wu1, wd1 = _ffn(ys, row(ffn1_pre_w), row(ffn1_post_w), ffn1_gate[l], ffn1_up[l],
                                 ffn1_down[l], tm=ms, tf=512, emit_bf16=True)
        yp, hp = _ffn(yp, row(ffn1_pre_w), row(ffn1_post_w), wg1, wu1, wd1, tm=512, tf=512,
                      next_pre_w=row(mix_pre_w))

        ps, gs, ks, vs, w_main = _inproj(ys, row(mix_pre_w), w_in_t, w_gates, tm=ms, tn=a_w, n=n_main,
                                         layer=l, emit_bf16=True)
        qs = ps[:, :a_w].reshape(n_seq, tn * N_HEADS, HEAD_DIM)
        ks = ks.reshape(n_seq, tn * N_HEADS, HEAD_DIM)
        vs = vs.reshape(n_seq, tn * N_HEADS, HEAD_DIM)
        oa_s = _attn_sample(page_table, lam_rows, subw, qs, ks, vs, cache_k[l], cache_v[l],
                            lam_init=lam_init, group=16)
        oa_s = oa_s.reshape(ms, a_w)
        intra_s = _gdn_intra_sample(ps, gs, cw, alog_row, dtb_row, state_conv[l], n_seq=n_seq, tn=tn)
        ob_s, s_s = _gdn_scan(intra_s, ps, 6, state_ssm[l].astype(F32), dnw, n_seq=n_seq, t=tn,
                              chunk=tn, chunks_per_step=n_seq, seqs_per_step=n_seq)
        ys, wo = _outproj(ys, oa_s, ob_s, w_out[l], row(mix_post_w), tm=ms, emit_bf16=True)

        pp, gp, kp, vp = _inproj(hp, row(mix_pre_w), w_main, w_gates, tm=1024, tn=a_w, n=n_main)
        oa_p = _attn_prompt(lam_rows, subw.reshape(HEAD_DIM, 1), pp, kp, vp, batch=batch, t=t,
                            lam_init=lam_init, tq=256, hpp=2)
        intra_p = _gdn_intra_prompt(pp, gp, cw, alog_row, dtb_row, batch=batch, t=t)
        zero_s = jnp.zeros((batch, N_HEADS, HEAD_DIM, HEAD_DIM), F32)
        ob_p, s_p = _gdn_scan(intra_p, pp, 6, zero_s, dnw, n_seq=batch, t=t, chunk=CHUNK,
                              chunks_per_step=GDN_ROWS // CHUNK)
        yp = _outproj(yp, oa_p, ob_p, wo, row(mix_post_w), tm=512)

        ys, wg2, wu2, wd2 = _ffn(ys, row(ffn2_pre_w), row(ffn2_post_w), ffn2_gate[l], ffn2_up[l],
                                 ffn2_down[l], tm=ms, tf=512, emit_bf16=True)
        yp = _ffn(yp, row(ffn2_pre_w), row(ffn2_post_w), wg2, wu2, wd2, tm=512, tf=512)

        ppb = pp.reshape(batch, t, -1)
        psb = ps.reshape(n_seq, tn, -1)
        outs[0].append(kp.reshape(batch, t, N_HEADS, HEAD_DIM))
        outs[1].append(vp.reshape(batch, t, N_HEADS, HEAD_DIM))
        outs[2].append(s_p.astype(state_ssm.dtype))
        outs[3].append(ppb[:, t - (CONV_W - 1):, 3 * a_w:6 * a_w])
        outs[4].append(ks.reshape(n_seq, tn, N_HEADS, HEAD_DIM))
        outs[5].append(vs.reshape(n_seq, tn, N_HEADS, HEAD_DIM))
        outs[6].append(s_s.astype(state_ssm.dtype))
        outs[7].append(jnp.concatenate([state_conv[l].astype(psb.dtype), psb[:, :, 3 * a_w:6 * a_w]],
                                       axis=1)[:, tn:])
    return (yp.reshape(batch, t, d), ys.reshape(n_seq, tn, d)) + tuple(jnp.stack(o) for o in outs)
```

```python
import functools
import math

import jax
import jax.numpy as jnp
from jax import lax
from jax.experimental import pallas as pl
from jax.experimental.pallas import tpu as pltpu

F32 = jnp.float32
BF16 = jnp.bfloat16
EPS = 1e-6
NEG_INF = float("-inf")
LOG2E = 1.4426950408889634

V7X_VMEM_LIMIT_BYTES = 60 * 1024 * 1024

QK_DIM = 64
HEAD_DIM = 128
N_HEADS = 8
CONV_W = 4
CHUNK = 64
GDN_ROWS = 128

def _dot(a, b):
    return jnp.dot(a, b, preferred_element_type=F32)


def _dot_nt(a, b):
    return lax.dot_general(a, b, (((1,), (1,)), ((), ())), preferred_element_type=F32)


def _split3(x):
    hi = x.astype(BF16)
    r = x - hi.astype(F32)
    mid = r.astype(BF16)
    lo = (r - mid.astype(F32)).astype(BF16)
    return hi, mid, lo


def _dot_sel_l(sel, x):
    hi, mid, lo = _split3(x)
    return _dot(sel, hi) + _dot(sel, mid) + _dot(sel, lo)


def _dot_sel_r(x, sel):
    hi, mid, lo = _split3(x)
    return _dot(hi, sel) + _dot(mid, sel) + _dot(lo, sel)


def _dot_hp(a, b):
    ah = a.astype(BF16)
    al = (a - ah.astype(F32)).astype(BF16)
    bh = b.astype(BF16)
    bl = (b - bh.astype(F32)).astype(BF16)
    return _dot(ah, bh) + _dot(ah, bl) + _dot(al, bh)


def _rms(x, w):
    return x * lax.rsqrt(jnp.mean(x * x, axis=-1, keepdims=True) + EPS) * w


def _silu(x):
    return x * jax.nn.sigmoid(x)


def _softplus(x):
    return jnp.maximum(x, 0.0) + jnp.log1p(jnp.exp(-jnp.abs(x)))


def _lambda_full(lam_ref, lam_init):
    l = lam_ref[...]
    a = jnp.sum(l[0:1] * l[1:2], axis=-1, keepdims=True)
    b = jnp.sum(l[2:3] * l[3:4], axis=-1, keepdims=True)
    return jnp.exp(a) - jnp.exp(b) + lam_init


def _idiv(x, n):
    assert n & (n - 1) == 0
    return x >> (n.bit_length() - 1)


def _imod(x, n):
    assert n & (n - 1) == 0
    return x & (n - 1)


def _pow2_neg(k):
    return lax.bitcast_convert_type((127 - k) << 23, F32)


def _ffn_kernel(*refs, nf, emit_bf16, emit_next, normed_in):
    x_ref, prew_ref, postw_ref, wg_ref, wu_ref, wd_ref = refs[:6]
    rest = list(refs[6:])
    nextw_ref = rest.pop(0) if emit_next else None
    hin_ref = rest.pop(0) if normed_in else None
    o_ref = rest.pop(0)
    cast_refs = [rest.pop(0) for _ in range(3)] if emit_bf16 else None
    hn_ref = rest.pop(0) if emit_next else None
    h_ref, = rest
    if normed_in:
        h_ref = hin_ref
    f = pl.program_id(1)

    @pl.when(f == 0)
    def _():
        if not normed_in:
            h_ref[...] = _rms(x_ref[...], prew_ref[...]).astype(BF16)
        o_ref[...] = jnp.zeros_like(o_ref)

    wg, wu, wd = wg_ref[...], wu_ref[...], wd_ref[...]
    if emit_bf16:
        wg, wu, wd = wg.astype(BF16), wu.astype(BF16), wd.astype(BF16)
        cast_refs[0][...], cast_refs[1][...], cast_refs[2][...] = wg, wu, wd
    h = h_ref[...]
    g = _dot(h, wg)
    u = _dot(h, wu)
    o_ref[...] += _dot((_silu(g) * u).astype(BF16), wd)

    @pl.when(f == nf - 1)
    def _():
        y = x_ref[...] + 0.5 * _rms(o_ref[...], postw_ref[...])
        o_ref[...] = y
        if emit_next:
            hn_ref[...] = _rms(y, nextw_ref[...]).astype(BF16)


def _ffn(x, pre_w, post_w, wg, wu, wd, *, tm, tf, emit_bf16=False, next_pre_w=None, h_in=None):
    m, d = x.shape
    ff = wg.shape[1]
    nf = ff // tf
    emit_next = next_pre_w is not None
    normed_in = h_in is not None
    assert not emit_bf16 or m == tm
    gu_spec = pl.BlockSpec((d, tf), lambda i, f: (0, f))
    dn_spec = pl.BlockSpec((tf, d), lambda i, f: (f, 0))
    row_spec = pl.BlockSpec((tm, d), lambda i, f: (i, 0))
    vec_spec = pl.BlockSpec((1, d), lambda i, f: (0, 0))
    out_specs = [row_spec]
    out_shape = [jax.ShapeDtypeStruct((m, d), F32)]
    if emit_bf16:
        out_specs += [gu_spec, gu_spec, dn_spec]
        out_shape += [jax.ShapeDtypeStruct(w.shape, BF16) for w in (wg, wu, wd)]
    if emit_next:
        out_specs.append(row_spec)
        out_shape.append(jax.ShapeDtypeStruct((m, d), BF16))
    outs = pl.pallas_call(
        functools.partial(_ffn_kernel, nf=nf, emit_bf16=emit_bf16, emit_next=emit_next, normed_in=normed_in),
        grid=(m // tm, nf),
        in_specs=([row_spec, vec_spec, vec_spec, gu_spec, gu_spec, dn_spec] + [vec_spec] * emit_next
                  + [row_spec] * normed_in),
        out_specs=out_specs,
        out_shape=out_shape,
        scratch_shapes=[pltpu.VMEM((tm, d), BF16)],
        compiler_params=pltpu.CompilerParams(
            dimension_semantics=("parallel", "arbitrary"), vmem_limit_bytes=V7X_VMEM_LIMIT_BYTES),
        name="ffn",
    )(x, pre_w, post_w, wg, wu, wd, *([next_pre_w] * emit_next), *([h_in] * normed_in))
    return outs if len(outs) > 1 else outs[0]


def _inproj_kernel(x_ref, prew_ref, wt_ref, wgt_ref, o_ref, og_ref, ko_ref, vo_ref, *rest, emit_bf16):
    normed = x_ref.dtype == BF16
    h_ref = x_ref if normed else rest[-1]
    j = pl.program_id(1)

    @pl.when(j == 0)
    def _():
        if not normed:
            h_ref[...] = _rms(x_ref[...], prew_ref[...]).astype(BF16)
        og_ref[...] = _dot_nt(h_ref[...], wgt_ref[...])

    wt = wt_ref[...]
    if emit_bf16:
        wt = wt.astype(BF16)
        rest[0][...] = wt
    y = _dot_nt(h_ref[...], wt)
    o_ref[...] = y

    @pl.when(j == 1)
    def _():
        ko_ref[...] = y

    @pl.when(j == 2)
    def _():
        vo_ref[...] = y


def _inproj(x, pre_w, w_main_t, w_gates_t, *, tm, tn, n, layer=None, emit_bf16=False):
    m, d = x.shape
    ng = w_gates_t.shape[0]
    assert not emit_bf16 or m == tm
    w_spec = pl.BlockSpec((tn, d), lambda i, j: (j, 0))
    w_in_spec = w_spec if layer is None else pl.BlockSpec((None, tn, d), lambda i, j: (layer, j, 0))
    row_spec = pl.BlockSpec((tm, tn), lambda i, j: (i, 0))
    out_specs = [pl.BlockSpec((tm, tn), lambda i, j: (i, j)), pl.BlockSpec((tm, ng), lambda i, j: (i, 0)),
                 row_spec, row_spec]
    out_shape = [jax.ShapeDtypeStruct((m, n), F32), jax.ShapeDtypeStruct((m, ng), F32),
                 jax.ShapeDtypeStruct((m, tn), F32), jax.ShapeDtypeStruct((m, tn), F32)]
    if emit_bf16:
        out_specs.append(w_spec)
        out_shape.append(jax.ShapeDtypeStruct((n, d), BF16))
    return pl.pallas_call(
        functools.partial(_inproj_kernel, emit_bf16=emit_bf16),
        grid=(m // tm, n // tn),
        in_specs=[
            pl.BlockSpec((tm, d), lambda i, j: (i, 0),
                         pipeline_mode=pl.Buffered(1 if tm * d * x.dtype.itemsize >= 8 * 1024 * 1024 else 2)),
            pl.BlockSpec((1, d), lambda i, j: (0, 0)),
            w_in_spec,
            pl.BlockSpec((ng, d), lambda i, j: (0, 0)),
        ],
        out_specs=out_specs,
        out_shape=out_shape,
        scratch_shapes=[pltpu.VMEM((tm, d), BF16)],
        compiler_params=pltpu.CompilerParams(
            dimension_semantics=("parallel", "arbitrary"), vmem_limit_bytes=V7X_VMEM_LIMIT_BYTES),
        name="inproj",
    )(x, pre_w, w_main_t, w_gates_t)


def _outproj_kernel(*refs, emit_bf16, emit_next):
    x_ref, oa_ref, ob_ref, wa_ref, wb_ref, postw_ref = refs[:6]
    rest = list(refs[6:])
    nextw_ref = rest.pop(0) if emit_next else None
    o_ref = rest.pop(0)
    wa, wb = wa_ref[...], wb_ref[...]
    if emit_bf16:
        wa, wb = wa.astype(BF16), wb.astype(BF16)
        ka = wa.shape[0]
        wo_ref = rest.pop(0)
        wo_ref[:ka, :] = wa
        wo_ref[ka:, :] = wb
    y = _dot(oa_ref[...], wa) + _dot(ob_ref[...], wb)
    out = x_ref[...] + _rms(y, postw_ref[...])
    o_ref[...] = out
    if emit_next:
        rest.pop(0)[...] = _rms(out, nextw_ref[...]).astype(BF16)


def _outproj(x, oa, ob, w_out, post_w, *, tm, emit_bf16=False, next_pre_w=None):
    m, d = x.shape
    ka = oa.shape[1]
    emit_next = next_pre_w is not None
    assert not emit_bf16 or m == tm
    out_specs = [pl.BlockSpec((tm, d), lambda i: (i, 0))]
    out_shape = [jax.ShapeDtypeStruct((m, d), F32)]
    if emit_bf16:
        out_specs.append(pl.BlockSpec(w_out.shape, lambda i: (0, 0)))
        out_shape.append(jax.ShapeDtypeStruct(w_out.shape, BF16))
    if emit_next:
        out_specs.append(pl.BlockSpec((tm, d), lambda i: (i, 0)))
        out_shape.append(jax.ShapeDtypeStruct((m, d), BF16))
    outs = pl.pallas_call(
        functools.partial(_outproj_kernel, emit_bf16=emit_bf16, emit_next=emit_next),
        grid=(m // tm,),
        in_specs=[
            pl.BlockSpec((tm, d), lambda i: (i, 0)),
            pl.BlockSpec((tm, ka), lambda i: (i, 0)),
            pl.BlockSpec((tm, ka), lambda i: (i, 0)),
            pl.BlockSpec((ka, d), lambda i: (0, 0)),
            pl.BlockSpec((ka, d), lambda i: (1, 0)),
            pl.BlockSpec((1, d), lambda i: (0, 0)),
        ] + [pl.BlockSpec((1, d), lambda i: (0, 0))] * emit_next,
        out_specs=out_specs,
        out_shape=out_shape,
        compiler_params=pltpu.CompilerParams(
            dimension_semantics=("parallel",), vmem_limit_bytes=V7X_VMEM_LIMIT_BYTES),
        name="outproj",
    )(x, oa, ob, w_out, w_out, post_w, *([next_pre_w] * emit_next))
    return outs if len(outs) > 1 else outs[0]


def _attn_prompt_kernel(lam_ref, subw_ref, q_ref, k_ref, v_ref, o_ref,
                        kb_ref, vt_ref, bias_ref, *, t, tq, hpp, lam_init):
    hg = pl.program_id(1)
    nq = t // tq
    lam = _lambda_full(lam_ref, lam_init)
    lane = lax.broadcasted_iota(jnp.int32, (1, HEAD_DIM), 1)
    first_map = lane < QK_DIM
    kk = lax.broadcasted_iota(jnp.int32, (tq, tq), 0)
    qq = lax.broadcasted_iota(jnp.int32, (tq, tq), 1)
    slopes = [_pow2_neg(jnp.full((1, 1), hg * hpp + hh + 1, jnp.int32)) * LOG2E for hh in range(hpp)]
    er = lax.broadcasted_iota(jnp.int32, (HEAD_DIM, HEAD_DIM), 0)
    eye_b = (er == lax.broadcasted_iota(jnp.int32, (HEAD_DIM, HEAD_DIM), 1)).astype(BF16)
    for hh in range(hpp):
        cols = slice(hh * HEAD_DIM, (hh + 1) * HEAD_DIM)
        b = slopes[hh] * (kk - qq).astype(F32)
        bias_ref[hh, 0] = b
        bias_ref[hh, 1] = jnp.where(qq >= kk, b, NEG_INF)
        vt_ref[hh] = _dot_nt(eye_b, v_ref[:, cols].astype(BF16)).astype(BF16)
    kb_ref[...] = k_ref[...].astype(BF16)
    scale = QK_DIM ** -0.5 * LOG2E
    subw_col = subw_ref[...]

    def q_block(i, _):
        rows_i = pl.ds(pl.multiple_of(i * tq, tq), tq)
        qs = []
        for hh in range(hpp):
            q = q_ref[rows_i, hh * HEAD_DIM:(hh + 1) * HEAD_DIM] * scale
            qs.append(jnp.concatenate(
                [jnp.where(first_map, q, 0.0), jnp.where(first_map, 0.0, q)], axis=0).astype(BF16))

        def scores(j):
            rows_j = pl.ds(pl.multiple_of(j * tq, tq), tq)
            diag = jnp.asarray(j == i, jnp.int32)
            out = []
            for hh in range(hpp):
                b = bias_ref[hh, diag]
                out.append(_dot_nt(kb_ref[rows_j, hh * HEAD_DIM:(hh + 1) * HEAD_DIM], qs[hh])
                           + jnp.concatenate([b, b], axis=1))
            return tuple(out)

        def consume(j, sts, state):
            rows_j = pl.ds(pl.multiple_of(j * tq, tq), tq)
            blocks_away = jnp.asarray((i - j) * tq, F32)
            hs = range(hpp)
            far = [slopes[hh] * blocks_away for hh in hs]
            m_new = [jnp.maximum(state[hh][0], jnp.max(sts[hh], axis=0, keepdims=True) - far[hh])
                     for hh in hs]
            p = [jnp.exp2(sts[hh] - (far[hh] + m_new[hh])) for hh in hs]
            alpha = [jnp.exp2(state[hh][0] - m_new[hh]) for hh in hs]
            l_new = [alpha[hh] * state[hh][1] + jnp.sum(p[hh], axis=0, keepdims=True) for hh in hs]
            pv = [_dot(vt_ref[hh, :, rows_j], p[hh].astype(BF16)) for hh in hs]
            return tuple((m_new[hh], l_new[hh], alpha[hh] * state[hh][2] + pv[hh]) for hh in hs)

        def step(j, carry):
            sts, state = carry
            nxt = scores(j + 1)
            return nxt, consume(j, sts, state)

        init = tuple((jnp.full((1, 2 * tq), NEG_INF, F32), jnp.zeros((1, 2 * tq), F32),
                      jnp.zeros((HEAD_DIM, 2 * tq), F32)) for _ in range(hpp))
        sts, state = lax.fori_loop(0, i, step, (scores(0), init))
        state = consume(i, sts, state)
        for hh in range(hpp):
            _, l_fin, acc_fin = state[hh]
            ot = acc_fin / l_fin
            dt = ot[:, :tq] - lam * ot[:, tq:]
            dt = dt * lax.rsqrt(jnp.mean(dt * dt, axis=0, keepdims=True) + EPS) * subw_col
            o_ref[rows_i, hh * HEAD_DIM:(hh + 1) * HEAD_DIM] = (
                (dt * (1.0 - lam_init)).T.astype(o_ref.dtype))
        return 0

    lax.fori_loop(0, nq, q_block, 0)


def _attn_prompt(lam_rows, subln_col, proj, k, v, *, batch, t, lam_init, tq, hpp):
    m = proj.shape[0]
    ng = N_HEADS // hpp
    w = hpp * HEAD_DIM
    return pl.pallas_call(
        functools.partial(_attn_prompt_kernel, t=t, tq=tq, hpp=hpp, lam_init=lam_init),
        grid=(batch, ng),
        in_specs=[
            pl.BlockSpec((4, QK_DIM), lambda b, g: (0, 0)),
            pl.BlockSpec((HEAD_DIM, 1), lambda b, g: (0, 0)),
            pl.BlockSpec((t, w), lambda b, g: (b, g)),
            pl.BlockSpec((t, w), lambda b, g: (b, g)),
            pl.BlockSpec((t, w), lambda b, g: (b, g)),
        ],
        out_specs=pl.BlockSpec((t, w), lambda b, g: (b, g)),
        out_shape=jax.ShapeDtypeStruct((m, N_HEADS * HEAD_DIM), BF16),
        scratch_shapes=[
            pltpu.VMEM((t, w), BF16),
            pltpu.VMEM((hpp, HEAD_DIM, t), BF16),
            pltpu.VMEM((hpp, 2, tq, tq), F32),
        ],
        compiler_params=pltpu.CompilerParams(
            dimension_semantics=("parallel", "parallel"), vmem_limit_bytes=V7X_VMEM_LIMIT_BYTES),
        name="attn_prompt",
    )(lam_rows, subln_col, proj, k, v)


def _attn_sample_kernel(pt_ref, lam_ref, subw_ref, q_ref, kn_ref, vn_ref, *rest,
                        n_pages, page, tn, group, lam_init):
    del pt_ref
    kc_refs = rest[:group]
    vc_refs = rest[group:2 * group]
    o_ref, qb_ref, bias_ref, m_ref, l_ref, acc_ref = rest[2 * group:]
    p = pl.program_id(1)
    nrow = 2 * tn * N_HEADS
    ncol = page * N_HEADS
    past = n_pages * page
    row = lax.broadcasted_iota(jnp.int32, (nrow, 1), 0)
    slope = _pow2_neg(_imod(row, N_HEADS) + 1) * LOG2E

    @pl.when(p == 0)
    def _():
        q = q_ref[0] * (QK_DIM ** -0.5 * LOG2E)
        lane = lax.broadcasted_iota(jnp.int32, (1, HEAD_DIM), 1)
        first_map = lane < QK_DIM
        qb_ref[...] = jnp.concatenate(
            [jnp.where(first_map, q, 0.0), jnp.where(first_map, 0.0, q)], axis=0).astype(BF16)
        rr = lax.broadcasted_iota(jnp.int32, (nrow, ncol), 0)
        cc = lax.broadcasted_iota(jnp.int32, (nrow, ncol), 1)
        tq_ = _idiv(_imod(rr, tn * N_HEADS), N_HEADS)
        same_head = _imod(rr, N_HEADS) == _imod(cc, N_HEADS)
        bias_ref[...] = jnp.where(same_head, slope * (_idiv(cc, N_HEADS) - tq_).astype(F32), NEG_INF)
        m_ref[...] = jnp.full(m_ref.shape, NEG_INF, F32)
        l_ref[...] = jnp.zeros(l_ref.shape, F32)
        acc_ref[...] = jnp.zeros(acc_ref.shape, F32)

    def lane_fold(x, op):
        out = x[:, :HEAD_DIM]
        for i in range(1, x.shape[1] // HEAD_DIM):
            out = op(out, x[:, i * HEAD_DIM:(i + 1) * HEAD_DIM])
        return out

    def update(scores, offs, vals):
        m_old = m_ref[...]
        m_loc = lane_fold(scores[0], jnp.maximum) - offs[0]
        for s, off in zip(scores[1:], offs[1:]):
            m_loc = jnp.maximum(m_loc, lane_fold(s, jnp.maximum) - off)
        m_new = jnp.maximum(m_old, jnp.max(m_loc, axis=-1, keepdims=True))
        alpha = jnp.exp2(m_old - m_new)
        pes = [jnp.exp2(s - (off + m_new)) for s, off in zip(scores, offs)]
        l_part = lane_fold(pes[0], jnp.add)
        for pe in pes[1:]:
            l_part = l_part + lane_fold(pe, jnp.add)
        pv = _dot(jnp.concatenate([pe.astype(BF16) for pe in pes], axis=1),
                  jnp.concatenate(vals, axis=0))
        l_ref[...] = alpha * l_ref[...] + l_part
        acc_ref[...] = alpha * acc_ref[...] + pv
        m_ref[...] = m_new

    qb = qb_ref[...]
    bias = bias_ref[...]
    scores, offs, vals = [], [], []
    for g in range(group):
        kf = kc_refs[g][0].reshape(ncol, HEAD_DIM).astype(BF16)
        vals.append(vc_refs[g][0].reshape(ncol, HEAD_DIM).astype(BF16))
        scores.append(_dot_nt(qb, kf) + bias)
        offs.append(slope * jnp.asarray(past - (p * group + g) * page, F32))
    update(scores, offs, vals)

    @pl.when(p == n_pages // group - 1)
    def _():
        ncn = tn * N_HEADS
        rr = lax.broadcasted_iota(jnp.int32, (nrow, HEAD_DIM), 0)
        cc = lax.broadcasted_iota(jnp.int32, (nrow, HEAD_DIM), 1)
        tq_ = _idiv(_imod(rr, ncn), N_HEADS)
        tk_ = _idiv(cc, N_HEADS)
        ok = (cc < ncn) & (_imod(rr, N_HEADS) == _imod(cc, N_HEADS)) & (tk_ <= tq_)
        bias_new = jnp.where(ok, slope * (tk_ - tq_).astype(F32), NEG_INF)
        pad = jnp.zeros((HEAD_DIM - ncn, HEAD_DIM), BF16)
        kn = jnp.concatenate([kn_ref[0].astype(BF16), pad], axis=0)
        vn = jnp.concatenate([vn_ref[0].astype(BF16), pad], axis=0)
        update([_dot_nt(qb, kn) + bias_new], [jnp.zeros((nrow, 1), F32)], [vn])
        lam = _lambda_full(lam_ref, lam_init)
        o = acc_ref[...] / jnp.sum(l_ref[...], axis=-1, keepdims=True)
        d = o[:ncn] - lam * o[ncn:]
        o_ref[0] = (_rms(d, subw_ref[...]) * (1.0 - lam_init)).astype(o_ref.dtype)


def _attn_sample(page_table, lam_rows, subln_w, q, k_new, v_new, cache_k, cache_v, *, lam_init, group):
    n_seq, n_pages = page_table.shape
    page = cache_k.shape[1]
    tn = q.shape[1] // N_HEADS
    nrow = 2 * tn * N_HEADS
    ncol = page * N_HEADS
    assert n_pages % group == 0

    def page_spec(g):
        return pl.BlockSpec((1, page, N_HEADS, HEAD_DIM),
                            lambda b, p, pt: (pt[b, p * group + g], 0, 0, 0))

    new_spec = pl.BlockSpec((1, tn * N_HEADS, HEAD_DIM), lambda b, p, pt: (b, 0, 0))
    grid_spec = pltpu.PrefetchScalarGridSpec(
        num_scalar_prefetch=1,
        grid=(n_seq, n_pages // group),
        in_specs=[
            pl.BlockSpec((4, QK_DIM), lambda b, p, pt: (0, 0)),
            pl.BlockSpec((1, HEAD_DIM), lambda b, p, pt: (0, 0)),
            new_spec, new_spec, new_spec,
        ] + [page_spec(g) for g in range(group)] * 2,
        out_specs=new_spec,
        scratch_shapes=[
            pltpu.VMEM((nrow, HEAD_DIM), BF16),
            pltpu.VMEM((nrow, ncol), F32),
            pltpu.VMEM((nrow, 1), F32),
            pltpu.VMEM((nrow, HEAD_DIM), F32),
            pltpu.VMEM((nrow, HEAD_DIM), F32),
        ],
    )
    return pl.pallas_call(
        functools.partial(_attn_sample_kernel, n_pages=n_pages, page=page, tn=tn, group=group,
                          lam_init=lam_init),
        grid_spec=grid_spec,
        out_shape=jax.ShapeDtypeStruct((n_seq, tn * N_HEADS, HEAD_DIM), BF16),
        compiler_params=pltpu.CompilerParams(
            dimension_semantics=("parallel", "arbitrary"), vmem_limit_bytes=V7X_VMEM_LIMIT_BYTES),
        name="attn_sample",
    )(page_table, lam_rows, subln_w, q, k_new, v_new, *([cache_k] * group), *([cache_v] * group))


def _gdn_n_masks(chunk):
    return 3 + (chunk.bit_length() - 1)


def _gdn_fill_masks(mask_ref, *, rows, chunk):
    ri = lax.broadcasted_iota(jnp.int32, (rows, rows), 0)
    ci = lax.broadcasted_iota(jnp.int32, (rows, rows), 1)
    same = _idiv(ri, chunk) == _idiv(ci, chunk)
    mask_ref[0] = (same & (ri >= ci)).astype(F32)
    mask_ref[1] = (same & (ri > ci)).astype(F32)
    mask_ref[2] = (ri == ci).astype(F32)
    s, k = 1, 3
    while s < chunk:
        mask_ref[k] = ((_idiv(ri, 2 * s) == _idiv(ci, 2 * s)) & (_imod(_idiv(ri, s), 2) == 1)
                       & (_imod(_idiv(ci, s), 2) == 0)).astype(F32)
        s, k = 2 * s, k + 1


def _gdn_intra_all(act, gates, alog_row, dtb_row, mask_ref, u_ref, w_ref, qg_ref, kdt_ref, qkc_ref,
                   egl_ref, *, rows, chunk):
    n_chunks = rows // chunk
    n_levels = _gdn_n_masks(chunk) - 3
    beta_all = jax.nn.sigmoid(gates)
    g_all = -jnp.exp(alog_row) * _softplus(gates + dtb_row)
    gcum_all = _dot_sel_l(mask_ref[0].astype(BF16), g_all)
    gtot_all = jnp.concatenate(
        [jnp.broadcast_to(gcum_all[(c + 1) * chunk - 1:(c + 1) * chunk], (chunk, HEAD_DIM))
         for c in range(n_chunks)], axis=0)
    gcum_t = gcum_all.T
    fr = lax.broadcasted_iota(jnp.int32, (rows, chunk), 0)
    fc = lax.broadcasted_iota(jnp.int32, (rows, chunk), 1)
    fold = (_imod(fr, chunk) == fc).astype(BF16)
    er = lax.broadcasted_iota(jnp.int32, (HEAD_DIM, HEAD_DIM), 0)
    eye_b = (er == lax.broadcasted_iota(jnp.int32, (HEAD_DIM, HEAD_DIM), 1)).astype(BF16)

    def lanes(col):
        return jnp.broadcast_to(col, (rows, HEAD_DIM))

    def square(x):
        if rows >= HEAD_DIM:
            return jnp.concatenate([x] * (rows // HEAD_DIM), axis=1)
        return x[:, :rows]

    heads = range(N_HEADS)
    egl_rows, lmats, lbs, tinvs, rhss = [], [], [], [], []
    for h in heads:
        cols = slice(h * HEAD_DIM, (h + 1) * HEAD_DIM)
        aq = act[:, h * HEAD_DIM:(h + 1) * HEAD_DIM]
        ak = act[:, (N_HEADS + h) * HEAD_DIM:(N_HEADS + h + 1) * HEAD_DIM]
        av = act[:, (2 * N_HEADS + h) * HEAD_DIM:(2 * N_HEADS + h + 1) * HEAD_DIM]
        qn = aq * lax.rsqrt(jnp.sum(aq * aq, axis=-1, keepdims=True) + 1e-6) * (HEAD_DIM ** -0.5)
        kn = ak * lax.rsqrt(jnp.sum(ak * ak, axis=-1, keepdims=True) + 1e-6)
        beta = lanes(beta_all[:, h:h + 1])
        g_cum = lanes(gcum_all[:, N_HEADS + h:N_HEADS + h + 1])
        g_tot = lanes(gtot_all[:, N_HEADS + h:N_HEADS + h + 1])
        g_row = gcum_t[N_HEADS + h:N_HEADS + h + 1, :]
        decay = jnp.exp(jnp.minimum(square(g_cum) - g_row, 0.0)) * mask_ref[0]

        kbeta = kn * beta
        knb = kn.astype(BF16)
        lmat = mask_ref[1] * (_dot_nt(kbeta.astype(BF16), knb) * decay)
        lmats.append(lmat)
        lbs.append(lmat.astype(BF16))
        tinvs.append(mask_ref[2] - lmat * mask_ref[3])
        e_cum = jnp.exp(g_cum)
        rhss.append(jnp.concatenate([av * beta, kbeta * e_cum], axis=1).astype(BF16))
        qk = (_dot_nt(qn.astype(BF16), knb) * decay).astype(BF16)
        qkc_ref[h] = _dot(qk, fold).astype(BF16)
        qg_ref[:, cols] = (qn * e_cum).astype(BF16)
        kd = (kn * jnp.exp(g_tot - g_cum)).astype(BF16)
        kdt_ref[h] = _dot_nt(eye_b, kd).astype(BF16)
        egl = jnp.exp(g_tot)
        egl_rows.append(jnp.concatenate(
            [egl[c * chunk:c * chunk + 1] for c in range(n_chunks)]
            + [jnp.zeros((8 - n_chunks, HEAD_DIM), F32)] * (n_chunks < 8), axis=0))
    egl_ref[0] = jnp.concatenate(egl_rows, axis=1)

    for k in range(1, n_levels):
        tbs = [tinvs[h].astype(BF16) for h in heads]
        mids = [_dot(tbs[h], lbs[h]).astype(BF16) for h in heads]
        tinvs = [tinvs[h] - mask_ref[3 + k] * _dot(mids[h], tbs[h]) for h in heads]
    resids = [mask_ref[2] - tinvs[h] - _dot_hp(lmats[h], tinvs[h]) for h in heads]
    tinvs = [tinvs[h] + _dot(tinvs[h].astype(BF16), resids[h].astype(BF16)) for h in heads]
    for h in heads:
        cols = slice(h * HEAD_DIM, (h + 1) * HEAD_DIM)
        uw = _dot(tinvs[h].astype(BF16), rhss[h])
        u_ref[:, cols] = uw[:, :HEAD_DIM]
        w_ref[:, cols] = uw[:, HEAD_DIM:].astype(BF16)


def _gdn_intra_prompt_kernel(x_ref, gates_ref, cw_ref, alog_ref, dtb_ref,
                             u_ref, w_ref, qg_ref, kdt_ref, qkc_ref, egl_ref, pad_ref, mask_ref,
                             *, rows, chunk):
    t = pl.program_id(1)

    @pl.when(t == 0)
    def _():
        pad_ref[0:8, :] = jnp.zeros((8, pad_ref.shape[1]), F32)
        _gdn_fill_masks(mask_ref, rows=rows, chunk=chunk)

    pad_ref[8:, :] = x_ref[...]
    cw = cw_ref[...]
    acc = pad_ref[pl.ds(8 - (CONV_W - 1), rows), :] * cw[0:1]
    for j in range(1, CONV_W):
        acc = acc + pad_ref[pl.ds(8 - (CONV_W - 1) + j, rows), :] * cw[j:j + 1]
    pad_ref[0:8, :] = x_ref[rows - 8:, :]
    _gdn_intra_all(_silu(acc), gates_ref[...], alog_ref[...], dtb_ref[...], mask_ref,
                   u_ref, w_ref, qg_ref, kdt_ref, qkc_ref, egl_ref, rows=rows, chunk=chunk)


def _gdn_intra_sample_kernel(x_ref, gates_ref, cw_ref, alog_ref, dtb_ref, buf_ref,
                             u_ref, w_ref, qg_ref, kdt_ref, qkc_ref, egl_ref, pad_ref, mask_ref,
                             *, n_seq, tn):
    seg = 8 + tn
    _gdn_fill_masks(mask_ref, rows=n_seq * tn, chunk=tn)
    cw = cw_ref[...]
    for s in range(n_seq):
        pad_ref[s * seg + 8 - (CONV_W - 1):s * seg + 8, :] = buf_ref[s]
        pad_ref[s * seg + 8:(s + 1) * seg, :] = x_ref[s * tn:(s + 1) * tn, :]
    pieces = []
    for s in range(n_seq):
        base = s * seg + 8 - (CONV_W - 1)
        acc = pad_ref[pl.ds(base, tn), :] * cw[0:1]
        for j in range(1, CONV_W):
            acc = acc + pad_ref[pl.ds(base + j, tn), :] * cw[j:j + 1]
        pieces.append(acc)
    _gdn_intra_all(_silu(jnp.concatenate(pieces, axis=0)), gates_ref[...], alog_ref[...], dtb_ref[...],
                   mask_ref, u_ref, w_ref, qg_ref, kdt_ref, qkc_ref, egl_ref, rows=n_seq * tn, chunk=tn)


def _intra_out(m, chunk, n_steps):
    width = N_HEADS * HEAD_DIM
    return [
        jax.ShapeDtypeStruct((m, width), F32),
        jax.ShapeDtypeStruct((m, width), BF16),
        jax.ShapeDtypeStruct((m, width), BF16),
        jax.ShapeDtypeStruct((N_HEADS, HEAD_DIM, m), BF16),
        jax.ShapeDtypeStruct((N_HEADS, m, chunk), BF16),
        jax.ShapeDtypeStruct((n_steps, 8, width), F32),
    ]


def _gdn_intra_prompt(proj, gates, conv_w, alog_row, dtb_row, *, batch, t):
    m = proj.shape[0]
    rows = GDN_ROWS
    nblk = t // rows
    width = N_HEADS * HEAD_DIM
    cwid = 3 * width

    def rb(b, i):
        return b * nblk + i

    in_specs = [
        pl.BlockSpec((rows, cwid), lambda b, i: (rb(b, i), 1)),
        pl.BlockSpec((rows, HEAD_DIM), lambda b, i: (rb(b, i), 0)),
        pl.BlockSpec((CONV_W, cwid), lambda b, i: (0, 0)),
        pl.BlockSpec((1, HEAD_DIM), lambda b, i: (0, 0)),
        pl.BlockSpec((1, HEAD_DIM), lambda b, i: (0, 0)),
    ]
    out_specs = [
        pl.BlockSpec((rows, width), lambda b, i: (rb(b, i), 0)),
        pl.BlockSpec((rows, width), lambda b, i: (rb(b, i), 0)),
        pl.BlockSpec((rows, width), lambda b, i: (rb(b, i), 0)),
        pl.BlockSpec((N_HEADS, HEAD_DIM, rows), lambda b, i: (0, 0, rb(b, i))),
        pl.BlockSpec((N_HEADS, rows, CHUNK), lambda b, i: (0, rb(b, i), 0)),
        pl.BlockSpec((1, 8, width), lambda b, i: (rb(b, i), 0, 0)),
    ]
    return pl.pallas_call(
        functools.partial(_gdn_intra_prompt_kernel, rows=rows, chunk=CHUNK),
        grid=(batch, nblk),
        in_specs=in_specs,
        out_specs=out_specs,
        out_shape=_intra_out(m, CHUNK, batch * nblk),
        scratch_shapes=[pltpu.VMEM((rows + 8, cwid), F32),
                        pltpu.VMEM((_gdn_n_masks(CHUNK), rows, rows), F32)],
        compiler_params=pltpu.CompilerParams(
            dimension_semantics=("parallel", "arbitrary"), vmem_limit_bytes=V7X_VMEM_LIMIT_BYTES),
        name="gdn_intra_prompt",
    )(proj, gates, conv_w, alog_row, dtb_row)


def _gdn_intra_sample(proj, gates, conv_w, alog_row, dtb_row, conv_state, *, n_seq, tn):
    m = proj.shape[0]
    width = N_HEADS * HEAD_DIM
    cwid = 3 * width
    in_specs = [
        pl.BlockSpec((m, cwid), lambda i: (0, 1)),
        pl.BlockSpec((m, HEAD_DIM), lambda i: (0, 0)),
        pl.BlockSpec((CONV_W, cwid), lambda i: (0, 0)),
        pl.BlockSpec((1, HEAD_DIM), lambda i: (0, 0)),
        pl.BlockSpec((1, HEAD_DIM), lambda i: (0, 0)),
        pl.BlockSpec((n_seq, CONV_W - 1, cwid), lambda i: (0, 0, 0)),
    ]
    out_specs = [
        pl.BlockSpec((m, width), lambda i: (0, 0)),
        pl.BlockSpec((m, width), lambda i: (0, 0)),
        pl.BlockSpec((m, width), lambda i: (0, 0)),
        pl.BlockSpec((N_HEADS, HEAD_DIM, m), lambda i: (0, 0, 0)),
        pl.BlockSpec((N_HEADS, m, tn), lambda i: (0, 0, 0)),
        pl.BlockSpec((1, 8, width), lambda i: (0, 0, 0)),
    ]
    return pl.pallas_call(
        functools.partial(_gdn_intra_sample_kernel, n_seq=n_seq, tn=tn),
        grid=(1,),
        in_specs=in_specs,
        out_specs=out_specs,
        out_shape=_intra_out(m, tn, 1),
        scratch_shapes=[pltpu.VMEM((n_seq * (8 + tn), cwid), F32),
                        pltpu.VMEM((_gdn_n_masks(tn), m, m), F32)],
        compiler_params=pltpu.CompilerParams(
            dimension_semantics=("arbitrary",), vmem_limit_bytes=V7X_VMEM_LIMIT_BYTES),
        name="gdn_intra_sample",
    )(proj, gates, conv_w, alog_row, dtb_row, conv_state)


def _gdn_scan_kernel(u_ref, w_ref, qg_ref, kdt_ref, qkc_ref, egl_ref, z_ref, s0_ref, dnw_ref,
                     o_ref, s_ref, *, chunk, n_chunks, seqs_per_step):
    t = pl.program_id(1)

    @pl.when(t == 0)
    def _():
        s_ref[...] = s0_ref[...]

    dnw = dnw_ref[...]
    for c in range(n_chunks):
        si = c if seqs_per_step > 1 else 0
        rows = slice(c * chunk, (c + 1) * chunk)
        heads = range(N_HEADS)
        hcols = [slice(h * HEAD_DIM, (h + 1) * HEAD_DIM) for h in heads]
        s_old = [s_ref[si, h] for h in heads]
        both = [_dot(jnp.concatenate([w_ref[rows, hcols[h]], qg_ref[rows, hcols[h]]], axis=0),
                     s_old[h].astype(BF16)) for h in heads]
        vn = [(u_ref[rows, hcols[h]] - both[h][:chunk]).astype(BF16) for h in heads]
        o = [both[h][chunk:] + _dot(qkc_ref[h, rows, :], vn[h]) for h in heads]
        for h in heads:
            s_ref[si, h] = s_old[h] * egl_ref[0, c:c + 1, hcols[h]] + _dot(kdt_ref[h, :, rows], vn[h])
        for h in heads:
            o_ref[rows, hcols[h]] = (_rms(o[h], dnw) * _silu(z_ref[rows, hcols[h]])).astype(o_ref.dtype)


def _gdn_scan(intra, z_src, z_col_block, s0, dnw, *, n_seq, t, chunk, chunks_per_step, seqs_per_step=1):
    u, w, qg, kdt, qkc, egl = intra
    m = u.shape[0]
    width = N_HEADS * HEAD_DIM
    rows = chunk * chunks_per_step
    if seqs_per_step > 1:
        assert seqs_per_step == n_seq == chunks_per_step and t == chunk
        nsteps, n_outer = 1, 1
    else:
        nsteps, n_outer = t // rows, n_seq

    def rb(b, i):
        return b * nsteps + i

    in_specs = [
        pl.BlockSpec((rows, width), lambda b, i: (rb(b, i), 0)),
        pl.BlockSpec((rows, width), lambda b, i: (rb(b, i), 0)),
        pl.BlockSpec((rows, width), lambda b, i: (rb(b, i), 0)),
        pl.BlockSpec((N_HEADS, HEAD_DIM, rows), lambda b, i: (0, 0, rb(b, i))),
        pl.BlockSpec((N_HEADS, rows, chunk), lambda b, i: (0, rb(b, i), 0)),
        pl.BlockSpec((1, 8, width), lambda b, i: (rb(b, i), 0, 0)),
        pl.BlockSpec((rows, width), lambda b, i: (rb(b, i), z_col_block)),
        pl.BlockSpec((seqs_per_step, N_HEADS, HEAD_DIM, HEAD_DIM), lambda b, i: (b, 0, 0, 0)),
        pl.BlockSpec((1, HEAD_DIM), lambda b, i: (0, 0)),
    ]
    out_specs = [
        pl.BlockSpec((rows, width), lambda b, i: (rb(b, i), 0)),
        pl.BlockSpec((seqs_per_step, N_HEADS, HEAD_DIM, HEAD_DIM), lambda b, i: (b, 0, 0, 0)),
    ]
    return pl.pallas_call(
        functools.partial(_gdn_scan_kernel, chunk=chunk, n_chunks=chunks_per_step,
                          seqs_per_step=seqs_per_step),
        grid=(n_outer, nsteps),
        in_specs=in_specs,
        out_specs=out_specs,
        out_shape=[jax.ShapeDtypeStruct((m, width), BF16),
                   jax.ShapeDtypeStruct((n_seq, N_HEADS, HEAD_DIM, HEAD_DIM), F32)],
        compiler_params=pltpu.CompilerParams(
            dimension_semantics=("parallel", "arbitrary"), vmem_limit_bytes=V7X_VMEM_LIMIT_BYTES),
        name="gdn_scan",
    )(u, w, qg, kdt, qkc, egl, z_src, s0, dnw)


def _pad_lanes(v, offset):
    out = jnp.zeros((1, HEAD_DIM), F32)
    return lax.dynamic_update_slice(out, v.reshape(1, -1).astype(F32), (0, offset))


def kernel(x_prompt, x_sample, cache_k, cache_v, state_ssm, state_conv, page_table, ffn1_pre_w, ffn1_post_w, ffn1_gate, ffn1_up, ffn1_down, mix_pre_w, mix_post_w, w_in, conv_w, a_log, dt_bias, delta_norm_w, lambda_q1, lambda_k1, lambda_q2, lambda_k2, subln_w, w_out, ffn2_pre_w, ffn2_post_w, ffn2_gate, ffn2_up, ffn2_down):
    depth = w_in.shape[0]
    batch, t, d = x_prompt.shape
    n_seq, tn, _ = x_sample.shape
    a_w = N_HEADS * HEAD_DIM
    n_main = 3 * a_w + 3 * a_w + a_w
    yp = x_prompt.reshape(batch * t, d)
    ys = x_sample.reshape(n_seq * tn, d)
    outs = [[] for _ in range(8)]
    w_in_t = jnp.swapaxes(w_in, 1, 2)
    for l in range(depth):
        lam_init = 0.8 - 0.6 * math.exp(-0.3 * l)
        lam_rows = jnp.stack([lambda_q1[l], lambda_k1[l], lambda_q2[l], lambda_k2[l]]).astype(F32)
        row = lambda v: v[l].reshape(1, -1).astype(F32)
        w_gates = jnp.pad(w_in_t[l, n_main:], ((0, HEAD_DIM - 2 * N_HEADS), (0, 0))).astype(BF16)
        alog_row = _pad_lanes(a_log[l], N_HEADS)
        dtb_row = _pad_lanes(dt_bias[l], N_HEADS)
        dnw = row(delta_norm_w)
        subw = row(subln_w)
        cw = conv_w[l].astype(F32)
        ms = n_seq * tn

        ys, wg1, wu1, wd1 = _ffn(ys, row(ffn1_pre_w), row(ffn1_post_w), ffn1_gate[l], ffn1_up[l],
                                 ffn1_down[l], tm=ms, tf=512, emit_bf16=True)
        yp, hp = _ffn(yp, row(ffn1_pre_w), row(ffn1_post_w), wg1, wu1, wd1, tm=512, tf=512,
                      next_pre_w=row(mix_pre_w))

        ps, gs, ks, vs, w_main = _inproj(ys, row(mix_pre_w), w_in_t, w_gates, tm=ms, tn=a_w, n=n_main,
                                         layer=l, emit_bf16=True)
        qs = ps[:, :a_w].reshape(n_seq, tn * N_HEADS, HEAD_DIM)
        ks = ks.reshape(n_seq, tn * N_HEADS, HEAD_DIM)
        vs = vs.reshape(n_seq, tn * N_HEADS, HEAD_DIM)
        oa_s = _attn_sample(page_table, lam_rows, subw, qs, ks, vs, cache_k[l], cache_v[l],
                            lam_init=lam_init, group=16)
        oa_s = oa_s.reshape(ms, a_w)
        intra_s = _gdn_intra_sample(ps, gs, cw, alog_row, dtb_row, state_conv[l], n_seq=n_seq, tn=tn)
        ob_s, s_s = _gdn_scan(intra_s, ps, 6, state_ssm[l].astype(F32), dnw, n_seq=n_seq, t=tn,
                              chunk=tn, chunks_per_step=n_seq, seqs_per_step=n_seq)
        ys, wo = _outproj(ys, oa_s, ob_s, w_out[l], row(mix_post_w), tm=ms, emit_bf16=True)

        pp, gp, kp, vp = _inproj(hp, row(mix_pre_w), w_main, w_gates, tm=1024, tn=a_w, n=n_main)
        oa_p = _attn_prompt(lam_rows, subw.reshape(HEAD_DIM, 1), pp, kp, vp, batch=batch, t=t,
                            lam_init=lam_init, tq=256, hpp=2)
        intra_p = _gdn_intra_prompt(pp, gp, cw, alog_row, dtb_row, batch=batch, t=t)
        zero_s = jnp.zeros((batch, N_HEADS, HEAD_DIM, HEAD_DIM), F32)
        ob_p, s_p = _gdn_scan(intra_p, pp, 6, zero_s, dnw, n_seq=batch, t=t, chunk=CHUNK,
                              chunks_per_step=GDN_ROWS // CHUNK)
        yp, hp2 = _outproj(yp, oa_p, ob_p, wo, row(mix_post_w), tm=512, next_pre_w=row(ffn2_pre_w))

        ys, wg2, wu2, wd2 = _ffn(ys, row(ffn2_pre_w), row(ffn2_post_w), ffn2_gate[l], ffn2_up[l],
                                 ffn2_down[l], tm=ms, tf=512, emit_bf16=True)
        yp = _ffn(yp, row(ffn2_pre_w), row(ffn2_post_w), wg2, wu2, wd2, tm=512, tf=512, h_in=hp2)

        ppb = pp.reshape(batch, t, -1)
        psb = ps.reshape(n_seq, tn, -1)
        outs[0].append(kp.reshape(batch, t, N_HEADS, HEAD_DIM))
        outs[1].append(vp.reshape(batch, t, N_HEADS, HEAD_DIM))
        outs[2].append(s_p.astype(state_ssm.dtype))
        outs[3].append(ppb[:, t - (CONV_W - 1):, 3 * a_w:6 * a_w])
        outs[4].append(ks.reshape(n_seq, tn, N_HEADS, HEAD_DIM))
        outs[5].append(vs.reshape(n_seq, tn, N_HEADS, HEAD_DIM))
        outs[6].append(s_s.astype(state_ssm.dtype))
        outs[7].append(jnp.concatenate([state_conv[l].astype(psb.dtype), psb[:, :, 3 * a_w:6 * a_w]],
                                       axis=1)[:, tn:])
    return (yp.reshape(batch, t, d), ys.reshape(n_seq, tn, d)) + tuple(jnp.stack(o) for o in outs)
```

```python
import functools
import math

import jax
import jax.numpy as jnp
from jax import lax
from jax.experimental import pallas as pl
from jax.experimental.pallas import tpu as pltpu

F32 = jnp.float32
BF16 = jnp.bfloat16
EPS = 1e-6
NEG_INF = float("-inf")
LOG2E = 1.4426950408889634

V7X_VMEM_LIMIT_BYTES = 60 * 1024 * 1024

QK_DIM = 64
HEAD_DIM = 128
N_HEADS = 8
CONV_W = 4
CHUNK = 64
GDN_ROWS = 128
GDN_SCAN_ROWS = 256

def _dot(a, b):
    return jnp.dot(a, b, preferred_element_type=F32)


def _dot_nt(a, b):
    return lax.dot_general(a, b, (((1,), (1,)), ((), ())), preferred_element_type=F32)


def _split3(x):
    hi = x.astype(BF16)
    r = x - hi.astype(F32)
    mid = r.astype(BF16)
    lo = (r - mid.astype(F32)).astype(BF16)
    return hi, mid, lo


def _dot_sel_l(sel, x):
    hi, mid, lo = _split3(x)
    return _dot(sel, hi) + _dot(sel, mid) + _dot(sel, lo)


def _dot_sel_r(x, sel):
    hi, mid, lo = _split3(x)
    return _dot(hi, sel) + _dot(mid, sel) + _dot(lo, sel)


def _dot_hp(a, b):
    ah = a.astype(BF16)
    al = (a - ah.astype(F32)).astype(BF16)
    bh = b.astype(BF16)
    bl = (b - bh.astype(F32)).astype(BF16)
    return _dot(ah, bh) + _dot(ah, bl) + _dot(al, bh)


def _rms(x, w):
    return x * lax.rsqrt(jnp.mean(x * x, axis=-1, keepdims=True) + EPS) * w


def _silu(x):
    return x * jax.nn.sigmoid(x)


def _softplus(x):
    return jnp.maximum(x, 0.0) + jnp.log1p(jnp.exp(-jnp.abs(x)))


def _lambda_full(lam_ref, lam_init):
    l = lam_ref[...]
    a = jnp.sum(l[0:1] * l[1:2], axis=-1, keepdims=True)
    b = jnp.sum(l[2:3] * l[3:4], axis=-1, keepdims=True)
    return jnp.exp(a) - jnp.exp(b) + lam_init


def _idiv(x, n):
    assert n & (n - 1) == 0
    return x >> (n.bit_length() - 1)


def _imod(x, n):
    assert n & (n - 1) == 0
    return x & (n - 1)


def _pow2_neg(k):
    return lax.bitcast_convert_type((127 - k) << 23, F32)


def _ffn_kernel(*refs, nf, emit_bf16, emit_next):
    x_ref, prew_ref, postw_ref, wg_ref, wu_ref, wd_ref = refs[:6]
    rest = list(refs[6:])
    nextw_ref = rest.pop(0) if emit_next else None
    o_ref = rest.pop(0)
    cast_refs = [rest.pop(0) for _ in range(3)] if emit_bf16 else None
    hn_ref = rest.pop(0) if emit_next else None
    h_ref, = rest
    f = pl.program_id(1)

    @pl.when(f == 0)
    def _():
        h_ref[...] = _rms(x_ref[...], prew_ref[...]).astype(BF16)
        o_ref[...] = jnp.zeros_like(o_ref)

    wg, wu, wd = wg_ref[...], wu_ref[...], wd_ref[...]
    if emit_bf16:
        wg, wu, wd = wg.astype(BF16), wu.astype(BF16), wd.astype(BF16)
        cast_refs[0][...], cast_refs[1][...], cast_refs[2][...] = wg, wu, wd
    h = h_ref[...]
    g = _dot(h, wg)
    u = _dot(h, wu)
    o_ref[...] += _dot((_silu(g) * u).astype(BF16), wd)

    @pl.when(f == nf - 1)
    def _():
        y = x_ref[...] + 0.5 * _rms(o_ref[...], postw_ref[...])
        o_ref[...] = y
        if emit_next:
            hn_ref[...] = _rms(y, nextw_ref[...]).astype(BF16)


def _ffn(x, pre_w, post_w, wg, wu, wd, *, tm, tf, emit_bf16=False, next_pre_w=None):
    m, d = x.shape
    ff = wg.shape[1]
    nf = ff // tf
    emit_next = next_pre_w is not None
    assert not emit_bf16 or m == tm
    gu_spec = pl.BlockSpec((d, tf), lambda i, f: (0, f))
    dn_spec = pl.BlockSpec((tf, d), lambda i, f: (f, 0))
    row_spec = pl.BlockSpec((tm, d), lambda i, f: (i, 0))
    vec_spec = pl.BlockSpec((1, d), lambda i, f: (0, 0))
    out_specs = [row_spec]
    out_shape = [jax.ShapeDtypeStruct((m, d), F32)]
    if emit_bf16:
        out_specs += [gu_spec, gu_spec, dn_spec]
        out_shape += [jax.ShapeDtypeStruct(w.shape, BF16) for w in (wg, wu, wd)]
    if emit_next:
        out_specs.append(row_spec)
        out_shape.append(jax.ShapeDtypeStruct((m, d), BF16))
    outs = pl.pallas_call(
        functools.partial(_ffn_kernel, nf=nf, emit_bf16=emit_bf16, emit_next=emit_next),
        grid=(m // tm, nf),
        in_specs=[row_spec, vec_spec, vec_spec, gu_spec, gu_spec, dn_spec] + [vec_spec] * emit_next,
        out_specs=out_specs,
        out_shape=out_shape,
        scratch_shapes=[pltpu.VMEM((tm, d), BF16)],
        compiler_params=pltpu.CompilerParams(
            dimension_semantics=("parallel", "arbitrary"), vmem_limit_bytes=V7X_VMEM_LIMIT_BYTES),
        name="ffn",
    )(x, pre_w, post_w, wg, wu, wd, *([next_pre_w] * emit_next))
    return outs if len(outs) > 1 else outs[0]


def _inproj_kernel(x_ref, prew_ref, wt_ref, wgt_ref, o_ref, og_ref, ko_ref, vo_ref, *rest, emit_bf16):
    normed = x_ref.dtype == BF16
    h_ref = x_ref if normed else rest[-1]
    j = pl.program_id(1)

    @pl.when(j == 0)
    def _():
        if not normed:
            h_ref[...] = _rms(x_ref[...], prew_ref[...]).astype(BF16)
        og_ref[...] = _dot_nt(h_ref[...], wgt_ref[...])

    wt = wt_ref[...]
    if emit_bf16:
        wt = wt.astype(BF16)
        rest[0][...] = wt
    y = _dot_nt(h_ref[...], wt)
    o_ref[...] = y

    @pl.when(j == 1)
    def _():
        ko_ref[...] = y

    @pl.when(j == 2)
    def _():
        vo_ref[...] = y


def _inproj(x, pre_w, w_main_t, w_gates_t, *, tm, tn, n, layer=None, emit_bf16=False):
    m, d = x.shape
    ng = w_gates_t.shape[0]
    assert not emit_bf16 or m == tm
    w_spec = pl.BlockSpec((tn, d), lambda i, j: (j, 0))
    w_in_spec = w_spec if layer is None else pl.BlockSpec((None, tn, d), lambda i, j: (layer, j, 0))
    row_spec = pl.BlockSpec((tm, tn), lambda i, j: (i, 0))
    out_specs = [pl.BlockSpec((tm, tn), lambda i, j: (i, j)), pl.BlockSpec((tm, ng), lambda i, j: (i, 0)),
                 row_spec, row_spec]
    out_shape = [jax.ShapeDtypeStruct((m, n), F32), jax.ShapeDtypeStruct((m, ng), F32),
                 jax.ShapeDtypeStruct((m, tn), F32), jax.ShapeDtypeStruct((m, tn), F32)]
    if emit_bf16:
        out_specs.append(w_spec)
        out_shape.append(jax.ShapeDtypeStruct((n, d), BF16))
    return pl.pallas_call(
        functools.partial(_inproj_kernel, emit_bf16=emit_bf16),
        grid=(m // tm, n // tn),
        in_specs=[
            pl.BlockSpec((tm, d), lambda i, j: (i, 0),
                         pipeline_mode=pl.Buffered(1 if tm * d * x.dtype.itemsize >= 8 * 1024 * 1024 else 2)),
            pl.BlockSpec((1, d), lambda i, j: (0, 0)),
            w_in_spec,
            pl.BlockSpec((ng, d), lambda i, j: (0, 0)),
        ],
        out_specs=out_specs,
        out_shape=out_shape,
        scratch_shapes=[pltpu.VMEM((tm, d), BF16)],
        compiler_params=pltpu.CompilerParams(
            dimension_semantics=("parallel", "arbitrary"), vmem_limit_bytes=V7X_VMEM_LIMIT_BYTES),
        name="inproj",
    )(x, pre_w, w_main_t, w_gates_t)


def _outproj_kernel(x_ref, oa_ref, ob_ref, wa_ref, wb_ref, postw_ref, o_ref, *rest, emit_bf16):
    wa, wb = wa_ref[...], wb_ref[...]
    if emit_bf16:
        wa, wb = wa.astype(BF16), wb.astype(BF16)
        ka = wa.shape[0]
        rest[0][:ka, :] = wa
        rest[0][ka:, :] = wb
    y = _dot(oa_ref[...], wa) + _dot(ob_ref[...], wb)
    o_ref[...] = x_ref[...] + _rms(y, postw_ref[...])


def _outproj(x, oa, ob, w_out, post_w, *, tm, emit_bf16=False):
    m, d = x.shape
    ka = oa.shape[1]
    assert not emit_bf16 or m == tm
    out_specs = [pl.BlockSpec((tm, d), lambda i: (i, 0))]
    out_shape = [jax.ShapeDtypeStruct((m, d), F32)]
    if emit_bf16:
        out_specs.append(pl.BlockSpec(w_out.shape, lambda i: (0, 0)))
        out_shape.append(jax.ShapeDtypeStruct(w_out.shape, BF16))
    outs = pl.pallas_call(
        functools.partial(_outproj_kernel, emit_bf16=emit_bf16),
        grid=(m // tm,),
        in_specs=[
            pl.BlockSpec((tm, d), lambda i: (i, 0)),
            pl.BlockSpec((tm, ka), lambda i: (i, 0)),
            pl.BlockSpec((tm, ka), lambda i: (i, 0)),
            pl.BlockSpec((ka, d), lambda i: (0, 0)),
            pl.BlockSpec((ka, d), lambda i: (1, 0)),
            pl.BlockSpec((1, d), lambda i: (0, 0)),
        ],
        out_specs=out_specs,
        out_shape=out_shape,
        compiler_params=pltpu.CompilerParams(
            dimension_semantics=("parallel",), vmem_limit_bytes=V7X_VMEM_LIMIT_BYTES),
        name="outproj",
    )(x, oa, ob, w_out, w_out, post_w)
    return outs if emit_bf16 else outs[0]


def _attn_prompt_kernel(lam_ref, subw_ref, q_ref, k_ref, v_ref, o_ref,
                        kb_ref, vt_ref, bias_ref, *, t, tq, hpp, lam_init):
    hg = pl.program_id(1)
    nq = t // tq
    lam = _lambda_full(lam_ref, lam_init)
    lane = lax.broadcasted_iota(jnp.int32, (1, HEAD_DIM), 1)
    first_map = lane < QK_DIM
    kk = lax.broadcasted_iota(jnp.int32, (tq, tq), 0)
    qq = lax.broadcasted_iota(jnp.int32, (tq, tq), 1)
    slopes = [_pow2_neg(jnp.full((1, 1), hg * hpp + hh + 1, jnp.int32)) * LOG2E for hh in range(hpp)]
    er = lax.broadcasted_iota(jnp.int32, (HEAD_DIM, HEAD_DIM), 0)
    eye_b = (er == lax.broadcasted_iota(jnp.int32, (HEAD_DIM, HEAD_DIM), 1)).astype(BF16)
    for hh in range(hpp):
        cols = slice(hh * HEAD_DIM, (hh + 1) * HEAD_DIM)
        b = slopes[hh] * (kk - qq).astype(F32)
        bias_ref[hh, 0] = b
        bias_ref[hh, 1] = jnp.where(qq >= kk, b, NEG_INF)
        vt_ref[hh] = _dot_nt(eye_b, v_ref[:, cols].astype(BF16)).astype(BF16)
    kb_ref[...] = k_ref[...].astype(BF16)
    scale = QK_DIM ** -0.5 * LOG2E
    subw_col = subw_ref[...]

    def q_block(i, _):
        rows_i = pl.ds(pl.multiple_of(i * tq, tq), tq)
        qs = []
        for hh in range(hpp):
            q = q_ref[rows_i, hh * HEAD_DIM:(hh + 1) * HEAD_DIM] * scale
            qs.append(jnp.concatenate(
                [jnp.where(first_map, q, 0.0), jnp.where(first_map, 0.0, q)], axis=0).astype(BF16))

        def scores(j):
            rows_j = pl.ds(pl.multiple_of(j * tq, tq), tq)
            diag = jnp.asarray(j == i, jnp.int32)
            out = []
            for hh in range(hpp):
                b = bias_ref[hh, diag]
                out.append(_dot_nt(kb_ref[rows_j, hh * HEAD_DIM:(hh + 1) * HEAD_DIM], qs[hh])
                           + jnp.concatenate([b, b], axis=1))
            return tuple(out)

        def consume(j, sts, state):
            rows_j = pl.ds(pl.multiple_of(j * tq, tq), tq)
            blocks_away = jnp.asarray((i - j) * tq, F32)
            hs = range(hpp)
            far = [slopes[hh] * blocks_away for hh in hs]
            m_new = [jnp.maximum(state[hh][0], jnp.max(sts[hh], axis=0, keepdims=True) - far[hh])
                     for hh in hs]
            p = [jnp.exp2(sts[hh] - (far[hh] + m_new[hh])) for hh in hs]
            alpha = [jnp.exp2(state[hh][0] - m_new[hh]) for hh in hs]
            l_new = [alpha[hh] * state[hh][1] + jnp.sum(p[hh], axis=0, keepdims=True) for hh in hs]
            pv = [_dot(vt_ref[hh, :, rows_j], p[hh].astype(BF16)) for hh in hs]
            return tuple((m_new[hh], l_new[hh], alpha[hh] * state[hh][2] + pv[hh]) for hh in hs)

        def step(j, carry):
            sts, state = carry
            nxt = scores(j + 1)
            return nxt, consume(j, sts, state)

        init = tuple((jnp.full((1, 2 * tq), NEG_INF, F32), jnp.zeros((1, 2 * tq), F32),
                      jnp.zeros((HEAD_DIM, 2 * tq), F32)) for _ in range(hpp))
        sts, state = lax.fori_loop(0, i, step, (scores(0), init))
        state = consume(i, sts, state)
        for hh in range(hpp):
            _, l_fin, acc_fin = state[hh]
            ot = acc_fin / l_fin
            dt = ot[:, :tq] - lam * ot[:, tq:]
            dt = dt * lax.rsqrt(jnp.mean(dt * dt, axis=0, keepdims=True) + EPS) * subw_col
            o_ref[rows_i, hh * HEAD_DIM:(hh + 1) * HEAD_DIM] = (
                (dt * (1.0 - lam_init)).T.astype(o_ref.dtype))
        return 0

    lax.fori_loop(0, nq, q_block, 0)


def _attn_prompt(lam_rows, subln_col, proj, k, v, *, batch, t, lam_init, tq, hpp):
    m = proj.shape[0]
    ng = N_HEADS // hpp
    w = hpp * HEAD_DIM
    return pl.pallas_call(
        functools.partial(_attn_prompt_kernel, t=t, tq=tq, hpp=hpp, lam_init=lam_init),
        grid=(batch, ng),
        in_specs=[
            pl.BlockSpec((4, QK_DIM), lambda b, g: (0, 0)),
            pl.BlockSpec((HEAD_DIM, 1), lambda b, g: (0, 0)),
            pl.BlockSpec((t, w), lambda b, g: (b, g)),
            pl.BlockSpec((t, w), lambda b, g: (b, g)),
            pl.BlockSpec((t, w), lambda b, g: (b, g)),
        ],
        out_specs=pl.BlockSpec((t, w), lambda b, g: (b, g)),
        out_shape=jax.ShapeDtypeStruct((m, N_HEADS * HEAD_DIM), BF16),
        scratch_shapes=[
            pltpu.VMEM((t, w), BF16),
            pltpu.VMEM((hpp, HEAD_DIM, t), BF16),
            pltpu.VMEM((hpp, 2, tq, tq), F32),
        ],
        compiler_params=pltpu.CompilerParams(
            dimension_semantics=("parallel", "parallel"), vmem_limit_bytes=V7X_VMEM_LIMIT_BYTES),
        name="attn_prompt",
    )(lam_rows, subln_col, proj, k, v)


def _attn_sample_kernel(pt_ref, lam_ref, subw_ref, q_ref, kn_ref, vn_ref, *rest,
                        n_pages, page, tn, group, lam_init):
    del pt_ref
    kc_refs = rest[:group]
    vc_refs = rest[group:2 * group]
    o_ref, qb_ref, bias_ref, m_ref, l_ref, acc_ref = rest[2 * group:]
    p = pl.program_id(1)
    nrow = 2 * tn * N_HEADS
    ncol = page * N_HEADS
    past = n_pages * page
    row = lax.broadcasted_iota(jnp.int32, (nrow, 1), 0)
    slope = _pow2_neg(_imod(row, N_HEADS) + 1) * LOG2E

    @pl.when(p == 0)
    def _():
        q = q_ref[0] * (QK_DIM ** -0.5 * LOG2E)
        lane = lax.broadcasted_iota(jnp.int32, (1, HEAD_DIM), 1)
        first_map = lane < QK_DIM
        qb_ref[...] = jnp.concatenate(
            [jnp.where(first_map, q, 0.0), jnp.where(first_map, 0.0, q)], axis=0).astype(BF16)
        rr = lax.broadcasted_iota(jnp.int32, (nrow, ncol), 0)
        cc = lax.broadcasted_iota(jnp.int32, (nrow, ncol), 1)
        tq_ = _idiv(_imod(rr, tn * N_HEADS), N_HEADS)
        same_head = _imod(rr, N_HEADS) == _imod(cc, N_HEADS)
        bias_ref[...] = jnp.where(same_head, slope * (_idiv(cc, N_HEADS) - tq_).astype(F32), NEG_INF)
        m_ref[...] = jnp.full(m_ref.shape, NEG_INF, F32)
        l_ref[...] = jnp.zeros(l_ref.shape, F32)
        acc_ref[...] = jnp.zeros(acc_ref.shape, F32)

    def lane_fold(x, op):
        out = x[:, :HEAD_DIM]
        for i in range(1, x.shape[1] // HEAD_DIM):
            out = op(out, x[:, i * HEAD_DIM:(i + 1) * HEAD_DIM])
        return out

    def update(scores, offs, vals):
        m_old = m_ref[...]
        m_loc = lane_fold(scores[0], jnp.maximum) - offs[0]
        for s, off in zip(scores[1:], offs[1:]):
            m_loc = jnp.maximum(m_loc, lane_fold(s, jnp.maximum) - off)
        m_new = jnp.maximum(m_old, jnp.max(m_loc, axis=-1, keepdims=True))
        alpha = jnp.exp2(m_old - m_new)
        pes = [jnp.exp2(s - (off + m_new)) for s, off in zip(scores, offs)]
        l_part = lane_fold(pes[0], jnp.add)
        for pe in pes[1:]:
            l_part = l_part + lane_fold(pe, jnp.add)
        pv = _dot(jnp.concatenate([pe.astype(BF16) for pe in pes], axis=1),
                  jnp.concatenate(vals, axis=0))
        l_ref[...] = alpha * l_ref[...] + l_part
        acc_ref[...] = alpha * acc_ref[...] + pv
        m_ref[...] = m_new

    qb = qb_ref[...]
    bias = bias_ref[...]
    scores, offs, vals = [], [], []
    for g in range(group):
        kf = kc_refs[g][0].reshape(ncol, HEAD_DIM).astype(BF16)
        vals.append(vc_refs[g][0].reshape(ncol, HEAD_DIM).astype(BF16))
        scores.append(_dot_nt(qb, kf) + bias)
        offs.append(slope * jnp.asarray(past - (p * group + g) * page, F32))
    update(scores, offs, vals)

    @pl.when(p == n_pages // group - 1)
    def _():
        ncn = tn * N_HEADS
        rr = lax.broadcasted_iota(jnp.int32, (nrow, HEAD_DIM), 0)
        cc = lax.broadcasted_iota(jnp.int32, (nrow, HEAD_DIM), 1)
        tq_ = _idiv(_imod(rr, ncn), N_HEADS)
        tk_ = _idiv(cc, N_HEADS)
        ok = (cc < ncn) & (_imod(rr, N_HEADS) == _imod(cc, N_HEADS)) & (tk_ <= tq_)
        bias_new = jnp.where(ok, slope * (tk_ - tq_).astype(F32), NEG_INF)
        pad = jnp.zeros((HEAD_DIM - ncn, HEAD_DIM), BF16)
        kn = jnp.concatenate([kn_ref[0].astype(BF16), pad], axis=0)
        vn = jnp.concatenate([vn_ref[0].astype(BF16), pad], axis=0)
        update([_dot_nt(qb, kn) + bias_new], [jnp.zeros((nrow, 1), F32)], [vn])
        lam = _lambda_full(lam_ref, lam_init)
        o = acc_ref[...] / jnp.sum(l_ref[...], axis=-1, keepdims=True)
        d = o[:ncn] - lam * o[ncn:]
        o_ref[0] = (_rms(d, subw_ref[...]) * (1.0 - lam_init)).astype(o_ref.dtype)


def _attn_sample(page_table, lam_rows, subln_w, q, k_new, v_new, cache_k, cache_v, *, lam_init, group):
    n_seq, n_pages = page_table.shape
    page = cache_k.shape[1]
    tn = q.shape[1] // N_HEADS
    nrow = 2 * tn * N_HEADS
    ncol = page * N_HEADS
    assert n_pages % group == 0

    def page_spec(g):
        return pl.BlockSpec((1, page, N_HEADS, HEAD_DIM),
                            lambda b, p, pt: (pt[b, p * group + g], 0, 0, 0))

    new_spec = pl.BlockSpec((1, tn * N_HEADS, HEAD_DIM), lambda b, p, pt: (b, 0, 0))
    grid_spec = pltpu.PrefetchScalarGridSpec(
        num_scalar_prefetch=1,
        grid=(n_seq, n_pages // group),
        in_specs=[
            pl.BlockSpec((4, QK_DIM), lambda b, p, pt: (0, 0)),
            pl.BlockSpec((1, HEAD_DIM), lambda b, p, pt: (0, 0)),
            new_spec, new_spec, new_spec,
        ] + [page_spec(g) for g in range(group)] * 2,
        out_specs=new_spec,
        scratch_shapes=[
            pltpu.VMEM((nrow, HEAD_DIM), BF16),
            pltpu.VMEM((nrow, ncol), F32),
            pltpu.VMEM((nrow, 1), F32),
            pltpu.VMEM((nrow, HEAD_DIM), F32),
            pltpu.VMEM((nrow, HEAD_DIM), F32),
        ],
    )
    return pl.pallas_call(
        functools.partial(_attn_sample_kernel, n_pages=n_pages, page=page, tn=tn, group=group,
                          lam_init=lam_init),
        grid_spec=grid_spec,
        out_shape=jax.ShapeDtypeStruct((n_seq, tn * N_HEADS, HEAD_DIM), BF16),
        compiler_params=pltpu.CompilerParams(
            dimension_semantics=("parallel", "arbitrary"), vmem_limit_bytes=V7X_VMEM_LIMIT_BYTES),
        name="attn_sample",
    )(page_table, lam_rows, subln_w, q, k_new, v_new, *([cache_k] * group), *([cache_v] * group))


def _gdn_n_masks(chunk):
    return 3 + (chunk.bit_length() - 1)


def _gdn_fill_masks(mask_ref, *, rows, chunk):
    ri = lax.broadcasted_iota(jnp.int32, (rows, rows), 0)
    ci = lax.broadcasted_iota(jnp.int32, (rows, rows), 1)
    same = _idiv(ri, chunk) == _idiv(ci, chunk)
    mask_ref[0] = (same & (ri >= ci)).astype(F32)
    mask_ref[1] = (same & (ri > ci)).astype(F32)
    mask_ref[2] = (ri == ci).astype(F32)
    s, k = 1, 3
    while s < chunk:
        mask_ref[k] = ((_idiv(ri, 2 * s) == _idiv(ci, 2 * s)) & (_imod(_idiv(ri, s), 2) == 1)
                       & (_imod(_idiv(ci, s), 2) == 0)).astype(F32)
        s, k = 2 * s, k + 1


def _gdn_intra_all(act, gates, alog_row, dtb_row, mask_ref, u_ref, w_ref, qg_ref, kdt_ref, qkc_ref,
                   egl_ref, *, rows, chunk):
    n_chunks = rows // chunk
    n_levels = _gdn_n_masks(chunk) - 3
    beta_all = jax.nn.sigmoid(gates)
    g_all = -jnp.exp(alog_row) * _softplus(gates + dtb_row)
    gcum_all = _dot_sel_l(mask_ref[0].astype(BF16), g_all)
    gtot_all = jnp.concatenate(
        [jnp.broadcast_to(gcum_all[(c + 1) * chunk - 1:(c + 1) * chunk], (chunk, HEAD_DIM))
         for c in range(n_chunks)], axis=0)
    gcum_t = gcum_all.T
    fr = lax.broadcasted_iota(jnp.int32, (rows, chunk), 0)
    fc = lax.broadcasted_iota(jnp.int32, (rows, chunk), 1)
    fold = (_imod(fr, chunk) == fc).astype(BF16)
    er = lax.broadcasted_iota(jnp.int32, (HEAD_DIM, HEAD_DIM), 0)
    eye_b = (er == lax.broadcasted_iota(jnp.int32, (HEAD_DIM, HEAD_DIM), 1)).astype(BF16)

    def lanes(col):
        return jnp.broadcast_to(col, (rows, HEAD_DIM))

    def square(x):
        if rows >= HEAD_DIM:
            return jnp.concatenate([x] * (rows // HEAD_DIM), axis=1)
        return x[:, :rows]

    heads = range(N_HEADS)
    egl_rows, lmats, lbs, tinvs, rhss = [], [], [], [], []
    for h in heads:
        cols = slice(h * HEAD_DIM, (h + 1) * HEAD_DIM)
        aq = act[:, h * HEAD_DIM:(h + 1) * HEAD_DIM]
        ak = act[:, (N_HEADS + h) * HEAD_DIM:(N_HEADS + h + 1) * HEAD_DIM]
        av = act[:, (2 * N_HEADS + h) * HEAD_DIM:(2 * N_HEADS + h + 1) * HEAD_DIM]
        qn = aq * lax.rsqrt(jnp.sum(aq * aq, axis=-1, keepdims=True) + 1e-6) * (HEAD_DIM ** -0.5)
        kn = ak * lax.rsqrt(jnp.sum(ak * ak, axis=-1, keepdims=True) + 1e-6)
        beta = lanes(beta_all[:, h:h + 1])
        g_cum = lanes(gcum_all[:, N_HEADS + h:N_HEADS + h + 1])
        g_tot = lanes(gtot_all[:, N_HEADS + h:N_HEADS + h + 1])
        g_row = gcum_t[N_HEADS + h:N_HEADS + h + 1, :]
        decay = jnp.exp(jnp.minimum(square(g_cum) - g_row, 0.0)) * mask_ref[0]

        kbeta = kn * beta
        knb = kn.astype(BF16)
        lmat = mask_ref[1] * (_dot_nt(kbeta.astype(BF16), knb) * decay)
        lmats.append(lmat)
        lbs.append(lmat.astype(BF16))
        tinvs.append(mask_ref[2] - lmat * mask_ref[3])
        e_cum = jnp.exp(g_cum)
        rhss.append(jnp.concatenate([av * beta, kbeta * e_cum], axis=1).astype(BF16))
        qk = (_dot_nt(qn.astype(BF16), knb) * decay).astype(BF16)
        qkc_ref[h] = _dot(qk, fold).astype(BF16)
        qg_ref[:, cols] = (qn * e_cum).astype(BF16)
        kd = (kn * jnp.exp(g_tot - g_cum)).astype(BF16)
        kdt_ref[h] = _dot_nt(eye_b, kd).astype(BF16)
        egl = jnp.exp(g_tot)
        egl_rows.append(jnp.concatenate(
            [egl[c * chunk:c * chunk + 1] for c in range(n_chunks)]
            + [jnp.zeros((8 - n_chunks, HEAD_DIM), F32)] * (n_chunks < 8), axis=0))
    egl_ref[0] = jnp.concatenate(egl_rows, axis=1)

    for k in range(1, n_levels):
        tbs = [tinvs[h].astype(BF16) for h in heads]
        mids = [_dot(tbs[h], lbs[h]).astype(BF16) for h in heads]
        tinvs = [tinvs[h] - mask_ref[3 + k] * _dot(mids[h], tbs[h]) for h in heads]
    resids = [mask_ref[2] - tinvs[h] - _dot_hp(lmats[h], tinvs[h]) for h in heads]
    tinvs = [tinvs[h] + _dot(tinvs[h].astype(BF16), resids[h].astype(BF16)) for h in heads]
    for h in heads:
        cols = slice(h * HEAD_DIM, (h + 1) * HEAD_DIM)
        uw = _dot(tinvs[h].astype(BF16), rhss[h])
        u_ref[:, cols] = uw[:, :HEAD_DIM]
        w_ref[:, cols] = uw[:, HEAD_DIM:].astype(BF16)


def _gdn_intra_prompt_kernel(x_ref, gates_ref, cw_ref, alog_ref, dtb_ref,
                             u_ref, w_ref, qg_ref, kdt_ref, qkc_ref, egl_ref, pad_ref, mask_ref,
                             *, rows, chunk):
    t = pl.program_id(1)

    @pl.when(t == 0)
    def _():
        pad_ref[0:8, :] = jnp.zeros((8, pad_ref.shape[1]), F32)
        _gdn_fill_masks(mask_ref, rows=rows, chunk=chunk)

    pad_ref[8:, :] = x_ref[...]
    cw = cw_ref[...]
    acc = pad_ref[pl.ds(8 - (CONV_W - 1), rows), :] * cw[0:1]
    for j in range(1, CONV_W):
        acc = acc + pad_ref[pl.ds(8 - (CONV_W - 1) + j, rows), :] * cw[j:j + 1]
    pad_ref[0:8, :] = x_ref[rows - 8:, :]
    _gdn_intra_all(_silu(acc), gates_ref[...], alog_ref[...], dtb_ref[...], mask_ref,
                   u_ref, w_ref, qg_ref, kdt_ref, qkc_ref, egl_ref, rows=rows, chunk=chunk)


def _gdn_intra_sample_kernel(x_ref, gates_ref, cw_ref, alog_ref, dtb_ref, buf_ref,
                             u_ref, w_ref, qg_ref, kdt_ref, qkc_ref, egl_ref, pad_ref, mask_ref,
                             *, n_seq, tn):
    seg = 8 + tn
    _gdn_fill_masks(mask_ref, rows=n_seq * tn, chunk=tn)
    cw = cw_ref[...]
    for s in range(n_seq):
        pad_ref[s * seg + 8 - (CONV_W - 1):s * seg + 8, :] = buf_ref[s]
        pad_ref[s * seg + 8:(s + 1) * seg, :] = x_ref[s * tn:(s + 1) * tn, :]
    pieces = []
    for s in range(n_seq):
        base = s * seg + 8 - (CONV_W - 1)
        acc = pad_ref[pl.ds(base, tn), :] * cw[0:1]
        for j in range(1, CONV_W):
            acc = acc + pad_ref[pl.ds(base + j, tn), :] * cw[j:j + 1]
        pieces.append(acc)
    _gdn_intra_all(_silu(jnp.concatenate(pieces, axis=0)), gates_ref[...], alog_ref[...], dtb_ref[...],
                   mask_ref, u_ref, w_ref, qg_ref, kdt_ref, qkc_ref, egl_ref, rows=n_seq * tn, chunk=tn)


def _intra_out(m, chunk, n_steps):
    width = N_HEADS * HEAD_DIM
    return [
        jax.ShapeDtypeStruct((m, width), F32),
        jax.ShapeDtypeStruct((m, width), BF16),
        jax.ShapeDtypeStruct((m, width), BF16),
        jax.ShapeDtypeStruct((N_HEADS, HEAD_DIM, m), BF16),
        jax.ShapeDtypeStruct((N_HEADS, m, chunk), BF16),
        jax.ShapeDtypeStruct((n_steps, 8, width), F32),
    ]


def _gdn_intra_prompt(proj, gates, conv_w, alog_row, dtb_row, *, batch, t):
    m = proj.shape[0]
    rows = GDN_ROWS
    nblk = t // rows
    width = N_HEADS * HEAD_DIM
    cwid = 3 * width

    def rb(b, i):
        return b * nblk + i

    in_specs = [
        pl.BlockSpec((rows, cwid), lambda b, i: (rb(b, i), 1)),
        pl.BlockSpec((rows, HEAD_DIM), lambda b, i: (rb(b, i), 0)),
        pl.BlockSpec((CONV_W, cwid), lambda b, i: (0, 0)),
        pl.BlockSpec((1, HEAD_DIM), lambda b, i: (0, 0)),
        pl.BlockSpec((1, HEAD_DIM), lambda b, i: (0, 0)),
    ]
    out_specs = [
        pl.BlockSpec((rows, width), lambda b, i: (rb(b, i), 0)),
        pl.BlockSpec((rows, width), lambda b, i: (rb(b, i), 0)),
        pl.BlockSpec((rows, width), lambda b, i: (rb(b, i), 0)),
        pl.BlockSpec((N_HEADS, HEAD_DIM, rows), lambda b, i: (0, 0, rb(b, i))),
        pl.BlockSpec((N_HEADS, rows, CHUNK), lambda b, i: (0, rb(b, i), 0)),
        pl.BlockSpec((1, 8, width), lambda b, i: (rb(b, i), 0, 0)),
    ]
    return pl.pallas_call(
        functools.partial(_gdn_intra_prompt_kernel, rows=rows, chunk=CHUNK),
        grid=(batch, nblk),
        in_specs=in_specs,
        out_specs=out_specs,
        out_shape=_intra_out(m, CHUNK, batch * nblk),
        scratch_shapes=[pltpu.VMEM((rows + 8, cwid), F32),
                        pltpu.VMEM((_gdn_n_masks(CHUNK), rows, rows), F32)],
        compiler_params=pltpu.CompilerParams(
            dimension_semantics=("parallel", "arbitrary"), vmem_limit_bytes=V7X_VMEM_LIMIT_BYTES),
        name="gdn_intra_prompt",
    )(proj, gates, conv_w, alog_row, dtb_row)


def _gdn_intra_sample(proj, gates, conv_w, alog_row, dtb_row, conv_state, *, n_seq, tn):
    m = proj.shape[0]
    width = N_HEADS * HEAD_DIM
    cwid = 3 * width
    in_specs = [
        pl.BlockSpec((m, cwid), lambda i: (0, 1)),
        pl.BlockSpec((m, HEAD_DIM), lambda i: (0, 0)),
        pl.BlockSpec((CONV_W, cwid), lambda i: (0, 0)),
        pl.BlockSpec((1, HEAD_DIM), lambda i: (0, 0)),
        pl.BlockSpec((1, HEAD_DIM), lambda i: (0, 0)),
        pl.BlockSpec((n_seq, CONV_W - 1, cwid), lambda i: (0, 0, 0)),
    ]
    out_specs = [
        pl.BlockSpec((m, width), lambda i: (0, 0)),
        pl.BlockSpec((m, width), lambda i: (0, 0)),
        pl.BlockSpec((m, width), lambda i: (0, 0)),
        pl.BlockSpec((N_HEADS, HEAD_DIM, m), lambda i: (0, 0, 0)),
        pl.BlockSpec((N_HEADS, m, tn), lambda i: (0, 0, 0)),
        pl.BlockSpec((1, 8, width), lambda i: (0, 0, 0)),
    ]
    return pl.pallas_call(
        functools.partial(_gdn_intra_sample_kernel, n_seq=n_seq, tn=tn),
        grid=(1,),
        in_specs=in_specs,
        out_specs=out_specs,
        out_shape=_intra_out(m, tn, 1),
        scratch_shapes=[pltpu.VMEM((n_seq * (8 + tn), cwid), F32),
                        pltpu.VMEM((_gdn_n_masks(tn), m, m), F32)],
        compiler_params=pltpu.CompilerParams(
            dimension_semantics=("arbitrary",), vmem_limit_bytes=V7X_VMEM_LIMIT_BYTES),
        name="gdn_intra_sample",
    )(proj, gates, conv_w, alog_row, dtb_row, conv_state)


def _gdn_scan_kernel(u_ref, w_ref, qg_ref, kdt_ref, qkc_ref, egl_ref, z_ref, s0_ref, dnw_ref,
                     o_ref, s_ref, *, chunk, n_chunks, seqs_per_step, egl_chunks):
    t = pl.program_id(1)

    @pl.when(t == 0)
    def _():
        s_ref[...] = s0_ref[...]

    dnw = dnw_ref[...]
    for c in range(n_chunks):
        si = c if seqs_per_step > 1 else 0
        rows = slice(c * chunk, (c + 1) * chunk)
        heads = range(N_HEADS)
        hcols = [slice(h * HEAD_DIM, (h + 1) * HEAD_DIM) for h in heads]
        s_old = [s_ref[si, h] for h in heads]
        both = [_dot(jnp.concatenate([w_ref[rows, hcols[h]], qg_ref[rows, hcols[h]]], axis=0),
                     s_old[h].astype(BF16)) for h in heads]
        vn = [(u_ref[rows, hcols[h]] - both[h][:chunk]).astype(BF16) for h in heads]
        o = [both[h][chunk:] + _dot(qkc_ref[h, rows, :], vn[h]) for h in heads]
        for h in heads:
            decay = egl_ref[c // egl_chunks, c % egl_chunks:c % egl_chunks + 1, hcols[h]]
            s_ref[si, h] = s_old[h] * decay + _dot(kdt_ref[h, :, rows], vn[h])
        for h in heads:
            o_ref[rows, hcols[h]] = (_rms(o[h], dnw) * _silu(z_ref[rows, hcols[h]])).astype(o_ref.dtype)


def _gdn_scan(intra, z_src, z_col_block, s0, dnw, *, n_seq, t, chunk, chunks_per_step, egl_chunks,
              seqs_per_step=1):
    u, w, qg, kdt, qkc, egl = intra
    m = u.shape[0]
    width = N_HEADS * HEAD_DIM
    rows = chunk * chunks_per_step
    if seqs_per_step > 1:
        assert seqs_per_step == n_seq == chunks_per_step and t == chunk
        nsteps, n_outer = 1, 1
    else:
        nsteps, n_outer = t // rows, n_seq

    def rb(b, i):
        return b * nsteps + i

    in_specs = [
        pl.BlockSpec((rows, width), lambda b, i: (rb(b, i), 0)),
        pl.BlockSpec((rows, width), lambda b, i: (rb(b, i), 0)),
        pl.BlockSpec((rows, width), lambda b, i: (rb(b, i), 0)),
        pl.BlockSpec((N_HEADS, HEAD_DIM, rows), lambda b, i: (0, 0, rb(b, i))),
        pl.BlockSpec((N_HEADS, rows, chunk), lambda b, i: (0, rb(b, i), 0)),
        pl.BlockSpec((chunks_per_step // egl_chunks, 8, width), lambda b, i: (rb(b, i), 0, 0)),
        pl.BlockSpec((rows, width), lambda b, i: (rb(b, i), z_col_block)),
        pl.BlockSpec((seqs_per_step, N_HEADS, HEAD_DIM, HEAD_DIM), lambda b, i: (b, 0, 0, 0)),
        pl.BlockSpec((1, HEAD_DIM), lambda b, i: (0, 0)),
    ]
    out_specs = [
        pl.BlockSpec((rows, width), lambda b, i: (rb(b, i), 0)),
        pl.BlockSpec((seqs_per_step, N_HEADS, HEAD_DIM, HEAD_DIM), lambda b, i: (b, 0, 0, 0)),
    ]
    return pl.pallas_call(
        functools.partial(_gdn_scan_kernel, chunk=chunk, n_chunks=chunks_per_step,
                          seqs_per_step=seqs_per_step, egl_chunks=egl_chunks),
        grid=(n_outer, nsteps),
        in_specs=in_specs,
        out_specs=out_specs,
        out_shape=[jax.ShapeDtypeStruct((m, width), BF16),
                   jax.ShapeDtypeStruct((n_seq, N_HEADS, HEAD_DIM, HEAD_DIM), F32)],
        compiler_params=pltpu.CompilerParams(
            dimension_semantics=("parallel", "arbitrary"), vmem_limit_bytes=V7X_VMEM_LIMIT_BYTES),
        name="gdn_scan",
    )(u, w, qg, kdt, qkc, egl, z_src, s0, dnw)


def _pad_lanes(v, offset):
    out = jnp.zeros((1, HEAD_DIM), F32)
    return lax.dynamic_update_slice(out, v.reshape(1, -1).astype(F32), (0, offset))


def kernel(x_prompt, x_sample, cache_k, cache_v, state_ssm, state_conv, page_table, ffn1_pre_w, ffn1_post_w, ffn1_gate, ffn1_up, ffn1_down, mix_pre_w, mix_post_w, w_in, conv_w, a_log, dt_bias, delta_norm_w, lambda_q1, lambda_k1, lambda_q2, lambda_k2, subln_w, w_out, ffn2_pre_w, ffn2_post_w, ffn2_gate, ffn2_up, ffn2_down):
    depth = w_in.shape[0]
    batch, t, d = x_prompt.shape
    n_seq, tn, _ = x_sample.shape
    a_w = N_HEADS * HEAD_DIM
    n_main = 3 * a_w + 3 * a_w + a_w
    yp = x_prompt.reshape(batch * t, d)
    ys = x_sample.reshape(n_seq * tn, d)
    outs = [[] for _ in range(8)]
    w_in_t = jnp.swapaxes(w_in, 1, 2)
    for l in range(depth):
        lam_init = 0.8 - 0.6 * math.exp(-0.3 * l)
        lam_rows = jnp.stack([lambda_q1[l], lambda_k1[l], lambda_q2[l], lambda_k2[l]]).astype(F32)
        row = lambda v: v[l].reshape(1, -1).astype(F32)
        w_gates = jnp.pad(w_in_t[l, n_main:], ((0, HEAD_DIM - 2 * N_HEADS), (0, 0))).astype(BF16)
        alog_row = _pad_lanes(a_log[l], N_HEADS)
        dtb_row = _pad_lanes(dt_bias[l], N_HEADS)
        dnw = row(delta_norm_w)
        subw = row(subln_w)
        cw = conv_w[l].astype(F32)
        ms = n_seq * tn

        ys, wg1, wu1, wd1 = _ffn(ys, row(ffn1_pre_w), row(ffn1_post_w), ffn1_gate[l], ffn1_up[l],
                                 ffn1_down[l], tm=ms, tf=512, emit_bf16=True)
        yp, hp = _ffn(yp, row(ffn1_pre_w), row(ffn1_post_w), wg1, wu1, wd1, tm=512, tf=512,
                      next_pre_w=row(mix_pre_w))

        ps, gs, ks, vs, w_main = _inproj(ys, row(mix_pre_w), w_in_t, w_gates, tm=ms, tn=a_w, n=n_main,
                                         layer=l, emit_bf16=True)
        qs = ps[:, :a_w].reshape(n_seq, tn * N_HEADS, HEAD_DIM)
        ks = ks.reshape(n_seq, tn * N_HEADS, HEAD_DIM)
        vs = vs.reshape(n_seq, tn * N_HEADS, HEAD_DIM)
        oa_s = _attn_sample(page_table, lam_rows, subw, qs, ks, vs, cache_k[l], cache_v[l],
                            lam_init=lam_init, group=16)
        oa_s = oa_s.reshape(ms, a_w)
        intra_s = _gdn_intra_sample(ps, gs, cw, alog_row, dtb_row, state_conv[l], n_seq=n_seq, tn=tn)
        ob_s, s_s = _gdn_scan(intra_s, ps, 6, state_ssm[l].astype(F32), dnw, n_seq=n_seq, t=tn,
                              chunk=tn, chunks_per_step=n_seq, egl_chunks=n_seq, seqs_per_step=n_seq)
        ys, wo = _outproj(ys, oa_s, ob_s, w_out[l], row(mix_post_w), tm=ms, emit_bf16=True)

        pp, gp, kp, vp = _inproj(hp, row(mix_pre_w), w_main, w_gates, tm=1024, tn=a_w, n=n_main)
        oa_p = _attn_prompt(lam_rows, subw.reshape(HEAD_DIM, 1), pp, kp, vp, batch=batch, t=t,
                            lam_init=lam_init, tq=256, hpp=2)
        intra_p = _gdn_intra_prompt(pp, gp, cw, alog_row, dtb_row, batch=batch, t=t)
        zero_s = jnp.zeros((batch, N_HEADS, HEAD_DIM, HEAD_DIM), F32)
        ob_p, s_p = _gdn_scan(intra_p, pp, 6, zero_s, dnw, n_seq=batch, t=t, chunk=CHUNK,
                              chunks_per_step=GDN_SCAN_ROWS // CHUNK, egl_chunks=GDN_ROWS // CHUNK)
        yp = _outproj(yp, oa_p, ob_p, wo, row(mix_post_w), tm=512)

        ys, wg2, wu2, wd2 = _ffn(ys, row(ffn2_pre_w), row(ffn2_post_w), ffn2_gate[l], ffn2_up[l],
                                 ffn2_down[l], tm=ms, tf=512, emit_bf16=True)
        yp = _ffn(yp, row(ffn2_pre_w), row(ffn2_post_w), wg2, wu2, wd2, tm=512, tf=512)

        ppb = pp.reshape(batch, t, -1)
        psb = ps.reshape(n_seq, tn, -1)
        outs[0].append(kp.reshape(batch, t, N_HEADS, HEAD_DIM))
        outs[1].append(vp.reshape(batch, t, N_HEADS, HEAD_DIM))
        outs[2].append(s_p.astype(state_ssm.dtype))
        outs[3].append(ppb[:, t - (CONV_W - 1):, 3 * a_w:6 * a_w])
        outs[4].append(ks.reshape(n_seq, tn, N_HEADS, HEAD_DIM))
        outs[5].append(vs.reshape(n_seq, tn, N_HEADS, HEAD_DIM))
        outs[6].append(s_s.astype(state_ssm.dtype))
        outs[7].append(jnp.concatenate([state_conv[l].astype(psb.dtype), psb[:, :, 3 * a_w:6 * a_w]],
                                       axis=1)[:, tn:])
    return (yp.reshape(batch, t, d), ys.reshape(n_seq, tn, d)) + tuple(jnp.stack(o) for o in outs)
```

```python
import functools
import math

import jax
import jax.numpy as jnp
from jax import lax
from jax.experimental import pallas as pl
from jax.experimental.pallas import tpu as pltpu

F32 = jnp.float32
BF16 = jnp.bfloat16
EPS = 1e-6
NEG_INF = float("-inf")
LOG2E = 1.4426950408889634

V7X_VMEM_LIMIT_BYTES = 60 * 1024 * 1024

QK_DIM = 64
HEAD_DIM = 128
N_HEADS = 8
CONV_W = 4
CHUNK = 64
GDN_ROWS = 128
GDN_SCAN_ROWS = 512

def _dot(a, b):
    return jnp.dot(a, b, preferred_element_type=F32)


def _dot_nt(a, b):
    return lax.dot_general(a, b, (((1,), (1,)), ((), ())), preferred_element_type=F32)


def _split3(x):
    hi = x.astype(BF16)
    r = x - hi.astype(F32)
    mid = r.astype(BF16)
    lo = (r - mid.astype(F32)).astype(BF16)
    return hi, mid, lo


def _dot_sel_l(sel, x):
    hi, mid, lo = _split3(x)
    return _dot(sel, hi) + _dot(sel, mid) + _dot(sel, lo)


def _dot_hp(a, b):
    ah = a.astype(BF16)
    al = (a - ah.astype(F32)).astype(BF16)
    bh = b.astype(BF16)
    bl = (b - bh.astype(F32)).astype(BF16)
    return _dot(ah, bh) + _dot(ah, bl) + _dot(al, bh)


def _rms(x, w):
    return x * lax.rsqrt(jnp.mean(x * x, axis=-1, keepdims=True) + EPS) * w


def _silu(x):
    return x * jax.nn.sigmoid(x)


def _softplus(x):
    return jnp.maximum(x, 0.0) + jnp.log1p(jnp.exp(-jnp.abs(x)))


def _lambda_full(lam_ref, lam_init):
    l = lam_ref[...]
    a = jnp.sum(l[0:1] * l[1:2], axis=-1, keepdims=True)
    b = jnp.sum(l[2:3] * l[3:4], axis=-1, keepdims=True)
    return jnp.exp(a) - jnp.exp(b) + lam_init


def _idiv(x, n):
    assert n & (n - 1) == 0
    return x >> (n.bit_length() - 1)


def _imod(x, n):
    assert n & (n - 1) == 0
    return x & (n - 1)


def _pow2_neg(k):
    return lax.bitcast_convert_type((127 - k) << 23, F32)


def _ffn_kernel(*refs, nf, emit_bf16, emit_next):
    x_ref, prew_ref, postw_ref, wg_ref, wu_ref, wd_ref = refs[:6]
    rest = list(refs[6:])
    nextw_ref = rest.pop(0) if emit_next else None
    o_ref = rest.pop(0)
    cast_refs = [rest.pop(0) for _ in range(3)] if emit_bf16 else None
    hn_ref = rest.pop(0) if emit_next else None
    h_ref, = rest
    f = pl.program_id(1)

    @pl.when(f == 0)
    def _():
        h_ref[...] = _rms(x_ref[...], prew_ref[...]).astype(BF16)
        o_ref[...] = jnp.zeros_like(o_ref)

    wg, wu, wd = wg_ref[...], wu_ref[...], wd_ref[...]
    if emit_bf16:
        wg, wu, wd = wg.astype(BF16), wu.astype(BF16), wd.astype(BF16)
        cast_refs[0][...], cast_refs[1][...], cast_refs[2][...] = wg, wu, wd
    h = h_ref[...]
    g = _dot(h, wg)
    u = _dot(h, wu)
    o_ref[...] += _dot((_silu(g) * u).astype(BF16), wd)

    @pl.when(f == nf - 1)
    def _():
        y = x_ref[...] + 0.5 * _rms(o_ref[...], postw_ref[...])
        o_ref[...] = y
        if emit_next:
            hn_ref[...] = _rms(y, nextw_ref[...]).astype(BF16)


def _ffn(x, pre_w, post_w, wg, wu, wd, *, tm, tf, emit_bf16=False, next_pre_w=None):
    m, d = x.shape
    ff = wg.shape[1]
    nf = ff // tf
    emit_next = next_pre_w is not None
    assert not emit_bf16 or m == tm
    gu_spec = pl.BlockSpec((d, tf), lambda i, f: (0, f))
    dn_spec = pl.BlockSpec((tf, d), lambda i, f: (f, 0))
    row_spec = pl.BlockSpec((tm, d), lambda i, f: (i, 0))
    vec_spec = pl.BlockSpec((1, d), lambda i, f: (0, 0))
    out_specs = [row_spec]
    out_shape = [jax.ShapeDtypeStruct((m, d), F32)]
    if emit_bf16:
        out_specs += [gu_spec, gu_spec, dn_spec]
        out_shape += [jax.ShapeDtypeStruct(w.shape, BF16) for w in (wg, wu, wd)]
    if emit_next:
        out_specs.append(row_spec)
        out_shape.append(jax.ShapeDtypeStruct((m, d), BF16))
    outs = pl.pallas_call(
        functools.partial(_ffn_kernel, nf=nf, emit_bf16=emit_bf16, emit_next=emit_next),
        grid=(m // tm, nf),
        in_specs=[row_spec, vec_spec, vec_spec, gu_spec, gu_spec, dn_spec] + [vec_spec] * emit_next,
        out_specs=out_specs,
        out_shape=out_shape,
        scratch_shapes=[pltpu.VMEM((tm, d), BF16)],
        compiler_params=pltpu.CompilerParams(
            dimension_semantics=("parallel", "arbitrary"), vmem_limit_bytes=V7X_VMEM_LIMIT_BYTES),
        name="ffn",
    )(x, pre_w, post_w, wg, wu, wd, *([next_pre_w] * emit_next))
    return outs if len(outs) > 1 else outs[0]


def _inproj_kernel(x_ref, prew_ref, wt_ref, wgt_ref, o_ref, og_ref, ko_ref, vo_ref, *rest, emit_bf16):
    normed = x_ref.dtype == BF16
    h_ref = x_ref if normed else rest[-1]
    j = pl.program_id(1)

    @pl.when(j == 0)
    def _():
        if not normed:
            h_ref[...] = _rms(x_ref[...], prew_ref[...]).astype(BF16)
        og_ref[...] = _dot_nt(h_ref[...], wgt_ref[...])

    wt = wt_ref[...]
    if emit_bf16:
        wt = wt.astype(BF16)
        rest[0][...] = wt
    y = _dot_nt(h_ref[...], wt)
    o_ref[...] = y

    @pl.when(j == 1)
    def _():
        ko_ref[...] = y

    @pl.when(j == 2)
    def _():
        vo_ref[...] = y


def _inproj(x, pre_w, w_main_t, w_gates_t, *, tm, tn, n, layer=None, emit_bf16=False):
    m, d = x.shape
    ng = w_gates_t.shape[0]
    assert not emit_bf16 or m == tm
    w_spec = pl.BlockSpec((tn, d), lambda i, j: (j, 0))
    w_in_spec = w_spec if layer is None else pl.BlockSpec((None, tn, d), lambda i, j: (layer, j, 0))
    row_spec = pl.BlockSpec((tm, tn), lambda i, j: (i, 0))
    out_specs = [pl.BlockSpec((tm, tn), lambda i, j: (i, j)), pl.BlockSpec((tm, ng), lambda i, j: (i, 0)),
                 row_spec, row_spec]
    out_shape = [jax.ShapeDtypeStruct((m, n), F32), jax.ShapeDtypeStruct((m, ng), F32),
                 jax.ShapeDtypeStruct((m, tn), F32), jax.ShapeDtypeStruct((m, tn), F32)]
    if emit_bf16:
        out_specs.append(w_spec)
        out_shape.append(jax.ShapeDtypeStruct((n, d), BF16))
    return pl.pallas_call(
        functools.partial(_inproj_kernel, emit_bf16=emit_bf16),
        grid=(m // tm, n // tn),
        in_specs=[
            pl.BlockSpec((tm, d), lambda i, j: (i, 0),
                         pipeline_mode=pl.Buffered(1 if tm * d * x.dtype.itemsize >= 8 * 1024 * 1024 else 2)),
            pl.BlockSpec((1, d), lambda i, j: (0, 0)),
            w_in_spec,
            pl.BlockSpec((ng, d), lambda i, j: (0, 0)),
        ],
        out_specs=out_specs,
        out_shape=out_shape,
        scratch_shapes=[pltpu.VMEM((tm, d), BF16)],
        compiler_params=pltpu.CompilerParams(
            dimension_semantics=("parallel", "arbitrary"), vmem_limit_bytes=V7X_VMEM_LIMIT_BYTES),
        name="inproj",
    )(x, pre_w, w_main_t, w_gates_t)


def _outproj_kernel(x_ref, oa_ref, ob_ref, wa_ref, wb_ref, postw_ref, o_ref, *rest, emit_bf16):
    wa, wb = wa_ref[...], wb_ref[...]
    if emit_bf16:
        wa, wb = wa.astype(BF16), wb.astype(BF16)
        ka = wa.shape[0]
        rest[0][:ka, :] = wa
        rest[0][ka:, :] = wb
    y = _dot(oa_ref[...], wa) + _dot(ob_ref[...], wb)
    o_ref[...] = x_ref[...] + _rms(y, postw_ref[...])


def _outproj(x, oa, ob, w_out, post_w, *, tm, emit_bf16=False):
    m, d = x.shape
    ka = oa.shape[1]
    assert not emit_bf16 or m == tm
    out_specs = [pl.BlockSpec((tm, d), lambda i: (i, 0))]
    out_shape = [jax.ShapeDtypeStruct((m, d), F32)]
    if emit_bf16:
        out_specs.append(pl.BlockSpec(w_out.shape, lambda i: (0, 0)))
        out_shape.append(jax.ShapeDtypeStruct(w_out.shape, BF16))
    outs = pl.pallas_call(
        functools.partial(_outproj_kernel, emit_bf16=emit_bf16),
        grid=(m // tm,),
        in_specs=[
            pl.BlockSpec((tm, d), lambda i: (i, 0)),
            pl.BlockSpec((tm, ka), lambda i: (i, 0)),
            pl.BlockSpec((tm, ka), lambda i: (i, 0)),
            pl.BlockSpec((ka, d), lambda i: (0, 0)),
            pl.BlockSpec((ka, d), lambda i: (1, 0)),
            pl.BlockSpec((1, d), lambda i: (0, 0)),
        ],
        out_specs=out_specs,
        out_shape=out_shape,
        compiler_params=pltpu.CompilerParams(
            dimension_semantics=("parallel",), vmem_limit_bytes=V7X_VMEM_LIMIT_BYTES),
        name="outproj",
    )(x, oa, ob, w_out, w_out, post_w)
    return outs if emit_bf16 else outs[0]


def _attn_prompt_kernel(lam_ref, subw_ref, q_ref, k_ref, v_ref, o_ref,
                        kb_ref, vt_ref, bias_ref, *, t, tq, hpp, lam_init):
    hg = pl.program_id(1)
    nq = t // tq
    lam = _lambda_full(lam_ref, lam_init)
    lane = lax.broadcasted_iota(jnp.int32, (1, HEAD_DIM), 1)
    first_map = lane < QK_DIM
    kk = lax.broadcasted_iota(jnp.int32, (tq, tq), 0)
    qq = lax.broadcasted_iota(jnp.int32, (tq, tq), 1)
    slopes = [_pow2_neg(jnp.full((1, 1), hg * hpp + hh + 1, jnp.int32)) * LOG2E for hh in range(hpp)]
    er = lax.broadcasted_iota(jnp.int32, (HEAD_DIM, HEAD_DIM), 0)
    eye_b = (er == lax.broadcasted_iota(jnp.int32, (HEAD_DIM, HEAD_DIM), 1)).astype(BF16)
    for hh in range(hpp):
        cols = slice(hh * HEAD_DIM, (hh + 1) * HEAD_DIM)
        b = slopes[hh] * (kk - qq).astype(F32)
        bias_ref[hh, 0] = b
        bias_ref[hh, 1] = jnp.where(qq >= kk, b, NEG_INF)
        vt_ref[hh] = _dot_nt(eye_b, v_ref[:, cols].astype(BF16)).astype(BF16)
    kb_ref[...] = k_ref[...].astype(BF16)
    scale = QK_DIM ** -0.5 * LOG2E
    subw_col = subw_ref[...]

    def q_block(i, _):
        rows_i = pl.ds(pl.multiple_of(i * tq, tq), tq)
        qs = []
        for hh in range(hpp):
            q = q_ref[rows_i, hh * HEAD_DIM:(hh + 1) * HEAD_DIM] * scale
            qs.append(jnp.concatenate(
                [jnp.where(first_map, q, 0.0), jnp.where(first_map, 0.0, q)], axis=0).astype(BF16))

        def scores(j):
            rows_j = pl.ds(pl.multiple_of(j * tq, tq), tq)
            diag = jnp.asarray(j == i, jnp.int32)
            out = []
            for hh in range(hpp):
                b = bias_ref[hh, diag]
                out.append(_dot_nt(kb_ref[rows_j, hh * HEAD_DIM:(hh + 1) * HEAD_DIM], qs[hh])
                           + jnp.concatenate([b, b], axis=1))
            return tuple(out)

        def consume(j, sts, state):
            rows_j = pl.ds(pl.multiple_of(j * tq, tq), tq)
            blocks_away = jnp.asarray((i - j) * tq, F32)
            hs = range(hpp)
            far = [slopes[hh] * blocks_away for hh in hs]
            m_new = [jnp.maximum(state[hh][0], jnp.max(sts[hh], axis=0, keepdims=True) - far[hh])
                     for hh in hs]
            p = [jnp.exp2(sts[hh] - (far[hh] + m_new[hh])) for hh in hs]
            alpha = [jnp.exp2(state[hh][0] - m_new[hh]) for hh in hs]
            l_new = [alpha[hh] * state[hh][1] + jnp.sum(p[hh], axis=0, keepdims=True) for hh in hs]
            pv = [_dot(vt_ref[hh, :, rows_j], p[hh].astype(BF16)) for hh in hs]
            return tuple((m_new[hh], l_new[hh], alpha[hh] * state[hh][2] + pv[hh]) for hh in hs)

        def step(j, carry):
            sts, state = carry
            nxt = scores(j + 1)
            return nxt, consume(j, sts, state)

        init = tuple((jnp.full((1, 2 * tq), NEG_INF, F32), jnp.zeros((1, 2 * tq), F32),
                      jnp.zeros((HEAD_DIM, 2 * tq), F32)) for _ in range(hpp))
        sts, state = lax.fori_loop(0, i, step, (scores(0), init))
        state = consume(i, sts, state)
        for hh in range(hpp):
            _, l_fin, acc_fin = state[hh]
            ot = acc_fin / l_fin
            dt = ot[:, :tq] - lam * ot[:, tq:]
            dt = dt * lax.rsqrt(jnp.mean(dt * dt, axis=0, keepdims=True) + EPS) * subw_col
            o_ref[rows_i, hh * HEAD_DIM:(hh + 1) * HEAD_DIM] = (
                (dt * (1.0 - lam_init)).T.astype(o_ref.dtype))
        return 0

    lax.fori_loop(0, nq, q_block, 0)


def _attn_prompt(lam_rows, subln_col, proj, k, v, *, batch, t, lam_init, tq, hpp):
    m = proj.shape[0]
    ng = N_HEADS // hpp
    w = hpp * HEAD_DIM
    return pl.pallas_call(
        functools.partial(_attn_prompt_kernel, t=t, tq=tq, hpp=hpp, lam_init=lam_init),
        grid=(batch, ng),
        in_specs=[
            pl.BlockSpec((4, QK_DIM), lambda b, g: (0, 0)),
            pl.BlockSpec((HEAD_DIM, 1), lambda b, g: (0, 0)),
            pl.BlockSpec((t, w), lambda b, g: (b, g)),
            pl.BlockSpec((t, w), lambda b, g: (b, g)),
            pl.BlockSpec((t, w), lambda b, g: (b, g)),
        ],
        out_specs=pl.BlockSpec((t, w), lambda b, g: (b, g)),
        out_shape=jax.ShapeDtypeStruct((m, N_HEADS * HEAD_DIM), BF16),
        scratch_shapes=[
            pltpu.VMEM((t, w), BF16),
            pltpu.VMEM((hpp, HEAD_DIM, t), BF16),
            pltpu.VMEM((hpp, 2, tq, tq), F32),
        ],
        compiler_params=pltpu.CompilerParams(
            dimension_semantics=("parallel", "parallel"), vmem_limit_bytes=V7X_VMEM_LIMIT_BYTES),
        name="attn_prompt",
    )(lam_rows, subln_col, proj, k, v)


def _attn_sample_kernel(pt_ref, lam_ref, subw_ref, q_ref, kn_ref, vn_ref, *rest,
                        n_pages, page, tn, group, lam_init):
    del pt_ref
    kc_refs = rest[:group]
    vc_refs = rest[group:2 * group]
    o_ref, qb_ref, bias_ref, m_ref, l_ref, acc_ref = rest[2 * group:]
    p = pl.program_id(1)
    nrow = 2 * tn * N_HEADS
    ncol = page * N_HEADS
    past = n_pages * page
    row = lax.broadcasted_iota(jnp.int32, (nrow, 1), 0)
    slope = _pow2_neg(_imod(row, N_HEADS) + 1) * LOG2E

    @pl.when(p == 0)
    def _():
        q = q_ref[0] * (QK_DIM ** -0.5 * LOG2E)
        lane = lax.broadcasted_iota(jnp.int32, (1, HEAD_DIM), 1)
        first_map = lane < QK_DIM
        qb_ref[...] = jnp.concatenate(
            [jnp.where(first_map, q, 0.0), jnp.where(first_map, 0.0, q)], axis=0).astype(BF16)
        rr = lax.broadcasted_iota(jnp.int32, (nrow, ncol), 0)
        cc = lax.broadcasted_iota(jnp.int32, (nrow, ncol), 1)
        tq_ = _idiv(_imod(rr, tn * N_HEADS), N_HEADS)
        same_head = _imod(rr, N_HEADS) == _imod(cc, N_HEADS)
        bias_ref[...] = jnp.where(same_head, slope * (_idiv(cc, N_HEADS) - tq_).astype(F32), NEG_INF)
        m_ref[...] = jnp.full(m_ref.shape, NEG_INF, F32)
        l_ref[...] = jnp.zeros(l_ref.shape, F32)
        acc_ref[...] = jnp.zeros(acc_ref.shape, F32)

    def lane_fold(x, op):
        out = x[:, :HEAD_DIM]
        for i in range(1, x.shape[1] // HEAD_DIM):
            out = op(out, x[:, i * HEAD_DIM:(i + 1) * HEAD_DIM])
        return out

    def update(scores, offs, vals):
        m_old = m_ref[...]
        m_loc = lane_fold(scores[0], jnp.maximum) - offs[0]
        for s, off in zip(scores[1:], offs[1:]):
            m_loc = jnp.maximum(m_loc, lane_fold(s, jnp.maximum) - off)
        m_new = jnp.maximum(m_old, jnp.max(m_loc, axis=-1, keepdims=True))
        alpha = jnp.exp2(m_old - m_new)
        pes = [jnp.exp2(s - (off + m_new)) for s, off in zip(scores, offs)]
        l_part = lane_fold(pes[0], jnp.add)
        for pe in pes[1:]:
            l_part = l_part + lane_fold(pe, jnp.add)
        pv = _dot(jnp.concatenate([pe.astype(BF16) for pe in pes], axis=1),
                  jnp.concatenate(vals, axis=0))
        l_ref[...] = alpha * l_ref[...] + l_part
        acc_ref[...] = alpha * acc_ref[...] + pv
        m_ref[...] = m_new

    qb = qb_ref[...]
    bias = bias_ref[...]
    scores, offs, vals = [], [], []
    for g in range(group):
        kf = kc_refs[g][0].reshape(ncol, HEAD_DIM).astype(BF16)
        vals.append(vc_refs[g][0].reshape(ncol, HEAD_DIM).astype(BF16))
        scores.append(_dot_nt(qb, kf) + bias)
        offs.append(slope * jnp.asarray(past - (p * group + g) * page, F32))
    update(scores, offs, vals)

    @pl.when(p == n_pages // group - 1)
    def _():
        ncn = tn * N_HEADS
        rr = lax.broadcasted_iota(jnp.int32, (nrow, HEAD_DIM), 0)
        cc = lax.broadcasted_iota(jnp.int32, (nrow, HEAD_DIM), 1)
        tq_ = _idiv(_imod(rr, ncn), N_HEADS)
        tk_ = _idiv(cc, N_HEADS)
        ok = (cc < ncn) & (_imod(rr, N_HEADS) == _imod(cc, N_HEADS)) & (tk_ <= tq_)
        bias_new = jnp.where(ok, slope * (tk_ - tq_).astype(F32), NEG_INF)
        pad = jnp.zeros((HEAD_DIM - ncn, HEAD_DIM), BF16)
        kn = jnp.concatenate([kn_ref[0].astype(BF16), pad], axis=0)
        vn = jnp.concatenate([vn_ref[0].astype(BF16), pad], axis=0)
        update([_dot_nt(qb, kn) + bias_new], [jnp.zeros((nrow, 1), F32)], [vn])
        lam = _lambda_full(lam_ref, lam_init)
        o = acc_ref[...] / jnp.sum(l_ref[...], axis=-1, keepdims=True)
        d = o[:ncn] - lam * o[ncn:]
        o_ref[0] = (_rms(d, subw_ref[...]) * (1.0 - lam_init)).astype(o_ref.dtype)


def _attn_sample(page_table, lam_rows, subln_w, q, k_new, v_new, cache_k, cache_v, *, lam_init, group):
    n_seq, n_pages = page_table.shape
    page = cache_k.shape[1]
    tn = q.shape[1] // N_HEADS
    nrow = 2 * tn * N_HEADS
    ncol = page * N_HEADS
    assert n_pages % group == 0

    def page_spec(g):
        return pl.BlockSpec((1, page, N_HEADS, HEAD_DIM),
                            lambda b, p, pt: (pt[b, p * group + g], 0, 0, 0))

    new_spec = pl.BlockSpec((1, tn * N_HEADS, HEAD_DIM), lambda b, p, pt: (b, 0, 0))
    grid_spec = pltpu.PrefetchScalarGridSpec(
        num_scalar_prefetch=1,
        grid=(n_seq, n_pages // group),
        in_specs=[
            pl.BlockSpec((4, QK_DIM), lambda b, p, pt: (0, 0)),
            pl.BlockSpec((1, HEAD_DIM), lambda b, p, pt: (0, 0)),
            new_spec, new_spec, new_spec,
        ] + [page_spec(g) for g in range(group)] * 2,
        out_specs=new_spec,
        scratch_shapes=[
            pltpu.VMEM((nrow, HEAD_DIM), BF16),
            pltpu.VMEM((nrow, ncol), F32),
            pltpu.VMEM((nrow, 1), F32),
            pltpu.VMEM((nrow, HEAD_DIM), F32),
            pltpu.VMEM((nrow, HEAD_DIM), F32),
        ],
    )
    return pl.pallas_call(
        functools.partial(_attn_sample_kernel, n_pages=n_pages, page=page, tn=tn, group=group,
                          lam_init=lam_init),
        grid_spec=grid_spec,
        out_shape=jax.ShapeDtypeStruct((n_seq, tn * N_HEADS, HEAD_DIM), BF16),
        compiler_params=pltpu.CompilerParams(
            dimension_semantics=("parallel", "arbitrary"), vmem_limit_bytes=V7X_VMEM_LIMIT_BYTES),
        name="attn_sample",
    )(page_table, lam_rows, subln_w, q, k_new, v_new, *([cache_k] * group), *([cache_v] * group))


def _gdn_n_masks(chunk):
    return 3 + (chunk.bit_length() - 1)


def _gdn_fill_masks(mask_ref, *, rows, chunk):
    ri = lax.broadcasted_iota(jnp.int32, (rows, rows), 0)
    ci = lax.broadcasted_iota(jnp.int32, (rows, rows), 1)
    same = _idiv(ri, chunk) == _idiv(ci, chunk)
    mask_ref[0] = (same & (ri >= ci)).astype(F32)
    mask_ref[1] = (same & (ri > ci)).astype(F32)
    mask_ref[2] = (ri == ci).astype(F32)
    s, k = 1, 3
    while s < chunk:
        mask_ref[k] = ((_idiv(ri, 2 * s) == _idiv(ci, 2 * s)) & (_imod(_idiv(ri, s), 2) == 1)
                       & (_imod(_idiv(ci, s), 2) == 0)).astype(F32)
        s, k = 2 * s, k + 1


def _gdn_intra_all(act, gates, alog_row, dtb_row, mask_ref, u_ref, w_ref, qg_ref, kdt_ref, qkc_ref,
                   egl_ref, *, rows, chunk):
    n_chunks = rows // chunk
    n_levels = _gdn_n_masks(chunk) - 3
    beta_all = jax.nn.sigmoid(gates)
    g_all = -jnp.exp(alog_row) * _softplus(gates + dtb_row)
    gcum_all = _dot_sel_l(mask_ref[0].astype(BF16), g_all)
    gtot_all = jnp.concatenate(
        [jnp.broadcast_to(gcum_all[(c + 1) * chunk - 1:(c + 1) * chunk], (chunk, HEAD_DIM))
         for c in range(n_chunks)], axis=0)
    gcum_t = gcum_all.T
    fr = lax.broadcasted_iota(jnp.int32, (rows, chunk), 0)
    fc = lax.broadcasted_iota(jnp.int32, (rows, chunk), 1)
    fold = (_imod(fr, chunk) == fc).astype(BF16)
    er = lax.broadcasted_iota(jnp.int32, (HEAD_DIM, HEAD_DIM), 0)
    eye_b = (er == lax.broadcasted_iota(jnp.int32, (HEAD_DIM, HEAD_DIM), 1)).astype(BF16)

    def lanes(col):
        return jnp.broadcast_to(col, (rows, HEAD_DIM))

    def square(x):
        if rows >= HEAD_DIM:
            return jnp.concatenate([x] * (rows // HEAD_DIM), axis=1)
        return x[:, :rows]

    heads = range(N_HEADS)
    egl_rows, lmats, lbs, tinvs, rhss = [], [], [], [], []
    for h in heads:
        cols = slice(h * HEAD_DIM, (h + 1) * HEAD_DIM)
        aq = act[:, h * HEAD_DIM:(h + 1) * HEAD_DIM]
        ak = act[:, (N_HEADS + h) * HEAD_DIM:(N_HEADS + h + 1) * HEAD_DIM]
        av = act[:, (2 * N_HEADS + h) * HEAD_DIM:(2 * N_HEADS + h + 1) * HEAD_DIM]
        qn = aq * lax.rsqrt(jnp.sum(aq * aq, axis=-1, keepdims=True) + 1e-6) * (HEAD_DIM ** -0.5)
        kn = ak * lax.rsqrt(jnp.sum(ak * ak, axis=-1, keepdims=True) + 1e-6)
        beta = lanes(beta_all[:, h:h + 1])
        g_cum = lanes(gcum_all[:, N_HEADS + h:N_HEADS + h + 1])
        g_tot = lanes(gtot_all[:, N_HEADS + h:N_HEADS + h + 1])
        g_row = gcum_t[N_HEADS + h:N_HEADS + h + 1, :]
        decay = jnp.exp(jnp.minimum(square(g_cum) - g_row, 0.0)) * mask_ref[0]

        kbeta = kn * beta
        knb = kn.astype(BF16)
        lmat = mask_ref[1] * (_dot_nt(kbeta.astype(BF16), knb) * decay)
        lmats.append(lmat)
        lbs.append(lmat.astype(BF16))
        tinvs.append(mask_ref[2] - lmat * mask_ref[3])
        e_cum = jnp.exp(g_cum)
        rhss.append(jnp.concatenate([av * beta, kbeta * e_cum], axis=1).astype(BF16))
        qk = (_dot_nt(qn.astype(BF16), knb) * decay).astype(BF16)
        qkc_ref[h] = _dot(qk, fold).astype(BF16)
        qg_ref[:, cols] = (qn * e_cum).astype(BF16)
        kd = (kn * jnp.exp(g_tot - g_cum)).astype(BF16)
        kdt_ref[h] = _dot_nt(eye_b, kd).astype(BF16)
        egl = jnp.exp(g_tot)
        egl_rows.append(jnp.concatenate(
            [egl[c * chunk:c * chunk + 1] for c in range(n_chunks)]
            + [jnp.zeros((8 - n_chunks, HEAD_DIM), F32)] * (n_chunks < 8), axis=0))
    egl_ref[0] = jnp.concatenate(egl_rows, axis=1)

    for k in range(1, n_levels):
        tbs = [tinvs[h].astype(BF16) for h in heads]
        mids = [_dot(tbs[h], lbs[h]).astype(BF16) for h in heads]
        tinvs = [tinvs[h] - mask_ref[3 + k] * _dot(mids[h], tbs[h]) for h in heads]
    resids = [mask_ref[2] - tinvs[h] - _dot_hp(lmats[h], tinvs[h]) for h in heads]
    tinvs = [tinvs[h] + _dot(tinvs[h].astype(BF16), resids[h].astype(BF16)) for h in heads]
    for h in heads:
        cols = slice(h * HEAD_DIM, (h + 1) * HEAD_DIM)
        uw = _dot(tinvs[h].astype(BF16), rhss[h])
        u_ref[:, cols] = uw[:, :HEAD_DIM]
        w_ref[:, cols] = uw[:, HEAD_DIM:].astype(BF16)


def _gdn_intra_prompt_kernel(x_ref, gates_ref, cw_ref, alog_ref, dtb_ref,
                             u_ref, w_ref, qg_ref, kdt_ref, qkc_ref, egl_ref, pad_ref, mask_ref,
                             *, rows, chunk):
    t = pl.program_id(1)

    @pl.when(t == 0)
    def _():
        pad_ref[0:8, :] = jnp.zeros((8, pad_ref.shape[1]), F32)
        _gdn_fill_masks(mask_ref, rows=rows, chunk=chunk)

    pad_ref[8:, :] = x_ref[...]
    cw = cw_ref[...]
    acc = pad_ref[pl.ds(8 - (CONV_W - 1), rows), :] * cw[0:1]
    for j in range(1, CONV_W):
        acc = acc + pad_ref[pl.ds(8 - (CONV_W - 1) + j, rows), :] * cw[j:j + 1]
    pad_ref[0:8, :] = x_ref[rows - 8:, :]
    _gdn_intra_all(_silu(acc), gates_ref[...], alog_ref[...], dtb_ref[...], mask_ref,
                   u_ref, w_ref, qg_ref, kdt_ref, qkc_ref, egl_ref, rows=rows, chunk=chunk)


def _gdn_intra_sample_kernel(x_ref, gates_ref, cw_ref, alog_ref, dtb_ref, buf_ref,
                             u_ref, w_ref, qg_ref, kdt_ref, qkc_ref, egl_ref, pad_ref, mask_ref,
                             *, n_seq, tn):
    seg = 8 + tn
    _gdn_fill_masks(mask_ref, rows=n_seq * tn, chunk=tn)
    cw = cw_ref[...]
    for s in range(n_seq):
        pad_ref[s * seg + 8 - (CONV_W - 1):s * seg + 8, :] = buf_ref[s]
        pad_ref[s * seg + 8:(s + 1) * seg, :] = x_ref[s * tn:(s + 1) * tn, :]
    pieces = []
    for s in range(n_seq):
        base = s * seg + 8 - (CONV_W - 1)
        acc = pad_ref[pl.ds(base, tn), :] * cw[0:1]
        for j in range(1, CONV_W):
            acc = acc + pad_ref[pl.ds(base + j, tn), :] * cw[j:j + 1]
        pieces.append(acc)
    _gdn_intra_all(_silu(jnp.concatenate(pieces, axis=0)), gates_ref[...], alog_ref[...], dtb_ref[...],
                   mask_ref, u_ref, w_ref, qg_ref, kdt_ref, qkc_ref, egl_ref, rows=n_seq * tn, chunk=tn)


def _intra_out(m, chunk, n_steps):
    width = N_HEADS * HEAD_DIM
    return [
        jax.ShapeDtypeStruct((m, width), F32),
        jax.ShapeDtypeStruct((m, width), BF16),
        jax.ShapeDtypeStruct((m, width), BF16),
        jax.ShapeDtypeStruct((N_HEADS, HEAD_DIM, m), BF16),
        jax.ShapeDtypeStruct((N_HEADS, m, chunk), BF16),
        jax.ShapeDtypeStruct((n_steps, 8, width), F32),
    ]


def _gdn_intra_prompt(proj, gates, conv_w, alog_row, dtb_row, *, batch, t):
    m = proj.shape[0]
    rows = GDN_ROWS
    nblk = t // rows
    width = N_HEADS * HEAD_DIM
    cwid = 3 * width

    def rb(b, i):
        return b * nblk + i

    in_specs = [
        pl.BlockSpec((rows, cwid), lambda b, i: (rb(b, i), 1)),
        pl.BlockSpec((rows, HEAD_DIM), lambda b, i: (rb(b, i), 0)),
        pl.BlockSpec((CONV_W, cwid), lambda b, i: (0, 0)),
        pl.BlockSpec((1, HEAD_DIM), lambda b, i: (0, 0)),
        pl.BlockSpec((1, HEAD_DIM), lambda b, i: (0, 0)),
    ]
    out_specs = [
        pl.BlockSpec((rows, width), lambda b, i: (rb(b, i), 0)),
        pl.BlockSpec((rows, width), lambda b, i: (rb(b, i), 0)),
        pl.BlockSpec((rows, width), lambda b, i: (rb(b, i), 0)),
        pl.BlockSpec((N_HEADS, HEAD_DIM, rows), lambda b, i: (0, 0, rb(b, i))),
        pl.BlockSpec((N_HEADS, rows, CHUNK), lambda b, i: (0, rb(b, i), 0)),
        pl.BlockSpec((1, 8, width), lambda b, i: (rb(b, i), 0, 0)),
    ]
    return pl.pallas_call(
        functools.partial(_gdn_intra_prompt_kernel, rows=rows, chunk=CHUNK),
        grid=(batch, nblk),
        in_specs=in_specs,
        out_specs=out_specs,
        out_shape=_intra_out(m, CHUNK, batch * nblk),
        scratch_shapes=[pltpu.VMEM((rows + 8, cwid), F32),
                        pltpu.VMEM((_gdn_n_masks(CHUNK), rows, rows), F32)],
        compiler_params=pltpu.CompilerParams(
            dimension_semantics=("parallel", "arbitrary"), vmem_limit_bytes=V7X_VMEM_LIMIT_BYTES),
        name="gdn_intra_prompt",
    )(proj, gates, conv_w, alog_row, dtb_row)


def _gdn_intra_sample(proj, gates, conv_w, alog_row, dtb_row, conv_state, *, n_seq, tn):
    m = proj.shape[0]
    width = N_HEADS * HEAD_DIM
    cwid = 3 * width
    in_specs = [
        pl.BlockSpec((m, cwid), lambda i: (0, 1)),
        pl.BlockSpec((m, HEAD_DIM), lambda i: (0, 0)),
        pl.BlockSpec((CONV_W, cwid), lambda i: (0, 0)),
        pl.BlockSpec((1, HEAD_DIM), lambda i: (0, 0)),
        pl.BlockSpec((1, HEAD_DIM), lambda i: (0, 0)),
        pl.BlockSpec((n_seq, CONV_W - 1, cwid), lambda i: (0, 0, 0)),
    ]
    out_specs = [
        pl.BlockSpec((m, width), lambda i: (0, 0)),
        pl.BlockSpec((m, width), lambda i: (0, 0)),
        pl.BlockSpec((m, width), lambda i: (0, 0)),
        pl.BlockSpec((N_HEADS, HEAD_DIM, m), lambda i: (0, 0, 0)),
        pl.BlockSpec((N_HEADS, m, tn), lambda i: (0, 0, 0)),
        pl.BlockSpec((1, 8, width), lambda i: (0, 0, 0)),
    ]
    return pl.pallas_call(
        functools.partial(_gdn_intra_sample_kernel, n_seq=n_seq, tn=tn),
        grid=(1,),
        in_specs=in_specs,
        out_specs=out_specs,
        out_shape=_intra_out(m, tn, 1),
        scratch_shapes=[pltpu.VMEM((n_seq * (8 + tn), cwid), F32),
                        pltpu.VMEM((_gdn_n_masks(tn), m, m), F32)],
        compiler_params=pltpu.CompilerParams(
            dimension_semantics=("arbitrary",), vmem_limit_bytes=V7X_VMEM_LIMIT_BYTES),
        name="gdn_intra_sample",
    )(proj, gates, conv_w, alog_row, dtb_row, conv_state)


def _gdn_scan_kernel(u_ref, w_ref, qg_ref, kdt_ref, qkc_ref, egl_ref, z_ref, s0_ref, dnw_ref,
                     o_ref, s_ref, *, chunk, n_chunks, seqs_per_step, egl_chunks):
    t = pl.program_id(1)

    @pl.when(t == 0)
    def _():
        s_ref[...] = s0_ref[...]

    dnw = dnw_ref[...]
    for c in range(n_chunks):
        si = c if seqs_per_step > 1 else 0
        rows = slice(c * chunk, (c + 1) * chunk)
        heads = range(N_HEADS)
        hcols = [slice(h * HEAD_DIM, (h + 1) * HEAD_DIM) for h in heads]
        s_old = [s_ref[si, h] for h in heads]
        both = [_dot(jnp.concatenate([w_ref[rows, hcols[h]], qg_ref[rows, hcols[h]]], axis=0),
                     s_old[h].astype(BF16)) for h in heads]
        vn = [(u_ref[rows, hcols[h]] - both[h][:chunk]).astype(BF16) for h in heads]
        o = [both[h][chunk:] + _dot(qkc_ref[h, rows, :], vn[h]) for h in heads]
        for h in heads:
            decay = egl_ref[c // egl_chunks, c % egl_chunks:c % egl_chunks + 1, hcols[h]]
            s_ref[si, h] = s_old[h] * decay + _dot(kdt_ref[h, :, rows], vn[h])
        for h in heads:
            o_ref[rows, hcols[h]] = (_rms(o[h], dnw) * _silu(z_ref[rows, hcols[h]])).astype(o_ref.dtype)


def _gdn_scan(intra, z_src, z_col_block, s0, dnw, *, n_seq, t, chunk, chunks_per_step, egl_chunks,
              seqs_per_step=1):
    u, w, qg, kdt, qkc, egl = intra
    m = u.shape[0]
    width = N_HEADS * HEAD_DIM
    rows = chunk * chunks_per_step
    if seqs_per_step > 1:
        assert seqs_per_step == n_seq == chunks_per_step and t == chunk
        nsteps, n_outer = 1, 1
    else:
        nsteps, n_outer = t // rows, n_seq

    def rb(b, i):
        return b * nsteps + i

    in_specs = [
        pl.BlockSpec((rows, width), lambda b, i: (rb(b, i), 0)),
        pl.BlockSpec((rows, width), lambda b, i: (rb(b, i), 0)),
        pl.BlockSpec((rows, width), lambda b, i: (rb(b, i), 0)),
        pl.BlockSpec((N_HEADS, HEAD_DIM, rows), lambda b, i: (0, 0, rb(b, i))),
        pl.BlockSpec((N_HEADS, rows, chunk), lambda b, i: (0, rb(b, i), 0)),
        pl.BlockSpec((chunks_per_step // egl_chunks, 8, width), lambda b, i: (rb(b, i), 0, 0)),
        pl.BlockSpec((rows, width), lambda b, i: (rb(b, i), z_col_block)),
        pl.BlockSpec((seqs_per_step, N_HEADS, HEAD_DIM, HEAD_DIM), lambda b, i: (b, 0, 0, 0)),
        pl.BlockSpec((1, HEAD_DIM), lambda b, i: (0, 0)),
    ]
    out_specs = [
        pl.BlockSpec((rows, width), lambda b, i: (rb(b, i), 0)),
        pl.BlockSpec((seqs_per_step, N_HEADS, HEAD_DIM, HEAD_DIM), lambda b, i: (b, 0, 0, 0)),
    ]
    return pl.pallas_call(
        functools.partial(_gdn_scan_kernel, chunk=chunk, n_chunks=chunks_per_step,
                          seqs_per_step=seqs_per_step, egl_chunks=egl_chunks),
        grid=(n_outer, nsteps),
        in_specs=in_specs,
        out_specs=out_specs,
        out_shape=[jax.ShapeDtypeStruct((m, width), BF16),
                   jax.ShapeDtypeStruct((n_seq, N_HEADS, HEAD_DIM, HEAD_DIM), F32)],
        compiler_params=pltpu.CompilerParams(
            dimension_semantics=("parallel", "arbitrary"), vmem_limit_bytes=V7X_VMEM_LIMIT_BYTES),
        name="gdn_scan",
    )(u, w, qg, kdt, qkc, egl, z_src, s0, dnw)


def _pad_lanes(v, offset):
    out = jnp.zeros((1, HEAD_DIM), F32)
    return lax.dynamic_update_slice(out, v.reshape(1, -1).astype(F32), (0, offset))


def kernel(x_prompt, x_sample, cache_k, cache_v, state_ssm, state_conv, page_table, ffn1_pre_w, ffn1_post_w, ffn1_gate, ffn1_up, ffn1_down, mix_pre_w, mix_post_w, w_in, conv_w, a_log, dt_bias, delta_norm_w, lambda_q1, lambda_k1, lambda_q2, lambda_k2, subln_w, w_out, ffn2_pre_w, ffn2_post_w, ffn2_gate, ffn2_up, ffn2_down):
    depth = w_in.shape[0]
    batch, t, d = x_prompt.shape
    n_seq, tn, _ = x_sample.shape
    a_w = N_HEADS * HEAD_DIM
    n_main = 3 * a_w + 3 * a_w + a_w
    yp = x_prompt.reshape(batch * t, d)
    ys = x_sample.reshape(n_seq * tn, d)
    outs = [[] for _ in range(8)]
    w_in_t = jnp.swapaxes(w_in, 1, 2)
    for l in range(depth):
        lam_init = 0.8 - 0.6 * math.exp(-0.3 * l)
        lam_rows = jnp.stack([lambda_q1[l], lambda_k1[l], lambda_q2[l], lambda_k2[l]]).astype(F32)
        row = lambda v: v[l].reshape(1, -1).astype(F32)
        w_gates = jnp.pad(w_in_t[l, n_main:], ((0, HEAD_DIM - 2 * N_HEADS), (0, 0))).astype(BF16)
        alog_row = _pad_lanes(a_log[l], N_HEADS)
        dtb_row = _pad_lanes(dt_bias[l], N_HEADS)
        dnw = row(delta_norm_w)
        subw = row(subln_w)
        cw = conv_w[l].astype(F32)
        ms = n_seq * tn

        ys, wg1, wu1, wd1 = _ffn(ys, row(ffn1_pre_w), row(ffn1_post_w), ffn1_gate[l], ffn1_up[l],
                                 ffn1_down[l], tm=ms, tf=512, emit_bf16=True)
        yp, hp = _ffn(yp, row(ffn1_pre_w), row(ffn1_post_w), wg1, wu1, wd1, tm=512, tf=512,
                      next_pre_w=row(mix_pre_w))

        ps, gs, ks, vs, w_main = _inproj(ys, row(mix_pre_w), w_in_t, w_gates, tm=ms, tn=a_w, n=n_main,
                                         layer=l, emit_bf16=True)
        qs = ps[:, :a_w].reshape(n_seq, tn * N_HEADS, HEAD_DIM)
        ks = ks.reshape(n_seq, tn * N_HEADS, HEAD_DIM)
        vs = vs.reshape(n_seq, tn * N_HEADS, HEAD_DIM)
        oa_s = _attn_sample(page_table, lam_rows, subw, qs, ks, vs, cache_k[l], cache_v[l],
                            lam_init=lam_init, group=16)
        oa_s = oa_s.reshape(ms, a_w)
        intra_s = _gdn_intra_sample(ps, gs, cw, alog_row, dtb_row, state_conv[l], n_seq=n_seq, tn=tn)
        ob_s, s_s = _gdn_scan(intra_s, ps, 6, state_ssm[l].astype(F32), dnw, n_seq=n_seq, t=tn,
                              chunk=tn, chunks_per_step=n_seq, egl_chunks=n_seq, seqs_per_step=n_seq)
        ys, wo = _outproj(ys, oa_s, ob_s, w_out[l], row(mix_post_w), tm=ms, emit_bf16=True)

        pp, gp, kp, vp = _inproj(hp, row(mix_pre_w), w_main, w_gates, tm=1024, tn=a_w, n=n_main)
        oa_p = _attn_prompt(lam_rows, subw.reshape(HEAD_DIM, 1), pp, kp, vp, batch=batch, t=t,
                            lam_init=lam_init, tq=256, hpp=2)
        intra_p = _gdn_intra_prompt(pp, gp, cw, alog_row, dtb_row, batch=batch, t=t)
        zero_s = jnp.zeros((batch, N_HEADS, HEAD_DIM, HEAD_DIM), F32)
        ob_p, s_p = _gdn_scan(intra_p, pp, 6, zero_s, dnw, n_seq=batch, t=t, chunk=CHUNK,
                              chunks_per_step=GDN_SCAN_ROWS // CHUNK, egl_chunks=GDN_ROWS // CHUNK)
        yp = _outproj(yp, oa_p, ob_p, wo, row(mix_post_w), tm=512)

        ys, wg2, wu2, wd2 = _ffn(ys, row(ffn2_pre_w), row(ffn2_post_w), ffn2_gate[l], ffn2_up[l],
                                 ffn2_down[l], tm=ms, tf=512, emit_bf16=True)
        yp = _ffn(yp, row(ffn2_pre_w), row(ffn2_post_w), wg2, wu2, wd2, tm=512, tf=512)

        ppb = pp.reshape(batch, t, -1)
        psb = ps.reshape(n_seq, tn, -1)
        outs[0].append(kp.reshape(batch, t, N_HEADS, HEAD_DIM))
        outs[1].append(vp.reshape(batch, t, N_HEADS, HEAD_DIM))
        outs[2].append(s_p.astype(state_ssm.dtype))
        outs[3].append(ppb[:, t - (CONV_W - 1):, 3 * a_w:6 * a_w])
        outs[4].append(ks.reshape(n_seq, tn, N_HEADS, HEAD_DIM))
        outs[5].append(vs.reshape(n_seq, tn, N_HEADS, HEAD_DIM))
        outs[6].append(s_s.astype(state_ssm.dtype))
        outs[7].append(jnp.concatenate([state_conv[l].astype(psb.dtype), psb[:, :, 3 * a_w:6 * a_w]],
                                       axis=1)[:, tn:])
    return (yp.reshape(batch, t, d), ys.reshape(n_seq, tn, d)) + tuple(jnp.stack(o) for o in outs)
```
